```python
import math
import jax, jax.numpy as jnp
from jax import lax
import numpy as np

D_MODEL = 1024
BATCH = 8
SEQ = 4096
DEPTH = 2

N_MIXERS = 2
N_SSM_LAYERS = (DEPTH + N_MIXERS - 1) // N_MIXERS
N_ATTN_LAYERS = DEPTH // N_MIXERS

SSM_EXPAND = 2
SSM_D_INNER = SSM_EXPAND * D_MODEL
SSM_HEAD_DIM = 64
SSM_HEADS = SSM_D_INNER // SSM_HEAD_DIM
SSM_STATE = 128
SSM_GROUPS = 8
SSM_HEADS_PER_GROUP = SSM_HEADS // SSM_GROUPS
SSM_CONV = 4
SSM_CHUNK = 256
SSM_CONV_DIM = SSM_D_INNER + 2 * SSM_GROUPS * SSM_STATE
SSM_IN_DIM = SSM_D_INNER + SSM_CONV_DIM + SSM_HEADS

ATTN_HEADS = 16
ATTN_HEAD_DIM = D_MODEL // ATTN_HEADS
MOBA_BLOCK = 256
MOBA_TOPK = 3
Q_BLOCK = 128

N_EXPERTS = 32
N_EXPERT_GROUPS = 4
EXPERTS_PER_GROUP = N_EXPERTS // N_EXPERT_GROUPS
TOP_K = 2
D_EXPERT = 512

DEEPNORM_ALPHA = (2.0 * DEPTH) ** 0.25
DEEPNORM_BETA = (8.0 * DEPTH) ** -0.25
LN_EPS = 1e-5
RMS_EPS = 1e-5
NEG_INF = -1e30

kernel_name = "hybrid_ssd_moba_grouped_moe_deepnorm"


def layer_norm(x, g, b):
    xf = x.astype(jnp.float32)
    mu = jnp.mean(xf, -1, keepdims=True)
    var = jnp.mean(jnp.square(xf - mu), -1, keepdims=True)
    return ((xf - mu) * lax.rsqrt(var + LN_EPS) * g + b).astype(x.dtype)


def pad_seq(h, m):
    L = h.shape[1]
    Lp = -(-L // m) * m
    return jnp.pad(h, ((0, 0), (0, Lp - L), (0, 0)))


def causal_depthwise_conv(u, w, bias):
    K = w.shape[0]
    out = lax.conv_general_dilated(u, w[:, None, :], window_strides=(1,), padding=[(K - 1, 0)],
                                   dimension_numbers=('NWC', 'WIO', 'NWC'),
                                   feature_group_count=u.shape[-1])
    return out + bias


def ssd_chunked_scan(xs, dt, a, bm, cm):
    b, Lp, G, R, P = xs.shape
    N = bm.shape[-1]
    nc = Lp // SSM_CHUNK

    def chunks(t):
        return jnp.moveaxis(t.reshape(b, nc, SSM_CHUNK, *t.shape[2:]), 1, 0)

    causal = jnp.tril(jnp.ones((SSM_CHUNK, SSM_CHUNK), dtype=bool))

    def step(state, inp):
        x_c, dt_c, b_c, c_c = inp
        acum = jnp.cumsum(dt_c * a, axis=1)
        seg = acum[:, :, None] - acum[:, None, :]
        decay = jnp.exp(jnp.where(causal[None, :, :, None, None], seg, -jnp.inf))
        w = jnp.einsum('btgn,bsgn->btsg', c_c, b_c)[..., None] * decay * dt_c[:, None]
        y = jnp.einsum('btsgr,bsgrp->btgrp', w, x_c)
        y = y + jnp.einsum('btgn,bgrpn->btgrp', c_c, state) * jnp.exp(acum)[..., None]
        to_end = jnp.exp(acum[:, -1:] - acum) * dt_c
        state = (state * jnp.exp(acum[:, -1])[..., None, None]
                 + jnp.einsum('bsgn,bsgr,bsgrp->bgrpn', b_c, to_end, x_c))
        return state, y

    state0 = jnp.zeros((b, G, R, P, N), jnp.float32)
    _, ys = lax.scan(step, state0, (chunks(xs), chunks(dt), chunks(bm), chunks(cm)))
    return jnp.moveaxis(ys, 0, 1).reshape(b, Lp, G, R, P)


def gated_group_rmsnorm(y, z, w):
    g = y * jax.nn.silu(z.astype(jnp.float32))
    gs = g.reshape(*g.shape[:-1], SSM_GROUPS, -1)
    gs = gs * lax.rsqrt(jnp.mean(gs * gs, -1, keepdims=True) + RMS_EPS)
    return gs.reshape(g.shape) * w


def mamba2_mixer(h, w_in, conv_w, conv_b, dt_bias, a_log, d_skip, norm_w, w_out):
    b, L, _ = h.shape
    hp = pad_seq(h, SSM_CHUNK)
    Lp = hp.shape[1]
    zxbcdt = hp @ w_in
    z, xbc, dt = jnp.split(zxbcdt, [SSM_D_INNER, SSM_D_INNER + SSM_CONV_DIM], axis=-1)
    xbc = jax.nn.silu(causal_depthwise_conv(xbc, conv_w, conv_b)).astype(jnp.float32)
    xs, bm, cm = jnp.split(xbc, [SSM_D_INNER, SSM_D_INNER + SSM_GROUPS * SSM_STATE], axis=-1)
    xs = xs.reshape(b, Lp, SSM_GROUPS, SSM_HEADS_PER_GROUP, SSM_HEAD_DIM)
    bm = bm.reshape(b, Lp, SSM_GROUPS, SSM_STATE)
    cm = cm.reshape(b, Lp, SSM_GROUPS, SSM_STATE)
    dt = jax.nn.softplus(dt.astype(jnp.float32) + dt_bias.astype(jnp.float32))
    dt = dt.reshape(b, Lp, SSM_GROUPS, SSM_HEADS_PER_GROUP)
    a = -jnp.exp(a_log.astype(jnp.float32)).reshape(SSM_GROUPS, SSM_HEADS_PER_GROUP)
    y = ssd_chunked_scan(xs, dt, a, bm, cm)
    y = y + d_skip.astype(jnp.float32).reshape(SSM_GROUPS, SSM_HEADS_PER_GROUP, 1) * xs
    y = gated_group_rmsnorm(y.reshape(b, Lp, SSM_D_INNER), z, norm_w.astype(jnp.float32))
    return (y.astype(h.dtype) @ w_out)[:, :L]


def moba_mixer(h, w_qkv, w_o):
    b, L, _ = h.shape
    hp = pad_seq(h, MOBA_BLOCK)
    Lp = hp.shape[1]
    nb = Lp // MOBA_BLOCK
    nqc = Lp // Q_BLOCK
    topk = min(MOBA_TOPK, nb)
    scale = ATTN_HEAD_DIM ** -0.5
    q, k, v = jnp.split(hp @ w_qkv, 3, axis=-1)

    def heads(t):
        return t.reshape(b, Lp, ATTN_HEADS, ATTN_HEAD_DIM).transpose(0, 2, 1, 3)

    q, k, v = heads(q), heads(k), heads(v)
    kb = k.reshape(b, ATTN_HEADS, nb, MOBA_BLOCK, ATTN_HEAD_DIM)
    vb = v.reshape(b, ATTN_HEADS, nb, MOBA_BLOCK, ATTN_HEAD_DIM)
    kmean = jnp.mean(kb.astype(jnp.float32), axis=3)
    qc = q.reshape(b, ATTN_HEADS, nqc, Q_BLOCK, ATTN_HEAD_DIM).transpose(0, 2, 1, 3, 4)

    def per_batch(args):
        q_b, kb_b, vb_b, km_b = args

        def per_chunk(cargs):
            q_c, ci = cargs
            q_pos = ci * Q_BLOCK + jnp.arange(Q_BLOCK)
            own = (ci * Q_BLOCK) // MOBA_BLOCK
            gate = jnp.einsum('hqd,hnd->hqn', q_c.astype(jnp.float32), km_b)
            past = jnp.arange(nb) < own
            gate = jnp.where(past[None, None, :], gate, NEG_INF)
            _, sel = lax.top_k(gate, topk)
            sel_valid = sel < own
            k_sel = jax.vmap(lambda kh, ih: kh[ih])(kb_b, sel)
            v_sel = jax.vmap(lambda vh, ih: vh[ih])(vb_b, sel)
            s_sel = jnp.einsum('hqd,hqksd->hqks', q_c, k_sel).astype(jnp.float32) * scale
            s_sel = jnp.where(sel_valid[..., None], s_sel, NEG_INF)
            s_sel = s_sel.reshape(ATTN_HEADS, Q_BLOCK, topk * MOBA_BLOCK)
            k_own = lax.dynamic_index_in_dim(kb_b, own, axis=1, keepdims=False)
            v_own = lax.dynamic_index_in_dim(vb_b, own, axis=1, keepdims=False)
            s_own = jnp.einsum('hqd,hsd->hqs', q_c, k_own).astype(jnp.float32) * scale
            k_pos = own * MOBA_BLOCK + jnp.arange(MOBA_BLOCK)
            s_own = jnp.where(k_pos[None, None, :] <= q_pos[None, :, None], s_own, NEG_INF)
            p = jax.nn.softmax(jnp.concatenate([s_sel, s_own], axis=-1), axis=-1)
            p_sel = p[..., :topk * MOBA_BLOCK].reshape(ATTN_HEADS, Q_BLOCK, topk, MOBA_BLOCK)
            p_own = p[..., topk * MOBA_BLOCK:]
            o = (jnp.einsum('hqks,hqksd->hqd', p_sel, v_sel.astype(jnp.float32))
                 + jnp.einsum('hqs,hsd->hqd', p_own, v_own.astype(jnp.float32)))
            return o.astype(h.dtype)

        return lax.map(per_chunk, (q_b, jnp.arange(nqc)))

    o = lax.map(per_batch, (qc, kb, vb, kmean))
    o = o.transpose(0, 1, 3, 2, 4).reshape(b, Lp, D_MODEL)
    return o[:, :L] @ w_o


def grouped_moe(h, w_router, b_router, w_gate, w_up, w_down):
    b, L, D = h.shape
    t = h.reshape(-1, D)
    T = t.shape[0]
    logits = (t @ w_router).astype(jnp.float32) + b_router.astype(jnp.float32)
    probs = jax.nn.softmax(logits, axis=-1)
    pg = probs.reshape(T, N_EXPERT_GROUPS, EXPERTS_PER_GROUP)
    group_score = jnp.sum(lax.top_k(pg, TOP_K)[0], axis=-1)
    g_sel = jnp.argmax(group_score, axis=-1)
    in_group = jnp.take_along_axis(pg, g_sel[:, None, None], axis=1)[:, 0]
    top_p, top_local = lax.top_k(in_group, TOP_K)
    expert_idx = g_sel[:, None] * EXPERTS_PER_GROUP + top_local
    gates = top_p / jnp.sum(top_p, axis=-1, keepdims=True)
    flat_e = expert_idx.reshape(-1)
    order = jnp.argsort(flat_e)
    tok = order // TOP_K
    xs = t[tok]
    sizes = jnp.bincount(flat_e, length=N_EXPERTS).astype(jnp.int32)
    hg = lax.ragged_dot(xs, w_gate, sizes)
    hu = lax.ragged_dot(xs, w_up, sizes)
    ye = lax.ragged_dot(jax.nn.silu(hg) * hu, w_down, sizes)
    ye = ye * gates.reshape(-1)[order][:, None].astype(ye.dtype)
    y = jax.ops.segment_sum(ye, tok, num_segments=T)
    return y.reshape(b, L, D)


def setup_inputs(seed: int = 0) -> dict:
    key = jax.random.key(seed)
    ks = jax.random.split(key, 24)
    f32 = jnp.float32
    D = D_MODEL
    nrm = lambda k, s: jax.random.normal(k, s, f32)
    x = nrm(ks[0], (BATCH, SEQ, D))
    c = nrm(ks[1], (BATCH, D))
    w_ada = nrm(ks[2], (DEPTH, D, 6 * D)) * (0.1 * D ** -0.5)
    b_ada = nrm(ks[3], (DEPTH, 6 * D)) * 0.02
    ln_g = 1.0 + 0.02 * nrm(ks[4], (DEPTH, 2, D))
    ln_b = 0.02 * nrm(ks[5], (DEPTH, 2, D))
    ssm_w_in = nrm(ks[6], (N_SSM_LAYERS, D, SSM_IN_DIM)) * D ** -0.5
    ssm_conv_w = nrm(ks[7], (N_SSM_LAYERS, SSM_CONV, SSM_CONV_DIM)) * SSM_CONV ** -0.5
    ssm_conv_b = 0.02 * nrm(ks[8], (N_SSM_LAYERS, SSM_CONV_DIM))
    dt0 = jnp.exp(jax.random.uniform(ks[9], (N_SSM_LAYERS, SSM_HEADS), f32)
                  * (math.log(0.1) - math.log(0.001)) + math.log(0.001))
    ssm_dt_bias = dt0 + jnp.log(-jnp.expm1(-dt0))
    ssm_a_log = jnp.log(jax.random.uniform(ks[10], (N_SSM_LAYERS, SSM_HEADS), f32, 1.0, 16.0))
    ssm_d = 1.0 + 0.1 * nrm(ks[11], (N_SSM_LAYERS, SSM_HEADS))
    ssm_norm_w = 1.0 + 0.02 * nrm(ks[12], (N_SSM_LAYERS, SSM_D_INNER))
    ssm_w_out = nrm(ks[13], (N_SSM_LAYERS, SSM_D_INNER, D)) * (SSM_D_INNER ** -0.5 * DEEPNORM_BETA)
    qkv_scale = jnp.concatenate([jnp.ones((2 * D,), f32), jnp.full((D,), DEEPNORM_BETA, f32)])
    attn_w_qkv = nrm(ks[14], (N_ATTN_LAYERS, D, 3 * D)) * D ** -0.5 * qkv_scale
    attn_w_o = nrm(ks[15], (N_ATTN_LAYERS, D, D)) * (D ** -0.5 * DEEPNORM_BETA)
    w_router = nrm(ks[16], (D, N_EXPERTS)) * D ** -0.5
    b_router = 0.01 * nrm(ks[17], (N_EXPERTS,))
    moe_w_gate = nrm(ks[18], (DEPTH, N_EXPERTS, D, D_EXPERT)) * D ** -0.5
    moe_w_up = nrm(ks[19], (DEPTH, N_EXPERTS, D, D_EXPERT)) * D ** -0.5
    moe_w_down = nrm(ks[20], (DEPTH, N_EXPERTS, D_EXPERT, D)) * (D_EXPERT ** -0.5 * DEEPNORM_BETA)
    return {"x": x, "c": c, "w_ada": w_ada, "b_ada": b_ada, "ln_g": ln_g, "ln_b": ln_b,
            "ssm_w_in": ssm_w_in, "ssm_conv_w": ssm_conv_w, "ssm_conv_b": ssm_conv_b,
            "ssm_dt_bias": ssm_dt_bias, "ssm_a_log": ssm_a_log, "ssm_d": ssm_d,
            "ssm_norm_w": ssm_norm_w, "ssm_w_out": ssm_w_out,
            "attn_w_qkv": attn_w_qkv, "attn_w_o": attn_w_o,
            "w_router": w_router, "b_router": b_router,
            "moe_w_gate": moe_w_gate, "moe_w_up": moe_w_up, "moe_w_down": moe_w_down}


def reference(x, c, w_ada, b_ada, ln_g, ln_b, ssm_w_in, ssm_conv_w, ssm_conv_b, ssm_dt_bias,
              ssm_a_log, ssm_d, ssm_norm_w, ssm_w_out, attn_w_qkv, attn_w_o, w_router, b_router,
              moe_w_gate, moe_w_up, moe_w_down):
    cs = jax.nn.silu(c)
    for i in range(DEPTH):
        mod = cs @ w_ada[i] + b_ada[i]
        sh1, sc1, g1, sh2, sc2, g2 = jnp.split(mod, 6, axis=-1)
        hm = x * (1.0 + sc1[:, None]) + sh1[:, None]
        j = i // N_MIXERS
        if i % N_MIXERS == 0:
            y = mamba2_mixer(hm, ssm_w_in[j], ssm_conv_w[j], ssm_conv_b[j], ssm_dt_bias[j],
                             ssm_a_log[j], ssm_d[j], ssm_norm_w[j], ssm_w_out[j])
        else:
            y = moba_mixer(hm, attn_w_qkv[j], attn_w_o[j])
        x = layer_norm(DEEPNORM_ALPHA * x + (1.0 + g1[:, None]) * y, ln_g[i, 0], ln_b[i, 0])
        hm = x * (1.0 + sc2[:, None]) + sh2[:, None]
        y = grouped_moe(hm, w_router, b_router, moe_w_gate[i], moe_w_up[i], moe_w_down[i])
        x = layer_norm(DEEPNORM_ALPHA * x + (1.0 + g2[:, None]) * y, ln_g[i, 1], ln_b[i, 1])
    return x
```

```python
import functools

import jax
import jax.numpy as jnp
from jax import lax
from jax.experimental import pallas as pl
from jax.experimental.pallas import tpu as pltpu

F32 = jnp.float32
BF16 = jnp.bfloat16

V7X_LANES = 128
V7X_SUBLANES = 8
V7X_VMEM_BYTES = 64 * 1024 * 1024

DEPTH = 2
SSM_HEAD_DIM = 64
SSM_STATE = 128
SSM_GROUPS = 8
SSM_HEADS_PER_GROUP = 4
SSM_CONV = 4
SSM_CHUNK = 256
ATTN_HEADS = 16
ATTN_HEAD_DIM = 64
MOBA_BLOCK = 256
MOBA_TOPK = 3
N_EXPERTS = 32
N_EXPERT_GROUPS = 4
EXPERTS_PER_GROUP = N_EXPERTS // N_EXPERT_GROUPS
DEEPNORM_ALPHA = (2.0 * DEPTH) ** 0.25
LN_EPS = 1e-5
RMS_EPS = 1e-5
NEG_INF = -1e30

GROUP_W = SSM_HEADS_PER_GROUP * SSM_HEAD_DIM
CONV_HALO = V7X_SUBLANES
MOE_ROW_TILE = 256
TOKEN_TILE = 256


def _cparams(semantics, vmem_mib):
    assert vmem_mib * 1024 * 1024 <= V7X_VMEM_BYTES
    return pltpu.CompilerParams(dimension_semantics=semantics, vmem_limit_bytes=vmem_mib * 1024 * 1024)


def _sds(shape, dtype):
    return jax.ShapeDtypeStruct(shape, dtype)


def _dot(a, b):
    return jnp.dot(a, b, preferred_element_type=F32)


def _dot_nt(a, b):
    return lax.dot_general(a, b, (((1,), (1,)), ((), ())), preferred_element_type=F32)


def _dot_tn(a, b):
    return lax.dot_general(a, b, (((0,), (0,)), ((), ())), preferred_element_type=F32)


def _split3(a):
    hi = a.astype(BF16)
    r1 = a - hi.astype(F32)
    mid = r1.astype(BF16)
    lo = (r1 - mid.astype(F32)).astype(BF16)
    return hi, mid, lo


def _silu(x):
    return x * jax.nn.sigmoid(x)


def _layer_norm(v, gamma, beta):
    mu = jnp.mean(v, axis=-1, keepdims=True)
    d = v - mu
    var = jnp.mean(d * d, axis=-1, keepdims=True)
    return d * lax.rsqrt(var + LN_EPS) * gamma + beta


def _ada_kernel(c_ref, w_ref, b_ref, o_ref):
    cs = _silu(c_ref[...])
    o_ref[0] = jnp.dot(cs, w_ref[0], preferred_element_type=F32, precision=lax.Precision.HIGHEST) + b_ref[0]


def _ada_mod(c, w_ada, b_ada):
    depth, d, n = w_ada.shape
    b = c.shape[0]
    tn = 1024
    out = pl.pallas_call(
        _ada_kernel,
        out_shape=_sds((depth, b, n), F32),
        grid=(depth, n // tn),
        in_specs=[pl.BlockSpec((b, d), lambda l, j: (0, 0)),
                  pl.BlockSpec((1, d, tn), lambda l, j: (l, 0, j)),
                  pl.BlockSpec((1, 1, tn), lambda l, j: (l, 0, j))],
        out_specs=pl.BlockSpec((1, b, tn), lambda l, j: (l, 0, j)),
        compiler_params=_cparams(("arbitrary", "arbitrary"), 32),
        name="ada_mod",
    )(c, w_ada, b_ada.reshape(depth, 1, n))
    return out.reshape(depth, b, 6, d)


def _modulate_into(hm_ref, x_ref, mod_ref, sc, sh):
    m = mod_ref[0]
    hm_ref[...] = (x_ref[...] * (1.0 + m[sc:sc + 1, :]) + m[sh:sh + 1, :]).astype(BF16)


def _mm_mod_kernel(x_ref, mod_ref, w_ref, o_ref, hm_ref, *, sc, sh):
    @pl.when(pl.program_id(1) == 0)
    def _():
        _modulate_into(hm_ref, x_ref, mod_ref, sc, sh)

    o_ref[...] = _dot(hm_ref[...], w_ref[...]).astype(o_ref.dtype)


def _mm_mod(x, mod, w, *, sc, sh, seq_len, out_dtype, tm=512, tn=512):
    t, k = x.shape
    n = w.shape[1]
    tn = min(tn, n)
    tiles_per_batch = seq_len // tm
    return pl.pallas_call(
        functools.partial(_mm_mod_kernel, sc=sc, sh=sh),
        out_shape=_sds((t, n), out_dtype),
        grid=(t // tm, n // tn),
        in_specs=[pl.BlockSpec((tm, k), lambda i, j: (i, 0)),
                  pl.BlockSpec((1, 6, k), lambda i, j: (i // tiles_per_batch, 0, 0)),
                  pl.BlockSpec((k, tn), lambda i, j: (0, j))],
        out_specs=pl.BlockSpec((tm, tn), lambda i, j: (i, j)),
        scratch_shapes=[pltpu.VMEM((tm, k), BF16)],
        compiler_params=_cparams(("arbitrary", "arbitrary"), 40),
        name="mm_mod",
    )(x, mod, w)


def _ssd_kernel(z_ref, xp_ref, bcp_ref, dtr_ref, cw_ref, cb_ref, dtb_ref, alog_ref, dsk_ref, nw_ref,
                o_ref, ubuf, act, state, acum_t):
    lc = SSM_CHUNK
    d_inner = xp_ref.shape[1]
    conv_dim = ubuf.shape[1]
    chunk = pl.program_id(1)

    @pl.when(chunk == 0)
    def _():
        ubuf[0:CONV_HALO, :] = jnp.zeros((CONV_HALO, conv_dim), F32)
        state[...] = jnp.zeros_like(state)

    ubuf[CONV_HALO:CONV_HALO + lc, 0:d_inner] = xp_ref[...].astype(F32)
    ubuf[CONV_HALO:CONV_HALO + lc, d_inner:conv_dim] = bcp_ref[...].astype(F32)
    strip = 512
    for s in range(0, conv_dim, strip):
        acc = cb_ref[:, s:s + strip]
        for k in range(SSM_CONV):
            off = CONV_HALO - (SSM_CONV - 1) + k
            acc = acc + cw_ref[k:k + 1, s:s + strip] * ubuf[off:off + lc, s:s + strip]
        act[:, s:s + strip] = _silu(acc)
    ubuf[0:CONV_HALO, :] = ubuf[lc:lc + CONV_HALO, :]

    dt = jax.nn.softplus(dtr_ref[...] + dtb_ref[...])
    da = dt * (-jnp.exp(alog_ref[...]))
    row = lax.broadcasted_iota(jnp.int32, (lc, lc), 0)
    col = lax.broadcasted_iota(jnp.int32, (lc, lc), 1)
    causal = col <= row
    tril = causal.astype(BF16)
    acum = sum(_dot(tril, part) for part in _split3(da))
    acum_t[...] = acum.T
    acum_parts = _split3(acum)
    dt_parts = _split3(dt)
    head_of_col = lax.broadcasted_iota(jnp.int32, (lc, GROUP_W), 1) // SSM_HEAD_DIM
    sel_row = lax.broadcasted_iota(jnp.int32, (V7X_LANES, GROUP_W), 0)
    sel_col = lax.broadcasted_iota(jnp.int32, (V7X_LANES, GROUP_W), 1) // SSM_HEAD_DIM

    def group_body(g, carry):
        xo = pl.multiple_of(g * GROUP_W, GROUP_W)
        bo = pl.multiple_of(d_inner + g * SSM_STATE, SSM_STATE)
        co = pl.multiple_of(d_inner + SSM_GROUPS * SSM_STATE + g * SSM_STATE, SSM_STATE)
        x_g = act[:, pl.ds(xo, GROUP_W)]
        b_g = act[:, pl.ds(bo, SSM_STATE)].astype(BF16)
        c_g = act[:, pl.ds(co, SSM_STATE)].astype(BF16)
        sel = (sel_row == SSM_HEADS_PER_GROUP * g + sel_col).astype(BF16)
        ab = sum(_dot(part, sel) for part in acum_parts)
        dtb = sum(_dot(part, sel) for part in dt_parts)
        alast = ab[lc - 1:lc, :]
        cb = _dot_nt(c_g, b_g)
        xdt = x_g * dtb
        y = jnp.zeros((lc, GROUP_W), F32)
        for r in range(SSM_HEADS_PER_GROUP):
            acol = ab[:, r * SSM_HEAD_DIM:r * SSM_HEAD_DIM + 1]
            arow = acum_t[pl.ds(SSM_HEADS_PER_GROUP * g + r, 1), :]
            decay = jnp.exp(jnp.where(causal, acol - arow, -jnp.inf))
            w = (cb * decay).astype(BF16)
            x_r = jnp.where(head_of_col == r, xdt, 0.0).astype(BF16)
            y = y + _dot(w, x_r)
        st = state[:, pl.ds(xo, GROUP_W)]
        y = y + _dot(c_g, st.astype(BF16)) * jnp.exp(ab)
        to_end = jnp.exp(alast - ab) * dtb
        xw = (x_g * to_end).astype(BF16)
        state[:, pl.ds(xo, GROUP_W)] = st * jnp.exp(alast) + _dot_tn(b_g, xw)
        y = y + dsk_ref[:, pl.ds(xo, GROUP_W)] * x_g
        gated = y * _silu(z_ref[:, pl.ds(xo, GROUP_W)].astype(F32))
        ms = jnp.mean(gated * gated, axis=-1, keepdims=True)
        o_ref[:, pl.ds(xo, GROUP_W)] = (gated * lax.rsqrt(ms + RMS_EPS) * nw_ref[:, pl.ds(xo, GROUP_W)]).astype(BF16)
        return carry

    lax.fori_loop(0, SSM_GROUPS, group_body, 0)


def _ssd(proj, dt_raw, conv_w, conv_b, dt_bias, a_log, d_skip, norm_w, *, batch, seq_len):
    t = proj.shape[0]
    d_inner = SSM_GROUPS * GROUP_W
    conv_dim = d_inner + 2 * SSM_GROUPS * SSM_STATE
    assert proj.shape[1] == d_inner + conv_dim and conv_dim == 2 * d_inner
    nc = seq_len // SSM_CHUNK
    lc = SSM_CHUNK
    pad = V7X_LANES - dt_bias.shape[0]
    heads = dt_bias.shape[0]
    row_map = lambda b, c: b * nc + c
    const = lambda b, c: (0, 0)
    return pl.pallas_call(
        _ssd_kernel,
        out_shape=_sds((t, d_inner), BF16),
        grid=(batch, nc),
        in_specs=[pl.BlockSpec((lc, d_inner), lambda b, c: (row_map(b, c), 0)),
                  pl.BlockSpec((lc, d_inner), lambda b, c: (row_map(b, c), 1)),
                  pl.BlockSpec((lc, d_inner), lambda b, c: (row_map(b, c), 2)),
                  pl.BlockSpec((lc, V7X_LANES), lambda b, c: (row_map(b, c), 0)),
                  pl.BlockSpec((SSM_CONV, conv_dim), const),
                  pl.BlockSpec((1, conv_dim), const),
                  pl.BlockSpec((1, V7X_LANES), const),
                  pl.BlockSpec((1, V7X_LANES), const),
                  pl.BlockSpec((1, d_inner), const),
                  pl.BlockSpec((1, d_inner), const)],
        out_specs=pl.BlockSpec((lc, d_inner), lambda b, c: (row_map(b, c), 0)),
        scratch_shapes=[pltpu.VMEM((CONV_HALO + lc, conv_dim), F32),
                        pltpu.VMEM((lc, conv_dim), F32),
                        pltpu.VMEM((SSM_STATE, d_inner), F32),
                        pltpu.VMEM((V7X_LANES, lc), F32)],
        compiler_params=_cparams(("arbitrary", "arbitrary"), 48),
        name="ssd_chunk_scan",
    )(proj, proj, proj, dt_raw, conv_w, conv_b.reshape(1, conv_dim),
      jnp.pad(dt_bias, (0, pad)).reshape(1, V7X_LANES), jnp.pad(a_log, (0, pad)).reshape(1, V7X_LANES),
      jnp.repeat(d_skip, SSM_HEAD_DIM).reshape(1, heads * SSM_HEAD_DIM), norm_w.reshape(1, d_inner))


def _route(logits, carry_ref):
    tm = logits.shape[0]
    lane = lax.broadcasted_iota(jnp.int32, (tm, V7X_LANES), 1)
    lane_f = lane.astype(F32)
    lg = jnp.where(lane < N_EXPERTS, logits, -jnp.inf)
    e = jnp.exp(lg - jnp.max(lg, axis=-1, keepdims=True))
    far = float(V7X_LANES)
    best = None
    for g in range(N_EXPERT_GROUPS):
        in_g = (lane >= g * EXPERTS_PER_GROUP) & (lane < (g + 1) * EXPERTS_PER_GROUP)
        eg = jnp.where(in_g, e, -1.0)
        m1 = jnp.max(eg, axis=-1, keepdims=True)
        i1 = jnp.min(jnp.where(eg == m1, lane_f, far), axis=-1, keepdims=True)
        eg2 = jnp.where(lane_f == i1, -1.0, eg)
        m2 = jnp.max(eg2, axis=-1, keepdims=True)
        i2 = jnp.min(jnp.where(eg2 == m2, lane_f, far), axis=-1, keepdims=True)
        cand = (m1 + m2, m1, m2, i1, i2)
        if best is None:
            best = cand
        else:
            take = cand[0] > best[0]
            best = tuple(jnp.where(take, c, b) for c, b in zip(cand, best))
    _, m1, m2, i1, i2 = best
    denom = m1 + m2
    hit1 = lane_f == i1
    hit2 = lane_f == i2
    onehot = (hit1 | hit2).astype(BF16)
    row = lax.broadcasted_iota(jnp.int32, (tm, tm), 0)
    col = lax.broadcasted_iota(jnp.int32, (tm, tm), 1)
    before = (col < row).astype(BF16)
    rank = _dot(before, onehot) + carry_ref[0:1, :]
    r1 = jnp.sum(jnp.where(hit1, rank, 0.0), axis=-1, keepdims=True)
    r2 = jnp.sum(jnp.where(hit2, rank, 0.0), axis=-1, keepdims=True)
    carry_ref[0:1, :] = carry_ref[0:1, :] + jnp.sum(onehot.astype(F32), axis=0, keepdims=True)
    rec = jnp.zeros((tm, V7X_LANES), F32)
    for k, val in enumerate((i1, i2, m1 / denom, m2 / denom, r1, r2)):
        rec = jnp.where(lane == k, val, rec)
    return rec


def _proj_ln_route_kernel(a_ref, w_ref, xres_ref, mod_ref, lng_ref, lnb_ref, wr_ref, br_ref,
                          x_ref, route_ref, cnt_ref, carry_ref, *, gate_idx, sc, sh):
    @pl.when(pl.program_id(0) == 0)
    def _():
        carry_ref[...] = jnp.zeros_like(carry_ref)

    m = mod_ref[0]
    y = _dot(a_ref[...], w_ref[...])
    v = DEEPNORM_ALPHA * xres_ref[...] + (1.0 + m[gate_idx:gate_idx + 1, :]) * y
    x1 = _layer_norm(v, lng_ref[...], lnb_ref[...])
    x_ref[...] = x1
    hm = (x1 * (1.0 + m[sc:sc + 1, :]) + m[sh:sh + 1, :]).astype(BF16)
    logits = _dot(hm, wr_ref[...]) + br_ref[...]
    route_ref[...] = _route(logits, carry_ref)
    cnt_ref[...] = carry_ref[...]


def _proj_ln_route(a, w, xres, mod, ln_g, ln_b, w_router, b_router, *, seq_len):
    t, k = a.shape
    d = w.shape[1]
    tm = TOKEN_TILE
    tiles_per_batch = seq_len // tm
    const = lambda i: (0, 0)
    wr = jnp.pad(w_router, ((0, 0), (0, V7X_LANES - N_EXPERTS))).astype(BF16)
    br = jnp.pad(b_router, (0, V7X_LANES - N_EXPERTS)).reshape(1, V7X_LANES)
    return pl.pallas_call(
        functools.partial(_proj_ln_route_kernel, gate_idx=2, sc=4, sh=3),
        out_shape=(_sds((t, d), F32), _sds((t, V7X_LANES), F32), _sds((V7X_SUBLANES, V7X_LANES), F32)),
        grid=(t // tm,),
        in_specs=[pl.BlockSpec((tm, k), lambda i: (i, 0)),
                  pl.BlockSpec((k, d), const),
                  pl.BlockSpec((tm, d), lambda i: (i, 0)),
                  pl.BlockSpec((1, 6, d), lambda i: (i // tiles_per_batch, 0, 0)),
                  pl.BlockSpec((1, d), const),
                  pl.BlockSpec((1, d), const),
                  pl.BlockSpec((d, V7X_LANES), const),
                  pl.BlockSpec((1, V7X_LANES), const)],
        out_specs=(pl.BlockSpec((tm, d), lambda i: (i, 0)),
                   pl.BlockSpec((tm, V7X_LANES), lambda i: (i, 0)),
                   pl.BlockSpec((V7X_SUBLANES, V7X_LANES), const)),
        scratch_shapes=[pltpu.VMEM((V7X_SUBLANES, V7X_LANES), F32)],
        compiler_params=_cparams(("arbitrary",), 40),
        name="proj_ln_route",
    )(a, w, xres, mod, ln_g.reshape(1, d), ln_b.reshape(1, d), wr, br)


def _row_copy(src_ref, src_row, dst_ref, dst_row, sem):
    return pltpu.make_async_copy(src_ref.at[pl.ds(src_row, 1)], dst_ref.at[pl.ds(dst_row, 1)], sem)


def _dispatch_kernel(pos_ref, x_ref, mod_ref, xs_init_ref, xs_ref, hm_ref, sem, *, sc, sh):
    del xs_init_ref
    tt = x_ref.shape[0]
    m = mod_ref[0]
    hm_ref[...] = x_ref[...] * (1.0 + m[sc:sc + 1, :]) + m[sh:sh + 1, :]

    def issue(r, carry):
        _row_copy(hm_ref, r, xs_ref, pos_ref[0, 0, 2 * r], sem).start()
        _row_copy(hm_ref, r, xs_ref, pos_ref[0, 0, 2 * r + 1], sem).start()
        return carry

    def drain(r, carry):
        _row_copy(hm_ref, 0, xs_ref, 0, sem).wait()
        _row_copy(hm_ref, 0, xs_ref, 0, sem).wait()
        return carry

    lax.fori_loop(0, tt, issue, 0)
    lax.fori_loop(0, tt, drain, 0)


def _dispatch(x, mod, pos, n_rows, *, seq_len):
    t, d = x.shape
    tt = TOKEN_TILE
    tiles_per_batch = seq_len // tt
    return pl.pallas_call(
        functools.partial(_dispatch_kernel, sc=4, sh=3),
        out_shape=_sds((n_rows, d), F32),
        grid=(t // tt,),
        in_specs=[pl.BlockSpec((1, 1, 2 * tt), lambda i: (i, 0, 0), memory_space=pltpu.SMEM),
                  pl.BlockSpec((tt, d), lambda i: (i, 0)),
                  pl.BlockSpec((1, 6, d), lambda i: (i // tiles_per_batch, 0, 0)),
                  pl.BlockSpec(memory_space=pl.ANY)],
        out_specs=pl.BlockSpec(memory_space=pl.ANY),
        scratch_shapes=[pltpu.VMEM((tt, d), F32), pltpu.SemaphoreType.DMA(())],
        input_output_aliases={3: 0},
        compiler_params=_cparams(("arbitrary",), 32),
        name="moe_dispatch",
    )(pos.reshape(t // tt, 1, 2 * tt), x, mod, jnp.zeros((n_rows, d), F32))


def _ffn_kernel(te_ref, nu_ref, x_ref, wg_ref, wu_ref, wd_ref, o_ref):
    del te_ref
    used = pl.program_id(0) < nu_ref[0]

    @pl.when(used)
    def _():
        x = x_ref[...].astype(BF16)
        hg = _dot(x, wg_ref[0])
        hu = _dot(x, wu_ref[0])
        o_ref[...] = _dot((_silu(hg) * hu).astype(BF16), wd_ref[0])

    @pl.when(jnp.logical_not(used))
    def _():
        o_ref[...] = jnp.zeros_like(o_ref)


def _ffn(xs, tile_expert, n_used, w_gate, w_up, w_down):
    p, d = xs.shape
    f = w_gate.shape[2]
    tm = MOE_ROW_TILE
    grid_spec = pltpu.PrefetchScalarGridSpec(
        num_scalar_prefetch=2,
        grid=(p // tm,),
        in_specs=[pl.BlockSpec((tm, d), lambda i, te, nu: (i, 0)),
                  pl.BlockSpec((1, d, f), lambda i, te, nu: (te[i], 0, 0)),
                  pl.BlockSpec((1, d, f), lambda i, te, nu: (te[i], 0, 0)),
                  pl.BlockSpec((1, f, d), lambda i, te, nu: (te[i], 0, 0))],
        out_specs=pl.BlockSpec((tm, d), lambda i, te, nu: (i, 0)),
    )
    return pl.pallas_call(
        _ffn_kernel,
        out_shape=_sds((p, d), F32),
        grid_spec=grid_spec,
        compiler_params=_cparams(("arbitrary",), 40),
        name="moe_ffn",
    )(tile_expert, n_used, xs, w_gate, w_up, w_down)


def _combine_ln_kernel(pos_ref, xres_ref, route_ref, mod_ref, lng_ref, lnb_ref, ye_ref, o_ref, buf, sem, *, gate_idx):
    tt = xres_ref.shape[0]

    def issue(r, carry):
        _row_copy(ye_ref, pos_ref[0, 0, 2 * r], buf.at[0], r, sem).start()
        _row_copy(ye_ref, pos_ref[0, 0, 2 * r + 1], buf.at[1], r, sem).start()
        return carry

    def drain(r, carry):
        _row_copy(ye_ref, 0, buf.at[0], 0, sem).wait()
        _row_copy(ye_ref, 0, buf.at[0], 0, sem).wait()
        return carry

    lax.fori_loop(0, tt, issue, 0)
    lax.fori_loop(0, tt, drain, 0)
    m = mod_ref[0]
    rec = route_ref[...]
    y = rec[:, 2:3] * buf[0] + rec[:, 3:4] * buf[1]
    v = DEEPNORM_ALPHA * xres_ref[...] + (1.0 + m[gate_idx:gate_idx + 1, :]) * y
    o_ref[...] = _layer_norm(v, lng_ref[...], lnb_ref[...])


def _combine_ln(xres, route, pos, ye, mod, ln_g, ln_b, *, seq_len):
    t, d = xres.shape
    tt = TOKEN_TILE
    tiles_per_batch = seq_len // tt
    const = lambda i: (0, 0)
    return pl.pallas_call(
        functools.partial(_combine_ln_kernel, gate_idx=5),
        out_shape=_sds((t, d), F32),
        grid=(t // tt,),
        in_specs=[pl.BlockSpec((1, 1, 2 * tt), lambda i: (i, 0, 0), memory_space=pltpu.SMEM),
                  pl.BlockSpec((tt, d), lambda i: (i, 0)),
                  pl.BlockSpec((tt, V7X_LANES), lambda i: (i, 0)),
                  pl.BlockSpec((1, 6, d), lambda i: (i // tiles_per_batch, 0, 0)),
                  pl.BlockSpec((1, d), const),
                  pl.BlockSpec((1, d), const),
                  pl.BlockSpec(memory_space=pl.ANY)],
        out_specs=pl.BlockSpec((tt, d), lambda i: (i, 0)),
        scratch_shapes=[pltpu.VMEM((2, tt, d), F32), pltpu.SemaphoreType.DMA(())],
        compiler_params=_cparams(("arbitrary",), 32),
        name="moe_combine_ln",
    )(pos.reshape(t // tt, 1, 2 * tt), xres, route, mod, ln_g.reshape(1, d), ln_b.reshape(1, d), ye)


def _moe_tables(route, counts):
    t = route.shape[0]
    tm = MOE_ROW_TILE
    max_tiles = (2 * t) // tm + N_EXPERTS
    cnt = counts[0, :N_EXPERTS].astype(jnp.int32)
    tiles_e = (cnt + tm - 1) // tm
    tile_end = jnp.cumsum(tiles_e)
    row_off = (tile_end - tiles_e) * tm
    ids = route[:, 0:2].astype(jnp.int32)
    ranks = route[:, 4:6].astype(jnp.int32)
    pos = (row_off[ids] + ranks).reshape(-1)
    n_used = tile_end[-1:]
    tile_ids = jnp.minimum(jnp.arange(max_tiles, dtype=jnp.int32), n_used - 1)
    tile_expert = jnp.sum(tile_ids[:, None] >= tile_end[None, :], axis=1).astype(jnp.int32)
    return pos, tile_expert, n_used.astype(jnp.int32), max_tiles * tm


def _moe_layer(x1, route, counts, mod, w_gate, w_up, w_down, ln_g, ln_b, *, seq_len):
    pos, tile_expert, n_used, n_rows = _moe_tables(route, counts)
    xs = _dispatch(x1, mod, pos, n_rows, seq_len=seq_len)
    ye = _ffn(xs, tile_expert, n_used, w_gate.astype(BF16), w_up.astype(BF16), w_down.astype(BF16))
    return _combine_ln(x1, route, pos, ye, mod, ln_g, ln_b, seq_len=seq_len)


def _qkv_body(x_ref, mod_ref, w_ref, o_ref, hm_ref, *, sc, sh, kind, tiles_per_batch):
    @pl.when(pl.program_id(1) == 0)
    def _():
        _modulate_into(hm_ref, x_ref, mod_ref, sc, sh)

    tm = x_ref.shape[0]
    acc = _dot(hm_ref[...], w_ref[...])
    lane = lax.broadcasted_iota(jnp.int32, (tm, V7X_LANES), 1)
    low = lane < ATTN_HEAD_DIM
    if kind == "q":
        acc = acc * (ATTN_HEAD_DIM ** -0.5)
        extra = jnp.zeros((tm, V7X_LANES), F32)
    elif kind == "k":
        row = lax.broadcasted_iota(jnp.int32, (tm, V7X_LANES), 0)
        pos = lax.rem(pl.program_id(0), tiles_per_batch) * tm + row
        extra = (lane == ATTN_HEAD_DIM + lax.div(pos, MOBA_BLOCK)).astype(F32)
    else:
        extra = (lane == ATTN_HEAD_DIM).astype(F32)
    for p in range(acc.shape[1] // V7X_LANES):
        t2 = acc[:, p * V7X_LANES:(p + 1) * V7X_LANES]
        even = jnp.where(low, t2, extra)
        odd = jnp.where(low, pltpu.roll(t2, ATTN_HEAD_DIM, axis=1), extra)
        o_ref[0, 2 * p] = even.astype(BF16)
        o_ref[0, 2 * p + 1] = odd.astype(BF16)


def _qkv(x, mod, w, *, kind, batch, seq_len, tm=512, tn=512):
    t, k = x.shape
    n = w.shape[1]
    heads_per_tile = tn // ATTN_HEAD_DIM
    tiles_per_batch = seq_len // tm
    return pl.pallas_call(
        functools.partial(_qkv_body, sc=1, sh=0, kind=kind, tiles_per_batch=tiles_per_batch),
        out_shape=_sds((batch, ATTN_HEADS, seq_len, V7X_LANES), BF16),
        grid=(t // tm, n // tn),
        in_specs=[pl.BlockSpec((tm, k), lambda i, j: (i, 0)),
                  pl.BlockSpec((1, 6, k), lambda i, j: (i // tiles_per_batch, 0, 0)),
                  pl.BlockSpec((k, tn), lambda i, j: (0, j))],
        out_specs=pl.BlockSpec((1, heads_per_tile, tm, V7X_LANES),
                               lambda i, j: (i // tiles_per_batch, j, i % tiles_per_batch, 0)),
        scratch_shapes=[pltpu.VMEM((tm, k), BF16)],
        compiler_params=_cparams(("arbitrary", "arbitrary"), 40),
        name="qkv_" + kind,
    )(x, mod, w)


def _attn_kernel(q_ref, k_ref, v_ref, pm_ref, o_ref, km_ref):
    tq = q_ref.shape[2]
    assert tq == MOBA_BLOCK
    qi = pl.program_id(2)

    @pl.when(qi == 0)
    def _():
        for hh in range(2):
            km_ref[hh] = _dot(pm_ref[...], k_ref[0, hh]).astype(BF16)

    lane = lax.broadcasted_iota(jnp.int32, (tq, V7X_LANES), 1)
    lane_f = lane.astype(F32)
    blk = lane - ATTN_HEAD_DIM
    mask_lanes = (blk >= 0) & (blk < pm_ref.shape[1] // MOBA_BLOCK)
    past = (blk >= 0) & (blk < qi)
    row = lax.broadcasted_iota(jnp.int32, (tq, MOBA_BLOCK), 0)
    col = lax.broadcasted_iota(jnp.int32, (tq, MOBA_BLOCK), 1)
    outs = []
    for hh in range(2):
        q = q_ref[0, hh]
        gate = jnp.where(past, _dot_nt(q, km_ref[hh]), -jnp.inf)
        chosen = jnp.zeros((tq, V7X_LANES), jnp.bool_)
        for _ in range(MOBA_TOPK):
            mx = jnp.max(gate, axis=-1, keepdims=True)
            idx = jnp.min(jnp.where(gate == mx, lane_f, float(V7X_LANES)), axis=-1, keepdims=True)
            pick = lane_f == idx
            chosen = chosen | pick
            gate = jnp.where(pick, -jnp.inf, gate)
        blocked = mask_lanes & jnp.logical_not(chosen & past)
        qa = (q.astype(F32) + jnp.where(blocked, NEG_INF, 0.0)).astype(BF16)

        def kv_step(j, carry, qa=qa, hh=hh):
            m_i, acc = carry
            start = pl.multiple_of(j * MOBA_BLOCK, MOBA_BLOCK)
            s = _dot_nt(qa, k_ref[0, hh, pl.ds(start, MOBA_BLOCK), :])
            m_new = jnp.maximum(m_i, jnp.max(s, axis=-1, keepdims=True))
            p = jnp.exp(s - m_new)
            acc = acc * jnp.exp(m_i - m_new) + _dot(p.astype(BF16), v_ref[0, hh, pl.ds(start, MOBA_BLOCK), :])
            return m_new, acc

        m0 = jnp.full((tq, 1), NEG_INF, F32)
        acc0 = jnp.zeros((tq, V7X_LANES), F32)
        m_i, acc = lax.fori_loop(0, qi, kv_step, (m0, acc0))
        own = pl.multiple_of(qi * MOBA_BLOCK, MOBA_BLOCK)
        s = jnp.where(col <= row, _dot_nt(q, k_ref[0, hh, pl.ds(own, MOBA_BLOCK), :]), NEG_INF)
        m_new = jnp.maximum(m_i, jnp.max(s, axis=-1, keepdims=True))
        p = jnp.exp(s - m_new)
        acc = acc * jnp.exp(m_i - m_new) + _dot(p.astype(BF16), v_ref[0, hh, pl.ds(own, MOBA_BLOCK), :])
        outs.append(acc / acc[:, ATTN_HEAD_DIM:ATTN_HEAD_DIM + 1])
    o_ref[0] = jnp.where(lane < ATTN_HEAD_DIM, outs[0], pltpu.roll(outs[1], ATTN_HEAD_DIM, axis=1)).astype(BF16)


def _attention(q, k, v, *, batch, seq_len):
    nb = seq_len // MOBA_BLOCK
    tq = MOBA_BLOCK
    rows = jnp.arange(V7X_LANES, dtype=jnp.int32)[:, None] - ATTN_HEAD_DIM
    cols = jnp.arange(seq_len, dtype=jnp.int32)[None, :] // MOBA_BLOCK
    pool = jnp.where(rows == cols, 1.0 / MOBA_BLOCK, 0.0).astype(BF16)
    assert nb <= V7X_LANES - ATTN_HEAD_DIM
    kv_spec = pl.BlockSpec((1, 2, seq_len, V7X_LANES), lambda b, h, i: (b, h, 0, 0))
    return pl.pallas_call(
        _attn_kernel,
        out_shape=_sds((batch, seq_len, ATTN_HEADS * ATTN_HEAD_DIM), BF16),
        grid=(batch, ATTN_HEADS // 2, seq_len // tq),
        in_specs=[pl.BlockSpec((1, 2, tq, V7X_LANES), lambda b, h, i: (b, h, i, 0)),
                  kv_spec, kv_spec,
                  pl.BlockSpec((V7X_LANES, seq_len), lambda b, h, i: (0, 0))],
        out_specs=pl.BlockSpec((1, tq, V7X_LANES), lambda b, h, i: (b, i, h)),
        scratch_shapes=[pltpu.VMEM((2, V7X_LANES, V7X_LANES), BF16)],
        compiler_params=_cparams(("arbitrary", "arbitrary", "arbitrary"), 40),
        name="moba_attention",
    )(q, k, v, pool)


def kernel(x, c, w_ada, b_ada, ln_g, ln_b, ssm_w_in, ssm_conv_w, ssm_conv_b, ssm_dt_bias, ssm_a_log, ssm_d,
           ssm_norm_w, ssm_w_out, attn_w_qkv, attn_w_o, w_router, b_router, moe_w_gate, moe_w_up, moe_w_down):
    batch, seq_len, d = x.shape
    assert seq_len % 512 == 0 and seq_len % SSM_CHUNK == 0 and seq_len % MOBA_BLOCK == 0
    t = batch * seq_len
    mod = _ada_mod(c, w_ada, b_ada)
    xf = x.reshape(t, d)

    d_inner = SSM_GROUPS * GROUP_W
    conv_dim = d_inner + 2 * SSM_GROUPS * SSM_STATE
    w_in = ssm_w_in[0]
    heads = ssm_dt_bias.shape[1]
    w_main = w_in[:, :d_inner + conv_dim].astype(BF16)
    w_dt = jnp.pad(w_in[:, d_inner + conv_dim:], ((0, 0), (0, V7X_LANES - heads))).astype(BF16)
    proj = _mm_mod(xf, mod[0], w_main, sc=1, sh=0, seq_len=seq_len, out_dtype=BF16)
    dt_raw = _mm_mod(xf, mod[0], w_dt, sc=1, sh=0, seq_len=seq_len, out_dtype=F32)
    y = _ssd(proj, dt_raw, ssm_conv_w[0], ssm_conv_b[0], ssm_dt_bias[0], ssm_a_log[0], ssm_d[0], ssm_norm_w[0],
             batch=batch, seq_len=seq_len)
    x1, route, counts = _proj_ln_route(y, ssm_w_out[0].astype(BF16), xf, mod[0], ln_g[0, 0], ln_b[0, 0],
                                       w_router, b_router, seq_len=seq_len)
    xf = _moe_layer(x1, route, counts, mod[0], moe_w_gate[0], moe_w_up[0], moe_w_down[0], ln_g[0, 1], ln_b[0, 1],
                    seq_len=seq_len)

    w_qkv = attn_w_qkv[0].astype(BF16)
    q = _qkv(xf, mod[1], w_qkv[:, :d], kind="q", batch=batch, seq_len=seq_len)
    k = _qkv(xf, mod[1], w_qkv[:, d:2 * d], kind="k", batch=batch, seq_len=seq_len)
    v = _qkv(xf, mod[1], w_qkv[:, 2 * d:], kind="v", batch=batch, seq_len=seq_len)
    o = _attention(q, k, v, batch=batch, seq_len=seq_len).reshape(t, d)
    x1, route, counts = _proj_ln_route(o, attn_w_o[0].astype(BF16), xf, mod[1], ln_g[1, 0], ln_b[1, 0],
                                       w_router, b_router, seq_len=seq_len)
    xf = _moe_layer(x1, route, counts, mod[1], moe_w_gate[1], moe_w_up[1], moe_w_down[1], ln_g[1, 1], ln_b[1, 1],
                    seq_len=seq_len)
    return xf.reshape(batch, seq_len, d)
```

```python
import functools

import jax
import jax.numpy as jnp
from jax import lax
from jax.experimental import pallas as pl
from jax.experimental.pallas import tpu as pltpu

F32 = jnp.float32
BF16 = jnp.bfloat16

V7X_LANES = 128
V7X_SUBLANES = 8
V7X_VMEM_BYTES = 64 * 1024 * 1024

DEPTH = 2
SSM_HEAD_DIM = 64
SSM_STATE = 128
SSM_GROUPS = 8
SSM_HEADS_PER_GROUP = 4
SSM_CONV = 4
SSM_CHUNK = 256
ATTN_HEADS = 16
ATTN_HEAD_DIM = 64
MOBA_BLOCK = 256
MOBA_TOPK = 3
N_EXPERTS = 32
N_EXPERT_GROUPS = 4
EXPERTS_PER_GROUP = N_EXPERTS // N_EXPERT_GROUPS
DEEPNORM_ALPHA = (2.0 * DEPTH) ** 0.25
LN_EPS = 1e-5
RMS_EPS = 1e-5
NEG_INF = -1e30

GROUP_W = SSM_HEADS_PER_GROUP * SSM_HEAD_DIM
CONV_HALO = V7X_SUBLANES
MOE_ROW_TILE = 256
TOKEN_TILE = 256
DMA_UNROLL = 8


def _cparams(semantics, vmem_mib):
    assert vmem_mib * 1024 * 1024 <= V7X_VMEM_BYTES
    return pltpu.CompilerParams(dimension_semantics=semantics, vmem_limit_bytes=vmem_mib * 1024 * 1024)


def _sds(shape, dtype):
    return jax.ShapeDtypeStruct(shape, dtype)


def _dot(a, b):
    return jnp.dot(a, b, preferred_element_type=F32)


def _dot_nt(a, b):
    return lax.dot_general(a, b, (((1,), (1,)), ((), ())), preferred_element_type=F32)


def _dot_tn(a, b):
    return lax.dot_general(a, b, (((0,), (0,)), ((), ())), preferred_element_type=F32)


def _split3(a):
    hi = a.astype(BF16)
    r1 = a - hi.astype(F32)
    mid = r1.astype(BF16)
    lo = (r1 - mid.astype(F32)).astype(BF16)
    return hi, mid, lo


def _silu(x):
    return x * jax.nn.sigmoid(x)


def _layer_norm(v, gamma, beta):
    mu = jnp.mean(v, axis=-1, keepdims=True)
    d = v - mu
    var = jnp.mean(d * d, axis=-1, keepdims=True)
    return d * lax.rsqrt(var + LN_EPS) * gamma + beta


def _ada_kernel(c_ref, w_ref, b_ref, o_ref):
    cs = _silu(c_ref[...])
    o_ref[0] = jnp.dot(cs, w_ref[0], preferred_element_type=F32, precision=lax.Precision.HIGHEST) + b_ref[0]


def _ada_mod(c, w_ada, b_ada):
    depth, d, n = w_ada.shape
    b = c.shape[0]
    tn = 1024
    out = pl.pallas_call(
        _ada_kernel,
        out_shape=_sds((depth, b, n), F32),
        grid=(depth, n // tn),
        in_specs=[pl.BlockSpec((b, d), lambda l, j: (0, 0)),
                  pl.BlockSpec((1, d, tn), lambda l, j: (l, 0, j)),
                  pl.BlockSpec((1, 1, tn), lambda l, j: (l, 0, j))],
        out_specs=pl.BlockSpec((1, b, tn), lambda l, j: (l, 0, j)),
        compiler_params=_cparams(("arbitrary", "arbitrary"), 32),
        name="ada_mod",
    )(c, w_ada, b_ada.reshape(depth, 1, n))
    return out.reshape(depth, b, 6, d)


def _modulate_into(hm_ref, x_ref, mod_ref, sc, sh):
    m = mod_ref[0]
    hm_ref[...] = (x_ref[...] * (1.0 + m[sc:sc + 1, :]) + m[sh:sh + 1, :]).astype(BF16)


def _mm_mod_kernel(x_ref, mod_ref, w_ref, o_ref, hm_ref, *, sc, sh):
    @pl.when(pl.program_id(1) == 0)
    def _():
        _modulate_into(hm_ref, x_ref, mod_ref, sc, sh)

    o_ref[...] = _dot(hm_ref[...], w_ref[...]).astype(o_ref.dtype)


def _mm_mod(x, mod, w, *, sc, sh, seq_len, out_dtype, tm=512, tn=512):
    t, k = x.shape
    n = w.shape[1]
    tn = min(tn, n)
    tiles_per_batch = seq_len // tm
    return pl.pallas_call(
        functools.partial(_mm_mod_kernel, sc=sc, sh=sh),
        out_shape=_sds((t, n), out_dtype),
        grid=(t // tm, n // tn),
        in_specs=[pl.BlockSpec((tm, k), lambda i, j: (i, 0)),
                  pl.BlockSpec((1, 6, k), lambda i, j: (i // tiles_per_batch, 0, 0)),
                  pl.BlockSpec((k, tn), lambda i, j: (0, j))],
        out_specs=pl.BlockSpec((tm, tn), lambda i, j: (i, j)),
        scratch_shapes=[pltpu.VMEM((tm, k), BF16)],
        compiler_params=_cparams(("arbitrary", "arbitrary"), 40),
        name="mm_mod",
    )(x, mod, w)


def _ssd_kernel(z_ref, xp_ref, bcp_ref, dtr_ref, cw_ref, cb_ref, dtb_ref, alog_ref, dsk_ref, nw_ref,
                o_ref, ubuf, act, state, acum_t):
    lc = SSM_CHUNK
    d_inner = xp_ref.shape[1]
    n_slabs = ubuf.shape[0]
    x_slabs = d_inner // V7X_LANES
    chunk = pl.program_id(1)

    @pl.when(chunk == 0)
    def _():
        ubuf[:, 0:CONV_HALO, :] = jnp.zeros((n_slabs, CONV_HALO, V7X_LANES), F32)
        state[...] = jnp.zeros_like(state)

    def conv_slabs(src_ref, first_slab, count):
        def slab_body(j, carry):
            src_off = pl.multiple_of(j * V7X_LANES, V7X_LANES)
            slab = first_slab + j
            off = pl.multiple_of(slab * V7X_LANES, V7X_LANES)
            u = src_ref[:, pl.ds(src_off, V7X_LANES)].astype(F32)
            ubuf[slab, CONV_HALO:CONV_HALO + lc, :] = u
            acc = cb_ref[:, pl.ds(off, V7X_LANES)] + cw_ref[SSM_CONV - 1:SSM_CONV, pl.ds(off, V7X_LANES)] * u
            for k in range(SSM_CONV - 1):
                tap = CONV_HALO - (SSM_CONV - 1) + k
                acc = acc + cw_ref[k:k + 1, pl.ds(off, V7X_LANES)] * ubuf[slab, tap:tap + lc, :]
            act[:, pl.ds(off, V7X_LANES)] = _silu(acc)
            ubuf[slab, 0:CONV_HALO, :] = u[lc - CONV_HALO:lc, :]
            return carry

        lax.fori_loop(0, count, slab_body, 0)

    conv_slabs(xp_ref, 0, x_slabs)
    conv_slabs(bcp_ref, x_slabs, n_slabs - x_slabs)

    dt = jax.nn.softplus(dtr_ref[...] + dtb_ref[...])
    da = dt * (-jnp.exp(alog_ref[...]))
    row = lax.broadcasted_iota(jnp.int32, (lc, lc), 0)
    col = lax.broadcasted_iota(jnp.int32, (lc, lc), 1)
    causal = col <= row
    tril = causal.astype(BF16)
    acum = sum(_dot(tril, part) for part in _split3(da))
    acum_t[...] = acum.T
    acum_parts = _split3(acum)
    dt_bf16 = dt.astype(BF16)
    head_of_col = lax.broadcasted_iota(jnp.int32, (lc, GROUP_W), 1) // SSM_HEAD_DIM
    sel_row = lax.broadcasted_iota(jnp.int32, (V7X_LANES, GROUP_W), 0)
    sel_col = lax.broadcasted_iota(jnp.int32, (V7X_LANES, GROUP_W), 1) // SSM_HEAD_DIM

    def group_body(g, carry):
        xo = pl.multiple_of(g * GROUP_W, GROUP_W)
        bo = pl.multiple_of(d_inner + g * SSM_STATE, SSM_STATE)
        co = pl.multiple_of(d_inner + SSM_GROUPS * SSM_STATE + g * SSM_STATE, SSM_STATE)
        x_g = act[:, pl.ds(xo, GROUP_W)]
        b_g = act[:, pl.ds(bo, SSM_STATE)].astype(BF16)
        c_g = act[:, pl.ds(co, SSM_STATE)].astype(BF16)
        sel = (sel_row == SSM_HEADS_PER_GROUP * g + sel_col).astype(BF16)
        ab = sum(_dot(part, sel) for part in acum_parts)
        dtb = _dot(dt_bf16, sel)
        alast = ab[lc - 1:lc, :]
        cb = _dot_nt(c_g, b_g)
        xdt = x_g * dtb
        y = jnp.zeros((lc, GROUP_W), F32)
        for r in range(SSM_HEADS_PER_GROUP):
            acol = ab[:, r * SSM_HEAD_DIM:r * SSM_HEAD_DIM + 1]
            arow = acum_t[pl.ds(SSM_HEADS_PER_GROUP * g + r, 1), :]
            decay = jnp.exp(jnp.where(causal, acol - arow, -jnp.inf))
            w = (cb * decay).astype(BF16)
            x_r = jnp.where(head_of_col == r, xdt, 0.0).astype(BF16)
            y = y + _dot(w, x_r)
        st = state[:, pl.ds(xo, GROUP_W)]
        y = y + _dot(c_g, st.astype(BF16)) * jnp.exp(ab)
        to_end = jnp.exp(alast - ab) * dtb
        xw = (x_g * to_end).astype(BF16)
        state[:, pl.ds(xo, GROUP_W)] = st * jnp.exp(alast) + _dot_tn(b_g, xw)
        y = y + dsk_ref[:, pl.ds(xo, GROUP_W)] * x_g
        gated = y * _silu(z_ref[:, pl.ds(xo, GROUP_W)].astype(F32))
        ms = jnp.mean(gated * gated, axis=-1, keepdims=True)
        o_ref[:, pl.ds(xo, GROUP_W)] = (gated * lax.rsqrt(ms + RMS_EPS) * nw_ref[:, pl.ds(xo, GROUP_W)]).astype(BF16)
        return carry

    lax.fori_loop(0, SSM_GROUPS, group_body, 0)


def _ssd(proj, dt_raw, conv_w, conv_b, dt_bias, a_log, d_skip, norm_w, *, batch, seq_len):
    t = proj.shape[0]
    d_inner = SSM_GROUPS * GROUP_W
    conv_dim = d_inner + 2 * SSM_GROUPS * SSM_STATE
    assert proj.shape[1] == d_inner + conv_dim and conv_dim == 2 * d_inner
    nc = seq_len // SSM_CHUNK
    lc = SSM_CHUNK
    pad = V7X_LANES - dt_bias.shape[0]
    heads = dt_bias.shape[0]
    row_map = lambda b, c: b * nc + c
    const = lambda b, c: (0, 0)
    return pl.pallas_call(
        _ssd_kernel,
        out_shape=_sds((t, d_inner), BF16),
        grid=(batch, nc),
        in_specs=[pl.BlockSpec((lc, d_inner), lambda b, c: (row_map(b, c), 0)),
                  pl.BlockSpec((lc, d_inner), lambda b, c: (row_map(b, c), 1)),
                  pl.BlockSpec((lc, d_inner), lambda b, c: (row_map(b, c), 2)),
                  pl.BlockSpec((lc, V7X_LANES), lambda b, c: (row_map(b, c), 0)),
                  pl.BlockSpec((SSM_CONV, conv_dim), const),
                  pl.BlockSpec((1, conv_dim), const),
                  pl.BlockSpec((1, V7X_LANES), const),
                  pl.BlockSpec((1, V7X_LANES), const),
                  pl.BlockSpec((1, d_inner), const),
                  pl.BlockSpec((1, d_inner), const)],
        out_specs=pl.BlockSpec((lc, d_inner), lambda b, c: (row_map(b, c), 0)),
        scratch_shapes=[pltpu.VMEM((conv_dim // V7X_LANES, CONV_HALO + lc, V7X_LANES), F32),
                        pltpu.VMEM((lc, conv_dim), F32),
                        pltpu.VMEM((SSM_STATE, d_inner), F32),
                        pltpu.VMEM((V7X_LANES, lc), F32)],
        compiler_params=_cparams(("arbitrary", "arbitrary"), 48),
        name="ssd_chunk_scan",
    )(proj, proj, proj, dt_raw, conv_w, conv_b.reshape(1, conv_dim),
      jnp.pad(dt_bias, (0, pad)).reshape(1, V7X_LANES), jnp.pad(a_log, (0, pad)).reshape(1, V7X_LANES),
      jnp.repeat(d_skip, SSM_HEAD_DIM).reshape(1, heads * SSM_HEAD_DIM), norm_w.reshape(1, d_inner))


def _route(logits, carry_ref):
    tm = logits.shape[0]
    lane = lax.broadcasted_iota(jnp.int32, (tm, V7X_LANES), 1)
    lane_f = lane.astype(F32)
    lg = jnp.where(lane < N_EXPERTS, logits, -jnp.inf)
    e = jnp.exp(lg - jnp.max(lg, axis=-1, keepdims=True))
    far = float(V7X_LANES)
    best = None
    for g in range(N_EXPERT_GROUPS):
        in_g = (lane >= g * EXPERTS_PER_GROUP) & (lane < (g + 1) * EXPERTS_PER_GROUP)
        eg = jnp.where(in_g, e, -1.0)
        m1 = jnp.max(eg, axis=-1, keepdims=True)
        i1 = jnp.min(jnp.where(eg == m1, lane_f, far), axis=-1, keepdims=True)
        eg2 = jnp.where(lane_f == i1, -1.0, eg)
        m2 = jnp.max(eg2, axis=-1, keepdims=True)
        i2 = jnp.min(jnp.where(eg2 == m2, lane_f, far), axis=-1, keepdims=True)
        cand = (m1 + m2, m1, m2, i1, i2)
        if best is None:
            best = cand
        else:
            take = cand[0] > best[0]
            best = tuple(jnp.where(take, c, b) for c, b in zip(cand, best))
    _, m1, m2, i1, i2 = best
    denom = m1 + m2
    hit1 = lane_f == i1
    hit2 = lane_f == i2
    onehot = (hit1 | hit2).astype(BF16)
    row = lax.broadcasted_iota(jnp.int32, (tm, tm), 0)
    col = lax.broadcasted_iota(jnp.int32, (tm, tm), 1)
    before = (col < row).astype(BF16)
    rank = _dot(before, onehot) + carry_ref[0:1, :]
    r1 = jnp.sum(jnp.where(hit1, rank, 0.0), axis=-1, keepdims=True)
    r2 = jnp.sum(jnp.where(hit2, rank, 0.0), axis=-1, keepdims=True)
    carry_ref[0:1, :] = carry_ref[0:1, :] + jnp.sum(onehot.astype(F32), axis=0, keepdims=True)
    rec = jnp.zeros((tm, V7X_LANES), F32)
    for k, val in enumerate((i1, i2, m1 / denom, m2 / denom, r1, r2)):
        rec = jnp.where(lane == k, val, rec)
    return rec


def _proj_ln_route_kernel(a_ref, w_ref, xres_ref, mod_ref, lng_ref, lnb_ref, wr_ref, br_ref,
                          x_ref, route_ref, cnt_ref, carry_ref, *, gate_idx, sc, sh):
    @pl.when(pl.program_id(0) == 0)
    def _():
        carry_ref[...] = jnp.zeros_like(carry_ref)

    m = mod_ref[0]
    y = _dot(a_ref[...], w_ref[...])
    v = DEEPNORM_ALPHA * xres_ref[...] + (1.0 + m[gate_idx:gate_idx + 1, :]) * y
    x1 = _layer_norm(v, lng_ref[...], lnb_ref[...])
    x_ref[...] = x1
    hm = (x1 * (1.0 + m[sc:sc + 1, :]) + m[sh:sh + 1, :]).astype(BF16)
    logits = _dot(hm, wr_ref[...]) + br_ref[...]
    route_ref[...] = _route(logits, carry_ref)
    cnt_ref[...] = carry_ref[...]


def _proj_ln_route(a, w, xres, mod, ln_g, ln_b, w_router, b_router, *, seq_len):
    t, k = a.shape
    d = w.shape[1]
    tm = TOKEN_TILE
    tiles_per_batch = seq_len // tm
    const = lambda i: (0, 0)
    wr = jnp.pad(w_router, ((0, 0), (0, V7X_LANES - N_EXPERTS))).astype(BF16)
    br = jnp.pad(b_router, (0, V7X_LANES - N_EXPERTS)).reshape(1, V7X_LANES)
    return pl.pallas_call(
        functools.partial(_proj_ln_route_kernel, gate_idx=2, sc=4, sh=3),
        out_shape=(_sds((t, d), F32), _sds((t, V7X_LANES), F32), _sds((V7X_SUBLANES, V7X_LANES), F32)),
        grid=(t // tm,),
        in_specs=[pl.BlockSpec((tm, k), lambda i: (i, 0)),
                  pl.BlockSpec((k, d), const),
                  pl.BlockSpec((tm, d), lambda i: (i, 0)),
                  pl.BlockSpec((1, 6, d), lambda i: (i // tiles_per_batch, 0, 0)),
                  pl.BlockSpec((1, d), const),
                  pl.BlockSpec((1, d), const),
                  pl.BlockSpec((d, V7X_LANES), const),
                  pl.BlockSpec((1, V7X_LANES), const)],
        out_specs=(pl.BlockSpec((tm, d), lambda i: (i, 0)),
                   pl.BlockSpec((tm, V7X_LANES), lambda i: (i, 0)),
                   pl.BlockSpec((V7X_SUBLANES, V7X_LANES), const)),
        scratch_shapes=[pltpu.VMEM((V7X_SUBLANES, V7X_LANES), F32)],
        compiler_params=_cparams(("arbitrary",), 40),
        name="proj_ln_route",
    )(a, w, xres, mod, ln_g.reshape(1, d), ln_b.reshape(1, d), wr, br)


def _row_copy(src_ref, src_row, dst_ref, dst_row, sem):
    return pltpu.make_async_copy(src_ref.at[pl.ds(src_row, 1)], dst_ref.at[pl.ds(dst_row, 1)], sem)


def _dispatch_kernel(pos_ref, x_ref, mod_ref, xs_init_ref, xs_ref, hm_ref, sem, *, sc, sh):
    del xs_init_ref
    tt = x_ref.shape[0]
    m = mod_ref[0]
    hm_ref[...] = x_ref[...] * (1.0 + m[sc:sc + 1, :]) + m[sh:sh + 1, :]

    def issue(blk, carry):
        for u in range(DMA_UNROLL):
            r = blk * DMA_UNROLL + u
            _row_copy(hm_ref, r, xs_ref, pos_ref[0, 0, 2 * r], sem).start(priority=0)
            _row_copy(hm_ref, r, xs_ref, pos_ref[0, 0, 2 * r + 1], sem).start(priority=1)
        return carry

    def drain(blk, carry):
        for _ in range(2 * DMA_UNROLL):
            _row_copy(hm_ref, 0, xs_ref, 0, sem).wait()
        return carry

    lax.fori_loop(0, tt // DMA_UNROLL, issue, 0)
    lax.fori_loop(0, tt // DMA_UNROLL, drain, 0)


def _dispatch(x, mod, pos, n_rows, *, seq_len):
    t, d = x.shape
    tt = TOKEN_TILE
    tiles_per_batch = seq_len // tt
    return pl.pallas_call(
        functools.partial(_dispatch_kernel, sc=4, sh=3),
        out_shape=_sds((n_rows, d), F32),
        grid=(t // tt,),
        in_specs=[pl.BlockSpec((1, 1, 2 * tt), lambda i: (i, 0, 0), memory_space=pltpu.SMEM),
                  pl.BlockSpec((tt, d), lambda i: (i, 0)),
                  pl.BlockSpec((1, 6, d), lambda i: (i // tiles_per_batch, 0, 0)),
                  pl.BlockSpec(memory_space=pl.ANY)],
        out_specs=pl.BlockSpec(memory_space=pl.ANY),
        scratch_shapes=[pltpu.VMEM((tt, d), F32), pltpu.SemaphoreType.DMA(())],
        input_output_aliases={3: 0},
        compiler_params=_cparams(("arbitrary",), 32),
        name="moe_dispatch",
    )(pos.reshape(t // tt, 1, 2 * tt), x, mod, jnp.zeros((n_rows, d), F32))


def _ffn_kernel(te_ref, nu_ref, x_ref, wg_ref, wu_ref, wd_ref, o_ref):
    del te_ref
    used = pl.program_id(0) < nu_ref[0]

    @pl.when(used)
    def _():
        x = x_ref[...].astype(BF16)
        hg = _dot(x, wg_ref[0])
        hu = _dot(x, wu_ref[0])
        o_ref[...] = _dot((_silu(hg) * hu).astype(BF16), wd_ref[0])

    @pl.when(jnp.logical_not(used))
    def _():
        o_ref[...] = jnp.zeros_like(o_ref)


def _ffn(xs, tile_expert, n_used, w_gate, w_up, w_down):
    p, d = xs.shape
    f = w_gate.shape[2]
    tm = MOE_ROW_TILE
    grid_spec = pltpu.PrefetchScalarGridSpec(
        num_scalar_prefetch=2,
        grid=(p // tm,),
        in_specs=[pl.BlockSpec((tm, d), lambda i, te, nu: (i, 0)),
                  pl.BlockSpec((1, d, f), lambda i, te, nu: (te[i], 0, 0)),
                  pl.BlockSpec((1, d, f), lambda i, te, nu: (te[i], 0, 0)),
                  pl.BlockSpec((1, f, d), lambda i, te, nu: (te[i], 0, 0))],
        out_specs=pl.BlockSpec((tm, d), lambda i, te, nu: (i, 0)),
    )
    return pl.pallas_call(
        _ffn_kernel,
        out_shape=_sds((p, d), F32),
        grid_spec=grid_spec,
        compiler_params=_cparams(("arbitrary",), 40),
        name="moe_ffn",
    )(tile_expert, n_used, xs, w_gate, w_up, w_down)


def _combine_ln_kernel(pos_ref, xres_ref, route_ref, mod_ref, lng_ref, lnb_ref, ye_ref, o_ref, buf, sem, *, gate_idx):
    tt = xres_ref.shape[0]

    def issue(blk, carry):
        for u in range(DMA_UNROLL):
            r = blk * DMA_UNROLL + u
            _row_copy(ye_ref, pos_ref[0, 0, 2 * r], buf.at[0], r, sem).start(priority=0)
            _row_copy(ye_ref, pos_ref[0, 0, 2 * r + 1], buf.at[1], r, sem).start(priority=1)
        return carry

    def drain(blk, carry):
        for _ in range(2 * DMA_UNROLL):
            _row_copy(ye_ref, 0, buf.at[0], 0, sem).wait()
        return carry

    lax.fori_loop(0, tt // DMA_UNROLL, issue, 0)
    lax.fori_loop(0, tt // DMA_UNROLL, drain, 0)
    m = mod_ref[0]
    rec = route_ref[...]
    y = rec[:, 2:3] * buf[0] + rec[:, 3:4] * buf[1]
    v = DEEPNORM_ALPHA * xres_ref[...] + (1.0 + m[gate_idx:gate_idx + 1, :]) * y
    o_ref[...] = _layer_norm(v, lng_ref[...], lnb_ref[...])


def _combine_ln(xres, route, pos, ye, mod, ln_g, ln_b, *, seq_len):
    t, d = xres.shape
    tt = TOKEN_TILE
    tiles_per_batch = seq_len // tt
    const = lambda i: (0, 0)
    return pl.pallas_call(
        functools.partial(_combine_ln_kernel, gate_idx=5),
        out_shape=_sds((t, d), F32),
        grid=(t // tt,),
        in_specs=[pl.BlockSpec((1, 1, 2 * tt), lambda i: (i, 0, 0), memory_space=pltpu.SMEM),
                  pl.BlockSpec((tt, d), lambda i: (i, 0)),
                  pl.BlockSpec((tt, V7X_LANES), lambda i: (i, 0)),
                  pl.BlockSpec((1, 6, d), lambda i: (i // tiles_per_batch, 0, 0)),
                  pl.BlockSpec((1, d), const),
                  pl.BlockSpec((1, d), const),
                  pl.BlockSpec(memory_space=pl.ANY)],
        out_specs=pl.BlockSpec((tt, d), lambda i: (i, 0)),
        scratch_shapes=[pltpu.VMEM((2, tt, d), F32), pltpu.SemaphoreType.DMA(())],
        compiler_params=_cparams(("arbitrary",), 32),
        name="moe_combine_ln",
    )(pos.reshape(t // tt, 1, 2 * tt), xres, route, mod, ln_g.reshape(1, d), ln_b.reshape(1, d), ye)


def _moe_tables(route, counts):
    t = route.shape[0]
    tm = MOE_ROW_TILE
    max_tiles = (2 * t) // tm + N_EXPERTS
    cnt = counts[0, :N_EXPERTS].astype(jnp.int32)
    tiles_e = (cnt + tm - 1) // tm
    tile_end = jnp.cumsum(tiles_e)
    row_off = (tile_end - tiles_e) * tm
    ids = route[:, 0:2].astype(jnp.int32)
    ranks = route[:, 4:6].astype(jnp.int32)
    pos = (row_off[ids] + ranks).reshape(-1)
    n_used = tile_end[-1:]
    tile_ids = jnp.minimum(jnp.arange(max_tiles, dtype=jnp.int32), n_used - 1)
    tile_expert = jnp.sum(tile_ids[:, None] >= tile_end[None, :], axis=1).astype(jnp.int32)
    return pos, tile_expert, n_used.astype(jnp.int32), max_tiles * tm


def _moe_layer(x1, route, counts, mod, w_gate, w_up, w_down, ln_g, ln_b, *, seq_len):
    pos, tile_expert, n_used, n_rows = _moe_tables(route, counts)
    xs = _dispatch(x1, mod, pos, n_rows, seq_len=seq_len)
    ye = _ffn(xs, tile_expert, n_used, w_gate.astype(BF16), w_up.astype(BF16), w_down.astype(BF16))
    return _combine_ln(x1, route, pos, ye, mod, ln_g, ln_b, seq_len=seq_len)


def _qkv_body(x_ref, mod_ref, w_ref, o_ref, hm_ref, *, sc, sh, kind, tiles_per_batch):
    @pl.when(pl.program_id(1) == 0)
    def _():
        _modulate_into(hm_ref, x_ref, mod_ref, sc, sh)

    tm = x_ref.shape[0]
    acc = _dot(hm_ref[...], w_ref[...])
    lane = lax.broadcasted_iota(jnp.int32, (tm, V7X_LANES), 1)
    low = lane < ATTN_HEAD_DIM
    if kind == "q":
        acc = acc * (ATTN_HEAD_DIM ** -0.5)
        extra = jnp.zeros((tm, V7X_LANES), F32)
    elif kind == "k":
        row = lax.broadcasted_iota(jnp.int32, (tm, V7X_LANES), 0)
        pos = lax.rem(pl.program_id(0), tiles_per_batch) * tm + row
        extra = (lane == ATTN_HEAD_DIM + lax.div(pos, MOBA_BLOCK)).astype(F32)
    else:
        extra = (lane == ATTN_HEAD_DIM).astype(F32)
    for p in range(acc.shape[1] // V7X_LANES):
        t2 = acc[:, p * V7X_LANES:(p + 1) * V7X_LANES]
        even = jnp.where(low, t2, extra)
        odd = jnp.where(low, pltpu.roll(t2, ATTN_HEAD_DIM, axis=1), extra)
        o_ref[0, 2 * p] = even.astype(BF16)
        o_ref[0, 2 * p + 1] = odd.astype(BF16)


def _qkv(x, mod, w, *, kind, batch, seq_len, tm=512, tn=512):
    t, k = x.shape
    n = w.shape[1]
    heads_per_tile = tn // ATTN_HEAD_DIM
    tiles_per_batch = seq_len // tm
    return pl.pallas_call(
        functools.partial(_qkv_body, sc=1, sh=0, kind=kind, tiles_per_batch=tiles_per_batch),
        out_shape=_sds((batch, ATTN_HEADS, seq_len, V7X_LANES), BF16),
        grid=(t // tm, n // tn),
        in_specs=[pl.BlockSpec((tm, k), lambda i, j: (i, 0)),
                  pl.BlockSpec((1, 6, k), lambda i, j: (i // tiles_per_batch, 0, 0)),
                  pl.BlockSpec((k, tn), lambda i, j: (0, j))],
        out_specs=pl.BlockSpec((1, heads_per_tile, tm, V7X_LANES),
                               lambda i, j: (i // tiles_per_batch, j, i % tiles_per_batch, 0)),
        scratch_shapes=[pltpu.VMEM((tm, k), BF16)],
        compiler_params=_cparams(("arbitrary", "arbitrary"), 40),
        name="qkv_" + kind,
    )(x, mod, w)


ATTN_HEADS_PER_STEP = 4
ATTN_KV_BLOCKS_PER_ITER = 2
GATE_ROWS = 512


def _attn_kernel(q_ref, k_ref, v_ref, pm_ref, o_ref, km_ref, qa_ref):
    tq = MOBA_BLOCK
    seq_len = q_ref.shape[2]
    n_blocks = seq_len // MOBA_BLOCK
    heads = range(ATTN_HEADS_PER_STEP)
    qi = pl.program_id(2)

    @pl.when(qi == 0)
    def _():
        for hh in heads:
            km_ref[hh] = _dot(pm_ref[...], k_ref[0, hh]).astype(BF16)
        lane = lax.broadcasted_iota(jnp.int32, (GATE_ROWS, V7X_LANES), 1)
        lane_f = lane.astype(F32)
        blk = lane - ATTN_HEAD_DIM
        mask_lanes = (blk >= 0) & (blk < n_blocks)
        row = lax.broadcasted_iota(jnp.int32, (GATE_ROWS, V7X_LANES), 0)

        def gate_rows(ci, carry):
            r0 = pl.multiple_of(ci * GATE_ROWS, GATE_ROWS)
            past = (blk >= 0) & (blk < lax.div(r0 + row, MOBA_BLOCK))
            for hh in heads:
                q = q_ref[0, hh, pl.ds(r0, GATE_ROWS), :]
                gate = jnp.where(past, _dot_nt(q, km_ref[hh]), -jnp.inf)
                chosen = jnp.zeros((GATE_ROWS, V7X_LANES), jnp.bool_)
                for _ in range(MOBA_TOPK):
                    mx = jnp.max(gate, axis=-1, keepdims=True)
                    idx = jnp.min(jnp.where(gate == mx, lane_f, float(V7X_LANES)), axis=-1, keepdims=True)
                    pick = lane_f == idx
                    chosen = chosen | pick
                    gate = jnp.where(pick, -jnp.inf, gate)
                blocked = mask_lanes & jnp.logical_not(chosen & past)
                qa_ref[hh, pl.ds(r0, GATE_ROWS), :] = (q.astype(F32) + jnp.where(blocked, NEG_INF, 0.0)).astype(BF16)
            return carry

        lax.fori_loop(0, seq_len // GATE_ROWS, gate_rows, 0)

    q0 = pl.multiple_of(qi * tq, tq)
    span = ATTN_KV_BLOCKS_PER_ITER * MOBA_BLOCK

    def online_update(m_i, acc, s, v):
        m_new = jnp.maximum(m_i, jnp.max(s, axis=-1, keepdims=True))
        p = jnp.exp(s - m_new)
        return m_new, acc * jnp.exp(m_i - m_new) + _dot(p.astype(BF16), v)

    def kv_step(j, carry):
        start = pl.multiple_of(j * span, span)
        out = []
        for hh in heads:
            m_i, acc = carry[hh]
            s = _dot_nt(qa_ref[hh, pl.ds(q0, tq), :], k_ref[0, hh, pl.ds(start, span), :])
            out.append(online_update(m_i, acc, s, v_ref[0, hh, pl.ds(start, span), :]))
        return tuple(out)

    init = tuple((jnp.full((tq, 1), NEG_INF, F32), jnp.zeros((tq, V7X_LANES), F32)) for _ in heads)
    n_iter = lax.div(qi + ATTN_KV_BLOCKS_PER_ITER - 1, ATTN_KV_BLOCKS_PER_ITER)
    carry = lax.fori_loop(0, n_iter, kv_step, init)

    row = lax.broadcasted_iota(jnp.int32, (tq, MOBA_BLOCK), 0)
    col = lax.broadcasted_iota(jnp.int32, (tq, MOBA_BLOCK), 1)
    outs = []
    for hh in heads:
        m_i, acc = carry[hh]
        s = _dot_nt(q_ref[0, hh, pl.ds(q0, tq), :], k_ref[0, hh, pl.ds(q0, MOBA_BLOCK), :])
        _, acc = online_update(m_i, acc, jnp.where(col <= row, s, NEG_INF), v_ref[0, hh, pl.ds(q0, MOBA_BLOCK), :])
        outs.append(acc / acc[:, ATTN_HEAD_DIM:ATTN_HEAD_DIM + 1])
    lane = lax.broadcasted_iota(jnp.int32, (tq, V7X_LANES), 1)
    for pair in range(ATTN_HEADS_PER_STEP // 2):
        both = jnp.where(lane < ATTN_HEAD_DIM, outs[2 * pair], pltpu.roll(outs[2 * pair + 1], ATTN_HEAD_DIM, axis=1))
        o_ref[0, :, pair * V7X_LANES:(pair + 1) * V7X_LANES] = both.astype(BF16)


def _attention(q, k, v, *, batch, seq_len):
    nb = seq_len // MOBA_BLOCK
    tq = MOBA_BLOCK
    hps = ATTN_HEADS_PER_STEP
    assert hps % 2 == 0 and 2 * ATTN_HEAD_DIM == V7X_LANES and nb <= V7X_LANES - ATTN_HEAD_DIM
    assert seq_len % GATE_ROWS == 0 and nb % ATTN_KV_BLOCKS_PER_ITER == 0
    rows = jnp.arange(V7X_LANES, dtype=jnp.int32)[:, None] - ATTN_HEAD_DIM
    cols = jnp.arange(seq_len, dtype=jnp.int32)[None, :] // MOBA_BLOCK
    pool = jnp.where(rows == cols, 1.0 / MOBA_BLOCK, 0.0).astype(BF16)
    head_spec = pl.BlockSpec((1, hps, seq_len, V7X_LANES), lambda b, h, i: (b, h, 0, 0))
    return pl.pallas_call(
        _attn_kernel,
        out_shape=_sds((batch, seq_len, ATTN_HEADS * ATTN_HEAD_DIM), BF16),
        grid=(batch, ATTN_HEADS // hps, seq_len // tq),
        in_specs=[head_spec, head_spec, head_spec,
                  pl.BlockSpec((V7X_LANES, seq_len), lambda b, h, i: (0, 0))],
        out_specs=pl.BlockSpec((1, tq, hps * ATTN_HEAD_DIM), lambda b, h, i: (b, i, h)),
        scratch_shapes=[pltpu.VMEM((hps, V7X_LANES, V7X_LANES), BF16),
                        pltpu.VMEM((hps, seq_len, V7X_LANES), BF16)],
        compiler_params=_cparams(("arbitrary", "arbitrary", "arbitrary"), 48),
        name="moba_attention",
    )(q, k, v, pool)


def kernel(x, c, w_ada, b_ada, ln_g, ln_b, ssm_w_in, ssm_conv_w, ssm_conv_b, ssm_dt_bias, ssm_a_log, ssm_d,
           ssm_norm_w, ssm_w_out, attn_w_qkv, attn_w_o, w_router, b_router, moe_w_gate, moe_w_up, moe_w_down):
    batch, seq_len, d = x.shape
    assert seq_len % 512 == 0 and seq_len % SSM_CHUNK == 0 and seq_len % MOBA_BLOCK == 0
    t = batch * seq_len
    mod = _ada_mod(c, w_ada, b_ada)
    xf = x.reshape(t, d)

    d_inner = SSM_GROUPS * GROUP_W
    conv_dim = d_inner + 2 * SSM_GROUPS * SSM_STATE
    w_in = ssm_w_in[0]
    heads = ssm_dt_bias.shape[1]
    w_main = w_in[:, :d_inner + conv_dim].astype(BF16)
    w_dt = jnp.pad(w_in[:, d_inner + conv_dim:], ((0, 0), (0, V7X_LANES - heads))).astype(BF16)
    proj = _mm_mod(xf, mod[0], w_main, sc=1, sh=0, seq_len=seq_len, out_dtype=BF16)
    dt_raw = _mm_mod(xf, mod[0], w_dt, sc=1, sh=0, seq_len=seq_len, out_dtype=F32)
    y = _ssd(proj, dt_raw, ssm_conv_w[0], ssm_conv_b[0], ssm_dt_bias[0], ssm_a_log[0], ssm_d[0], ssm_norm_w[0],
             batch=batch, seq_len=seq_len)
    x1, route, counts = _proj_ln_route(y, ssm_w_out[0].astype(BF16), xf, mod[0], ln_g[0, 0], ln_b[0, 0],
                                       w_router, b_router, seq_len=seq_len)
    xf = _moe_layer(x1, route, counts, mod[0], moe_w_gate[0], moe_w_up[0], moe_w_down[0], ln_g[0, 1], ln_b[0, 1],
                    seq_len=seq_len)

    w_qkv = attn_w_qkv[0].astype(BF16)
    q = _qkv(xf, mod[1], w_qkv[:, :d], kind="q", batch=batch, seq_len=seq_len)
    k = _qkv(xf, mod[1], w_qkv[:, d:2 * d], kind="k", batch=batch, seq_len=seq_len)
    v = _qkv(xf, mod[1], w_qkv[:, 2 * d:], kind="v", batch=batch, seq_len=seq_len)
    o = _attention(q, k, v, batch=batch, seq_len=seq_len).reshape(t, d)
    x1, route, counts = _proj_ln_route(o, attn_w_o[0].astype(BF16), xf, mod[1], ln_g[1, 0], ln_b[1, 0],
                                       w_router, b_router, seq_len=seq_len)
    xf = _moe_layer(x1, route, counts, mod[1], moe_w_gate[1], moe_w_up[1], moe_w_down[1], ln_g[1, 1], ln_b[1, 1],
                    seq_len=seq_len)
    return xf.reshape(batch, seq_len, d)
```

```python
import functools

import jax
import jax.numpy as jnp
from jax import lax
from jax.experimental import pallas as pl
from jax.experimental.pallas import tpu as pltpu

F32 = jnp.float32
BF16 = jnp.bfloat16

V7X_LANES = 128
V7X_SUBLANES = 8
V7X_VMEM_BYTES = 64 * 1024 * 1024

DEPTH = 2
SSM_HEAD_DIM = 64
SSM_STATE = 128
SSM_GROUPS = 8
SSM_HEADS_PER_GROUP = 4
SSM_CONV = 4
SSM_CHUNK = 256
ATTN_HEADS = 16
ATTN_HEAD_DIM = 64
MOBA_BLOCK = 256
MOBA_TOPK = 3
N_EXPERTS = 32
N_EXPERT_GROUPS = 4
EXPERTS_PER_GROUP = N_EXPERTS // N_EXPERT_GROUPS
DEEPNORM_ALPHA = (2.0 * DEPTH) ** 0.25
LN_EPS = 1e-5
RMS_EPS = 1e-5
NEG_INF = -1e30

GROUP_W = SSM_HEADS_PER_GROUP * SSM_HEAD_DIM
CONV_HALO = V7X_SUBLANES
MOE_ROW_TILE = 256
TOKEN_TILE = 256
DMA_UNROLL = 8


def _cparams(semantics, vmem_mib):
    assert vmem_mib * 1024 * 1024 <= V7X_VMEM_BYTES
    return pltpu.CompilerParams(dimension_semantics=semantics, vmem_limit_bytes=vmem_mib * 1024 * 1024)


def _sds(shape, dtype):
    return jax.ShapeDtypeStruct(shape, dtype)


def _dot(a, b):
    return jnp.dot(a, b, preferred_element_type=F32)


def _dot_nt(a, b):
    return lax.dot_general(a, b, (((1,), (1,)), ((), ())), preferred_element_type=F32)


def _dot_tn(a, b):
    return lax.dot_general(a, b, (((0,), (0,)), ((), ())), preferred_element_type=F32)


def _split3(a):
    hi = a.astype(BF16)
    r1 = a - hi.astype(F32)
    mid = r1.astype(BF16)
    lo = (r1 - mid.astype(F32)).astype(BF16)
    return hi, mid, lo


def _silu(x):
    return x * jax.nn.sigmoid(x)


def _layer_norm(v, gamma, beta):
    mu = jnp.mean(v, axis=-1, keepdims=True)
    d = v - mu
    var = jnp.mean(d * d, axis=-1, keepdims=True)
    return d * lax.rsqrt(var + LN_EPS) * gamma + beta


def _ada_kernel(c_ref, w_ref, b_ref, o_ref):
    cs = _silu(c_ref[...])
    o_ref[0] = jnp.dot(cs, w_ref[0], preferred_element_type=F32, precision=lax.Precision.HIGHEST) + b_ref[0]


def _ada_mod(c, w_ada, b_ada):
    depth, d, n = w_ada.shape
    b = c.shape[0]
    tn = 1024
    out = pl.pallas_call(
        _ada_kernel,
        out_shape=_sds((depth, b, n), F32),
        grid=(depth, n // tn),
        in_specs=[pl.BlockSpec((b, d), lambda l, j: (0, 0)),
                  pl.BlockSpec((1, d, tn), lambda l, j: (l, 0, j)),
                  pl.BlockSpec((1, 1, tn), lambda l, j: (l, 0, j))],
        out_specs=pl.BlockSpec((1, b, tn), lambda l, j: (l, 0, j)),
        compiler_params=_cparams(("arbitrary", "arbitrary"), 32),
        name="ada_mod",
    )(c, w_ada, b_ada.reshape(depth, 1, n))
    return out.reshape(depth, b, 6, d)


def _modulate_into(hm_ref, x_ref, mod_ref, sc, sh):
    m = mod_ref[0]
    hm_ref[...] = (x_ref[...] * (1.0 + m[sc:sc + 1, :]) + m[sh:sh + 1, :]).astype(BF16)


def _mm_mod_kernel(x_ref, mod_ref, w_ref, o_ref, hm_ref, *, sc, sh):
    @pl.when(pl.program_id(1) == 0)
    def _():
        _modulate_into(hm_ref, x_ref, mod_ref, sc, sh)

    o_ref[...] = _dot(hm_ref[...], w_ref[...]).astype(o_ref.dtype)


def _mm_mod(x, mod, w, *, sc, sh, seq_len, out_dtype, tm=2048, tn=1536):
    t, k = x.shape
    n = w.shape[1]
    tn = min(tn, n)
    assert seq_len % tm == 0 and n % tn == 0
    tiles_per_batch = seq_len // tm
    return pl.pallas_call(
        functools.partial(_mm_mod_kernel, sc=sc, sh=sh),
        out_shape=_sds((t, n), out_dtype),
        grid=(t // tm, n // tn),
        in_specs=[pl.BlockSpec((tm, k), lambda i, j: (i, 0)),
                  pl.BlockSpec((1, 6, k), lambda i, j: (i // tiles_per_batch, 0, 0)),
                  pl.BlockSpec((k, tn), lambda i, j: (0, j))],
        out_specs=pl.BlockSpec((tm, tn), lambda i, j: (i, j)),
        scratch_shapes=[pltpu.VMEM((tm, k), BF16)],
        compiler_params=_cparams(("arbitrary", "arbitrary"), 48),
        name="mm_mod",
    )(x, mod, w)


def _ssd_kernel(z_ref, xp_ref, bcp_ref, dtr_ref, cw_ref, cb_ref, dtb_ref, alog_ref, dsk_ref, nw_ref,
                o_ref, ubuf, act, state, acum_t):
    lc = SSM_CHUNK
    d_inner = xp_ref.shape[1]
    n_slabs = ubuf.shape[0]
    x_slabs = d_inner // V7X_LANES
    chunk = pl.program_id(1)

    @pl.when(chunk == 0)
    def _():
        ubuf[:, 0:CONV_HALO, :] = jnp.zeros((n_slabs, CONV_HALO, V7X_LANES), F32)
        state[...] = jnp.zeros_like(state)

    def conv_slabs(src_ref, first_slab, count):
        def slab_body(j, carry):
            src_off = pl.multiple_of(j * V7X_LANES, V7X_LANES)
            slab = first_slab + j
            off = pl.multiple_of(slab * V7X_LANES, V7X_LANES)
            u = src_ref[:, pl.ds(src_off, V7X_LANES)].astype(F32)
            ubuf[slab, CONV_HALO:CONV_HALO + lc, :] = u
            acc = cb_ref[:, pl.ds(off, V7X_LANES)] + cw_ref[SSM_CONV - 1:SSM_CONV, pl.ds(off, V7X_LANES)] * u
            for k in range(SSM_CONV - 1):
                tap = CONV_HALO - (SSM_CONV - 1) + k
                acc = acc + cw_ref[k:k + 1, pl.ds(off, V7X_LANES)] * ubuf[slab, tap:tap + lc, :]
            act[:, pl.ds(off, V7X_LANES)] = _silu(acc)
            ubuf[slab, 0:CONV_HALO, :] = u[lc - CONV_HALO:lc, :]
            return carry

        lax.fori_loop(0, count, slab_body, 0)

    conv_slabs(xp_ref, 0, x_slabs)
    conv_slabs(bcp_ref, x_slabs, n_slabs - x_slabs)

    dt = jax.nn.softplus(dtr_ref[...] + dtb_ref[...])
    da = dt * (-jnp.exp(alog_ref[...]))
    row = lax.broadcasted_iota(jnp.int32, (lc, lc), 0)
    col = lax.broadcasted_iota(jnp.int32, (lc, lc), 1)
    causal = col <= row
    tril = causal.astype(BF16)
    acum = sum(_dot(tril, part) for part in _split3(da))
    acum_t[...] = acum.T
    acum_parts = _split3(acum)
    dt_bf16 = dt.astype(BF16)
    head_of_col = lax.broadcasted_iota(jnp.int32, (lc, GROUP_W), 1) // SSM_HEAD_DIM
    sel_row = lax.broadcasted_iota(jnp.int32, (V7X_LANES, GROUP_W), 0)
    sel_col = lax.broadcasted_iota(jnp.int32, (V7X_LANES, GROUP_W), 1) // SSM_HEAD_DIM

    def group_body(g, carry):
        xo = pl.multiple_of(g * GROUP_W, GROUP_W)
        bo = pl.multiple_of(d_inner + g * SSM_STATE, SSM_STATE)
        co = pl.multiple_of(d_inner + SSM_GROUPS * SSM_STATE + g * SSM_STATE, SSM_STATE)
        x_g = act[:, pl.ds(xo, GROUP_W)]
        b_g = act[:, pl.ds(bo, SSM_STATE)].astype(BF16)
        c_g = act[:, pl.ds(co, SSM_STATE)].astype(BF16)
        sel = (sel_row == SSM_HEADS_PER_GROUP * g + sel_col).astype(BF16)
        ab = sum(_dot(part, sel) for part in acum_parts)
        dtb = _dot(dt_bf16, sel)
        alast = ab[lc - 1:lc, :]
        cb = _dot_nt(c_g, b_g)
        xdt = x_g * dtb
        y = jnp.zeros((lc, GROUP_W), F32)
        for r in range(SSM_HEADS_PER_GROUP):
            acol = ab[:, r * SSM_HEAD_DIM:r * SSM_HEAD_DIM + 1]
            arow = acum_t[pl.ds(SSM_HEADS_PER_GROUP * g + r, 1), :]
            decay = jnp.exp(jnp.where(causal, acol - arow, -jnp.inf))
            w = (cb * decay).astype(BF16)
            x_r = jnp.where(head_of_col == r, xdt, 0.0).astype(BF16)
            y = y + _dot(w, x_r)
        st = state[:, pl.ds(xo, GROUP_W)]
        y = y + _dot(c_g, st.astype(BF16)) * jnp.exp(ab)
        to_end = jnp.exp(alast - ab) * dtb
        xw = (x_g * to_end).astype(BF16)
        state[:, pl.ds(xo, GROUP_W)] = st * jnp.exp(alast) + _dot_tn(b_g, xw)
        y = y + dsk_ref[:, pl.ds(xo, GROUP_W)] * x_g
        gated = y * _silu(z_ref[:, pl.ds(xo, GROUP_W)].astype(F32))
        ms = jnp.mean(gated * gated, axis=-1, keepdims=True)
        o_ref[:, pl.ds(xo, GROUP_W)] = (gated * lax.rsqrt(ms + RMS_EPS) * nw_ref[:, pl.ds(xo, GROUP_W)]).astype(BF16)
        return carry

    lax.fori_loop(0, SSM_GROUPS, group_body, 0)


def _ssd(proj, dt_raw, conv_w, conv_b, dt_bias, a_log, d_skip, norm_w, *, batch, seq_len):
    t = proj.shape[0]
    d_inner = SSM_GROUPS * GROUP_W
    conv_dim = d_inner + 2 * SSM_GROUPS * SSM_STATE
    assert proj.shape[1] == d_inner + conv_dim and conv_dim == 2 * d_inner
    nc = seq_len // SSM_CHUNK
    lc = SSM_CHUNK
    pad = V7X_LANES - dt_bias.shape[0]
    heads = dt_bias.shape[0]
    row_map = lambda b, c: b * nc + c
    const = lambda b, c: (0, 0)
    return pl.pallas_call(
        _ssd_kernel,
        out_shape=_sds((t, d_inner), BF16),
        grid=(batch, nc),
        in_specs=[pl.BlockSpec((lc, d_inner), lambda b, c: (row_map(b, c), 0)),
                  pl.BlockSpec((lc, d_inner), lambda b, c: (row_map(b, c), 1)),
                  pl.BlockSpec((lc, d_inner), lambda b, c: (row_map(b, c), 2)),
                  pl.BlockSpec((lc, V7X_LANES), lambda b, c: (row_map(b, c), 0)),
                  pl.BlockSpec((SSM_CONV, conv_dim), const),
                  pl.BlockSpec((1, conv_dim), const),
                  pl.BlockSpec((1, V7X_LANES), const),
                  pl.BlockSpec((1, V7X_LANES), const),
                  pl.BlockSpec((1, d_inner), const),
                  pl.BlockSpec((1, d_inner), const)],
        out_specs=pl.BlockSpec((lc, d_inner), lambda b, c: (row_map(b, c), 0)),
        scratch_shapes=[pltpu.VMEM((conv_dim // V7X_LANES, CONV_HALO + lc, V7X_LANES), F32),
                        pltpu.VMEM((lc, conv_dim), F32),
                        pltpu.VMEM((SSM_STATE, d_inner), F32),
                        pltpu.VMEM((V7X_LANES, lc), F32)],
        compiler_params=_cparams(("arbitrary", "arbitrary"), 48),
        name="ssd_chunk_scan",
    )(proj, proj, proj, dt_raw, conv_w, conv_b.reshape(1, conv_dim),
      jnp.pad(dt_bias, (0, pad)).reshape(1, V7X_LANES), jnp.pad(a_log, (0, pad)).reshape(1, V7X_LANES),
      jnp.repeat(d_skip, SSM_HEAD_DIM).reshape(1, heads * SSM_HEAD_DIM), norm_w.reshape(1, d_inner))


def _route(logits, carry_ref):
    tm = logits.shape[0]
    lane = lax.broadcasted_iota(jnp.int32, (tm, V7X_LANES), 1)
    lane_f = lane.astype(F32)
    lg = jnp.where(lane < N_EXPERTS, logits, -jnp.inf)
    e = jnp.exp(lg - jnp.max(lg, axis=-1, keepdims=True))
    far = float(V7X_LANES)
    best = None
    for g in range(N_EXPERT_GROUPS):
        in_g = (lane >= g * EXPERTS_PER_GROUP) & (lane < (g + 1) * EXPERTS_PER_GROUP)
        eg = jnp.where(in_g, e, -1.0)
        m1 = jnp.max(eg, axis=-1, keepdims=True)
        i1 = jnp.min(jnp.where(eg == m1, lane_f, far), axis=-1, keepdims=True)
        eg2 = jnp.where(lane_f == i1, -1.0, eg)
        m2 = jnp.max(eg2, axis=-1, keepdims=True)
        i2 = jnp.min(jnp.where(eg2 == m2, lane_f, far), axis=-1, keepdims=True)
        cand = (m1 + m2, m1, m2, i1, i2)
        if best is None:
            best = cand
        else:
            take = cand[0] > best[0]
            best = tuple(jnp.where(take, c, b) for c, b in zip(cand, best))
    _, m1, m2, i1, i2 = best
    denom = m1 + m2
    hit1 = lane_f == i1
    hit2 = lane_f == i2
    onehot = (hit1 | hit2).astype(BF16)
    row = lax.broadcasted_iota(jnp.int32, (tm, tm), 0)
    col = lax.broadcasted_iota(jnp.int32, (tm, tm), 1)
    before = (col < row).astype(BF16)
    rank = _dot(before, onehot) + carry_ref[0:1, :]
    r1 = jnp.sum(jnp.where(hit1, rank, 0.0), axis=-1, keepdims=True)
    r2 = jnp.sum(jnp.where(hit2, rank, 0.0), axis=-1, keepdims=True)
    carry_ref[0:1, :] = carry_ref[0:1, :] + jnp.sum(onehot.astype(F32), axis=0, keepdims=True)
    rec = jnp.zeros((tm, V7X_LANES), F32)
    for k, val in enumerate((i1, i2, m1 / denom, m2 / denom, r1, r2)):
        rec = jnp.where(lane == k, val, rec)
    return rec


def _proj_ln_route_kernel(a_ref, w_ref, xres_ref, mod_ref, lng_ref, lnb_ref, wr_ref, br_ref,
                          x_ref, route_ref, cnt_ref, carry_ref, *, gate_idx, sc, sh):
    @pl.when(pl.program_id(0) == 0)
    def _():
        carry_ref[...] = jnp.zeros_like(carry_ref)

    m = mod_ref[0]
    y = _dot(a_ref[...], w_ref[...])
    v = DEEPNORM_ALPHA * xres_ref[...] + (1.0 + m[gate_idx:gate_idx + 1, :]) * y
    x1 = _layer_norm(v, lng_ref[...], lnb_ref[...])
    x_ref[...] = x1
    hm = (x1 * (1.0 + m[sc:sc + 1, :]) + m[sh:sh + 1, :]).astype(BF16)
    logits = _dot(hm, wr_ref[...]) + br_ref[...]
    route_ref[...] = _route(logits, carry_ref)
    cnt_ref[...] = carry_ref[...]


def _proj_ln_route(a, w, xres, mod, ln_g, ln_b, w_router, b_router, *, seq_len):
    t, k = a.shape
    d = w.shape[1]
    tm = TOKEN_TILE
    tiles_per_batch = seq_len // tm
    const = lambda i: (0, 0)
    wr = jnp.pad(w_router, ((0, 0), (0, V7X_LANES - N_EXPERTS))).astype(BF16)
    br = jnp.pad(b_router, (0, V7X_LANES - N_EXPERTS)).reshape(1, V7X_LANES)
    return pl.pallas_call(
        functools.partial(_proj_ln_route_kernel, gate_idx=2, sc=4, sh=3),
        out_shape=(_sds((t, d), F32), _sds((t, V7X_LANES), F32), _sds((V7X_SUBLANES, V7X_LANES), F32)),
        grid=(t // tm,),
        in_specs=[pl.BlockSpec((tm, k), lambda i: (i, 0)),
                  pl.BlockSpec((k, d), const),
                  pl.BlockSpec((tm, d), lambda i: (i, 0)),
                  pl.BlockSpec((1, 6, d), lambda i: (i // tiles_per_batch, 0, 0)),
                  pl.BlockSpec((1, d), const),
                  pl.BlockSpec((1, d), const),
                  pl.BlockSpec((d, V7X_LANES), const),
                  pl.BlockSpec((1, V7X_LANES), const)],
        out_specs=(pl.BlockSpec((tm, d), lambda i: (i, 0)),
                   pl.BlockSpec((tm, V7X_LANES), lambda i: (i, 0)),
                   pl.BlockSpec((V7X_SUBLANES, V7X_LANES), const)),
        scratch_shapes=[pltpu.VMEM((V7X_SUBLANES, V7X_LANES), F32)],
        compiler_params=_cparams(("arbitrary",), 40),
        name="proj_ln_route",
    )(a, w, xres, mod, ln_g.reshape(1, d), ln_b.reshape(1, d), wr, br)


def _row_copy(src_ref, src_row, dst_ref, dst_row, sem):
    return pltpu.make_async_copy(src_ref.at[pl.ds(src_row, 1)], dst_ref.at[pl.ds(dst_row, 1)], sem)


def _dispatch_kernel(pos_ref, x_ref, mod_ref, xs_init_ref, xs_ref, hm_ref, sem, *, sc, sh):
    del xs_init_ref
    tt = x_ref.shape[0]
    m = mod_ref[0]
    hm_ref[...] = x_ref[...] * (1.0 + m[sc:sc + 1, :]) + m[sh:sh + 1, :]

    def issue(blk, carry):
        for u in range(DMA_UNROLL):
            r = blk * DMA_UNROLL + u
            _row_copy(hm_ref, r, xs_ref, pos_ref[0, 0, 2 * r], sem).start(priority=0)
            _row_copy(hm_ref, r, xs_ref, pos_ref[0, 0, 2 * r + 1], sem).start(priority=1)
        return carry

    def drain(blk, carry):
        for _ in range(2 * DMA_UNROLL):
            _row_copy(hm_ref, 0, xs_ref, 0, sem).wait()
        return carry

    lax.fori_loop(0, tt // DMA_UNROLL, issue, 0)
    lax.fori_loop(0, tt // DMA_UNROLL, drain, 0)


def _dispatch(x, mod, pos, n_rows, *, seq_len):
    t, d = x.shape
    tt = TOKEN_TILE
    tiles_per_batch = seq_len // tt
    return pl.pallas_call(
        functools.partial(_dispatch_kernel, sc=4, sh=3),
        out_shape=_sds((n_rows, d), F32),
        grid=(t // tt,),
        in_specs=[pl.BlockSpec((1, 1, 2 * tt), lambda i: (i, 0, 0), memory_space=pltpu.SMEM),
                  pl.BlockSpec((tt, d), lambda i: (i, 0)),
                  pl.BlockSpec((1, 6, d), lambda i: (i // tiles_per_batch, 0, 0)),
                  pl.BlockSpec(memory_space=pl.ANY)],
        out_specs=pl.BlockSpec(memory_space=pl.ANY),
        scratch_shapes=[pltpu.VMEM((tt, d), F32), pltpu.SemaphoreType.DMA(())],
        input_output_aliases={3: 0},
        compiler_params=_cparams(("arbitrary",), 32),
        name="moe_dispatch",
    )(pos.reshape(t // tt, 1, 2 * tt), x, mod, jnp.zeros((n_rows, d), F32))


def _ffn_kernel(te_ref, nu_ref, x_ref, wg_ref, wu_ref, wd_ref, o_ref, wg_bf, wu_bf, wd_bf):
    i = pl.program_id(0)
    used = i < nu_ref[0]
    new_expert = jnp.logical_or(i == 0, te_ref[i] != te_ref[jnp.maximum(i - 1, 0)])

    @pl.when(jnp.logical_and(used, new_expert))
    def _():
        wg_bf[...] = wg_ref[0].astype(BF16)
        wu_bf[...] = wu_ref[0].astype(BF16)
        wd_bf[...] = wd_ref[0].astype(BF16)

    @pl.when(used)
    def _():
        x = x_ref[...].astype(BF16)
        hg = _dot(x, wg_bf[...])
        hu = _dot(x, wu_bf[...])
        o_ref[...] = _dot((_silu(hg) * hu).astype(BF16), wd_bf[...])

    @pl.when(jnp.logical_not(used))
    def _():
        o_ref[...] = jnp.zeros_like(o_ref)


def _ffn(xs, tile_expert, n_used, w_gate, w_up, w_down):
    p, d = xs.shape
    f = w_gate.shape[2]
    tm = MOE_ROW_TILE
    grid_spec = pltpu.PrefetchScalarGridSpec(
        num_scalar_prefetch=2,
        grid=(p // tm,),
        in_specs=[pl.BlockSpec((tm, d), lambda i, te, nu: (i, 0)),
                  pl.BlockSpec((1, d, f), lambda i, te, nu: (te[i], 0, 0)),
                  pl.BlockSpec((1, d, f), lambda i, te, nu: (te[i], 0, 0)),
                  pl.BlockSpec((1, f, d), lambda i, te, nu: (te[i], 0, 0))],
        out_specs=pl.BlockSpec((tm, d), lambda i, te, nu: (i, 0)),
        scratch_shapes=[pltpu.VMEM((d, f), BF16), pltpu.VMEM((d, f), BF16), pltpu.VMEM((f, d), BF16)],
    )
    return pl.pallas_call(
        _ffn_kernel,
        out_shape=_sds((p, d), F32),
        grid_spec=grid_spec,
        compiler_params=_cparams(("arbitrary",), 48),
        name="moe_ffn",
    )(tile_expert, n_used, xs, w_gate, w_up, w_down)


def _combine_ln_kernel(pos_ref, xres_ref, route_ref, mod_ref, lng_ref, lnb_ref, ye_ref, o_ref, buf, sem, *, gate_idx):
    tt = xres_ref.shape[0]

    def issue(blk, carry):
        for u in range(DMA_UNROLL):
            r = blk * DMA_UNROLL + u
            _row_copy(ye_ref, pos_ref[0, 0, 2 * r], buf.at[0], r, sem).start(priority=0)
            _row_copy(ye_ref, pos_ref[0, 0, 2 * r + 1], buf.at[1], r, sem).start(priority=1)
        return carry

    def drain(blk, carry):
        for _ in range(2 * DMA_UNROLL):
            _row_copy(ye_ref, 0, buf.at[0], 0, sem).wait()
        return carry

    lax.fori_loop(0, tt // DMA_UNROLL, issue, 0)
    lax.fori_loop(0, tt // DMA_UNROLL, drain, 0)
    m = mod_ref[0]
    rec = route_ref[...]
    y = rec[:, 2:3] * buf[0] + rec[:, 3:4] * buf[1]
    v = DEEPNORM_ALPHA * xres_ref[...] + (1.0 + m[gate_idx:gate_idx + 1, :]) * y
    o_ref[...] = _layer_norm(v, lng_ref[...], lnb_ref[...])


def _combine_ln(xres, route, pos, ye, mod, ln_g, ln_b, *, seq_len):
    t, d = xres.shape
    tt = TOKEN_TILE
    tiles_per_batch = seq_len // tt
    const = lambda i: (0, 0)
    return pl.pallas_call(
        functools.partial(_combine_ln_kernel, gate_idx=5),
        out_shape=_sds((t, d), F32),
        grid=(t // tt,),
        in_specs=[pl.BlockSpec((1, 1, 2 * tt), lambda i: (i, 0, 0), memory_space=pltpu.SMEM),
                  pl.BlockSpec((tt, d), lambda i: (i, 0)),
                  pl.BlockSpec((tt, V7X_LANES), lambda i: (i, 0)),
                  pl.BlockSpec((1, 6, d), lambda i: (i // tiles_per_batch, 0, 0)),
                  pl.BlockSpec((1, d), const),
                  pl.BlockSpec((1, d), const),
                  pl.BlockSpec(memory_space=pl.ANY)],
        out_specs=pl.BlockSpec((tt, d), lambda i: (i, 0)),
        scratch_shapes=[pltpu.VMEM((2, tt, d), F32), pltpu.SemaphoreType.DMA(())],
        compiler_params=_cparams(("arbitrary",), 32),
        name="moe_combine_ln",
    )(pos.reshape(t // tt, 1, 2 * tt), xres, route, mod, ln_g.reshape(1, d), ln_b.reshape(1, d), ye)


def _moe_tables(route, counts):
    t = route.shape[0]
    tm = MOE_ROW_TILE
    max_tiles = (2 * t) // tm + N_EXPERTS
    cnt = counts[0, :N_EXPERTS].astype(jnp.int32)
    tiles_e = (cnt + tm - 1) // tm
    tile_end = jnp.cumsum(tiles_e)
    row_off = (tile_end - tiles_e) * tm
    ids = route[:, 0:2].astype(jnp.int32)
    ranks = route[:, 4:6].astype(jnp.int32)
    pos = (row_off[ids] + ranks).reshape(-1)
    n_used = tile_end[-1:]
    tile_ids = jnp.minimum(jnp.arange(max_tiles, dtype=jnp.int32), n_used - 1)
    tile_expert = jnp.sum(tile_ids[:, None] >= tile_end[None, :], axis=1).astype(jnp.int32)
    return pos, tile_expert, n_used.astype(jnp.int32), max_tiles * tm


def _moe_layer(x1, route, counts, mod, w_gate, w_up, w_down, ln_g, ln_b, *, seq_len):
    pos, tile_expert, n_used, n_rows = _moe_tables(route, counts)
    xs = _dispatch(x1, mod, pos, n_rows, seq_len=seq_len)
    ye = _ffn(xs, tile_expert, n_used, w_gate, w_up, w_down)
    return _combine_ln(x1, route, pos, ye, mod, ln_g, ln_b, seq_len=seq_len)


def _qkv_body(x_ref, mod_ref, w_ref, o_ref, hm_ref, *, sc, sh, tiles_per_batch, tiles_per_kind):
    @pl.when(pl.program_id(1) == 0)
    def _():
        _modulate_into(hm_ref, x_ref, mod_ref, sc, sh)

    tm = x_ref.shape[0]
    kind = pl.program_id(1) // tiles_per_kind
    acc = _dot(hm_ref[...], w_ref[...])
    acc = acc * jnp.where(kind == 0, ATTN_HEAD_DIM ** -0.5, 1.0)
    lane = lax.broadcasted_iota(jnp.int32, (tm, V7X_LANES), 1)
    low = lane < ATTN_HEAD_DIM
    row = lax.broadcasted_iota(jnp.int32, (tm, V7X_LANES), 0)
    pos = lax.rem(pl.program_id(0), tiles_per_batch) * tm + row
    k_extra = (lane == ATTN_HEAD_DIM + lax.div(pos, MOBA_BLOCK)).astype(F32)
    v_extra = (lane == ATTN_HEAD_DIM).astype(F32)
    extra = jnp.where(kind == 1, k_extra, jnp.where(kind == 2, v_extra, 0.0))
    for p in range(acc.shape[1] // V7X_LANES):
        t2 = acc[:, p * V7X_LANES:(p + 1) * V7X_LANES]
        even = jnp.where(low, t2, extra)
        odd = jnp.where(low, pltpu.roll(t2, ATTN_HEAD_DIM, axis=1), extra)
        o_ref[0, 0, 2 * p] = even.astype(BF16)
        o_ref[0, 0, 2 * p + 1] = odd.astype(BF16)


def _qkv(x, mod, w, *, batch, seq_len, tm=1024, tn=512):
    t, k = x.shape
    n = w.shape[1]
    heads_per_tile = tn // ATTN_HEAD_DIM
    tiles_per_batch = seq_len // tm
    tiles_per_kind = ATTN_HEADS // heads_per_tile
    assert n == 3 * ATTN_HEADS * ATTN_HEAD_DIM and seq_len % tm == 0
    return pl.pallas_call(
        functools.partial(_qkv_body, sc=1, sh=0, tiles_per_batch=tiles_per_batch, tiles_per_kind=tiles_per_kind),
        out_shape=_sds((3, batch, ATTN_HEADS, seq_len, V7X_LANES), BF16),
        grid=(t // tm, n // tn),
        in_specs=[pl.BlockSpec((tm, k), lambda i, j: (i, 0)),
                  pl.BlockSpec((1, 6, k), lambda i, j: (i // tiles_per_batch, 0, 0)),
                  pl.BlockSpec((k, tn), lambda i, j: (0, j))],
        out_specs=pl.BlockSpec((1, 1, heads_per_tile, tm, V7X_LANES),
                               lambda i, j: (j // tiles_per_kind, i // tiles_per_batch, j % tiles_per_kind,
                                             i % tiles_per_batch, 0)),
        scratch_shapes=[pltpu.VMEM((tm, k), BF16)],
        compiler_params=_cparams(("arbitrary", "arbitrary"), 40),
        name="qkv_proj",
    )(x, mod, w)


ATTN_HEADS_PER_STEP = 2
ATTN_Q_TILE = 1024
GATE_ROWS = 512


def _attn_kernel(q_ref, k_ref, v_ref, pm_ref, o_ref, km_ref, qa_ref):
    tq = ATTN_Q_TILE
    seq_len = q_ref.shape[2]
    n_blocks = seq_len // MOBA_BLOCK
    heads = range(ATTN_HEADS_PER_STEP)
    qi = pl.program_id(2)

    @pl.when(qi == 0)
    def _():
        for hh in heads:
            km_ref[hh] = _dot(pm_ref[...], k_ref[0, hh]).astype(BF16)
        lane = lax.broadcasted_iota(jnp.int32, (GATE_ROWS, V7X_LANES), 1)
        lane_f = lane.astype(F32)
        blk = lane - ATTN_HEAD_DIM
        mask_lanes = (blk >= 0) & (blk < n_blocks)
        row = lax.broadcasted_iota(jnp.int32, (GATE_ROWS, V7X_LANES), 0)

        def gate_rows(ci, carry):
            r0 = pl.multiple_of(ci * GATE_ROWS, GATE_ROWS)
            past = (blk >= 0) & (blk < lax.div(r0 + row, MOBA_BLOCK))
            for hh in heads:
                q = q_ref[0, hh, pl.ds(r0, GATE_ROWS), :]
                gate = jnp.where(past, _dot_nt(q, km_ref[hh]), -jnp.inf)
                chosen = jnp.zeros((GATE_ROWS, V7X_LANES), jnp.bool_)
                for _ in range(MOBA_TOPK):
                    mx = jnp.max(gate, axis=-1, keepdims=True)
                    idx = jnp.min(jnp.where(gate == mx, lane_f, float(V7X_LANES)), axis=-1, keepdims=True)
                    pick = lane_f == idx
                    chosen = chosen | pick
                    gate = jnp.where(pick, -jnp.inf, gate)
                own = blk == lax.div(r0 + row, MOBA_BLOCK)
                blocked = mask_lanes & jnp.logical_not((chosen & past) | own)
                qa_ref[hh, pl.ds(r0, GATE_ROWS), :] = (q.astype(F32) + jnp.where(blocked, NEG_INF, 0.0)).astype(BF16)
            return carry

        lax.fori_loop(0, seq_len // GATE_ROWS, gate_rows, 0)

    q0 = pl.multiple_of(qi * tq, tq)

    def online_update(m_i, acc, s, v):
        m_new = jnp.maximum(m_i, jnp.max(s, axis=-1, keepdims=True))
        p = jnp.exp(s - m_new)
        return m_new, acc * jnp.exp(m_i - m_new) + _dot(p.astype(BF16), v)

    def kv_step(j, carry):
        start = pl.multiple_of(j * tq, tq)
        out = []
        for hh in heads:
            m_i, acc = carry[hh]
            s = _dot_nt(qa_ref[hh, pl.ds(q0, tq), :], k_ref[0, hh, pl.ds(start, tq), :])
            out.append(online_update(m_i, acc, s, v_ref[0, hh, pl.ds(start, tq), :]))
        return tuple(out)

    init = tuple((jnp.full((tq, 1), NEG_INF, F32), jnp.zeros((tq, V7X_LANES), F32)) for _ in heads)
    carry = lax.fori_loop(0, qi, kv_step, init)

    row = lax.broadcasted_iota(jnp.int32, (tq, tq), 0)
    col = lax.broadcasted_iota(jnp.int32, (tq, tq), 1)
    outs = []
    for hh in heads:
        m_i, acc = carry[hh]
        s = _dot_nt(qa_ref[hh, pl.ds(q0, tq), :], k_ref[0, hh, pl.ds(q0, tq), :])
        _, acc = online_update(m_i, acc, jnp.where(col <= row, s, NEG_INF), v_ref[0, hh, pl.ds(q0, tq), :])
        outs.append(acc / acc[:, ATTN_HEAD_DIM:ATTN_HEAD_DIM + 1])
    lane = lax.broadcasted_iota(jnp.int32, (tq, V7X_LANES), 1)
    for pair in range(ATTN_HEADS_PER_STEP // 2):
        both = jnp.where(lane < ATTN_HEAD_DIM, outs[2 * pair], pltpu.roll(outs[2 * pair + 1], ATTN_HEAD_DIM, axis=1))
        o_ref[0, :, pair * V7X_LANES:(pair + 1) * V7X_LANES] = both.astype(BF16)


def _attention(qkv, *, batch, seq_len):
    nb = seq_len // MOBA_BLOCK
    tq = ATTN_Q_TILE
    hps = ATTN_HEADS_PER_STEP
    assert hps % 2 == 0 and 2 * ATTN_HEAD_DIM == V7X_LANES and nb <= V7X_LANES - ATTN_HEAD_DIM
    assert seq_len % GATE_ROWS == 0 and seq_len % tq == 0 and tq % MOBA_BLOCK == 0
    rows = jnp.arange(V7X_LANES, dtype=jnp.int32)[:, None] - ATTN_HEAD_DIM
    cols = jnp.arange(seq_len, dtype=jnp.int32)[None, :] // MOBA_BLOCK
    pool = jnp.where(rows == cols, 1.0 / MOBA_BLOCK, 0.0).astype(BF16)
    def head_spec(kind):
        return pl.BlockSpec((None, 1, hps, seq_len, V7X_LANES), lambda b, h, i: (kind, b, h, 0, 0))

    return pl.pallas_call(
        _attn_kernel,
        out_shape=_sds((batch, seq_len, ATTN_HEADS * ATTN_HEAD_DIM), BF16),
        grid=(batch, ATTN_HEADS // hps, seq_len // tq),
        in_specs=[head_spec(0), head_spec(1), head_spec(2),
                  pl.BlockSpec((V7X_LANES, seq_len), lambda b, h, i: (0, 0))],
        out_specs=pl.BlockSpec((1, tq, hps * ATTN_HEAD_DIM), lambda b, h, i: (b, i, h)),
        scratch_shapes=[pltpu.VMEM((hps, V7X_LANES, V7X_LANES), BF16),
                        pltpu.VMEM((hps, seq_len, V7X_LANES), BF16)],
        compiler_params=_cparams(("arbitrary", "arbitrary", "arbitrary"), 48),
        name="moba_attention",
    )(qkv, qkv, qkv, pool)


def kernel(x, c, w_ada, b_ada, ln_g, ln_b, ssm_w_in, ssm_conv_w, ssm_conv_b, ssm_dt_bias, ssm_a_log, ssm_d,
           ssm_norm_w, ssm_w_out, attn_w_qkv, attn_w_o, w_router, b_router, moe_w_gate, moe_w_up, moe_w_down):
    batch, seq_len, d = x.shape
    assert seq_len % 512 == 0 and seq_len % SSM_CHUNK == 0 and seq_len % MOBA_BLOCK == 0
    t = batch * seq_len
    mod = _ada_mod(c, w_ada, b_ada)
    xf = x.reshape(t, d)

    d_inner = SSM_GROUPS * GROUP_W
    conv_dim = d_inner + 2 * SSM_GROUPS * SSM_STATE
    w_in = ssm_w_in[0]
    heads = ssm_dt_bias.shape[1]
    w_main = w_in[:, :d_inner + conv_dim].astype(BF16)
    w_dt = jnp.pad(w_in[:, d_inner + conv_dim:], ((0, 0), (0, V7X_LANES - heads))).astype(BF16)
    proj = _mm_mod(xf, mod[0], w_main, sc=1, sh=0, seq_len=seq_len, out_dtype=BF16)
    dt_raw = _mm_mod(xf, mod[0], w_dt, sc=1, sh=0, seq_len=seq_len, out_dtype=F32)
    y = _ssd(proj, dt_raw, ssm_conv_w[0], ssm_conv_b[0], ssm_dt_bias[0], ssm_a_log[0], ssm_d[0], ssm_norm_w[0],
             batch=batch, seq_len=seq_len)
    x1, route, counts = _proj_ln_route(y, ssm_w_out[0].astype(BF16), xf, mod[0], ln_g[0, 0], ln_b[0, 0],
                                       w_router, b_router, seq_len=seq_len)
    xf = _moe_layer(x1, route, counts, mod[0], moe_w_gate[0], moe_w_up[0], moe_w_down[0], ln_g[0, 1], ln_b[0, 1],
                    seq_len=seq_len)

    w_qkv = attn_w_qkv[0].astype(BF16)
    qkv = _qkv(xf, mod[1], w_qkv, batch=batch, seq_len=seq_len)
    o = _attention(qkv, batch=batch, seq_len=seq_len).reshape(t, d)
    x1, route, counts = _proj_ln_route(o, attn_w_o[0].astype(BF16), xf, mod[1], ln_g[1, 0], ln_b[1, 0],
                                       w_router, b_router, seq_len=seq_len)
    xf = _moe_layer(x1, route, counts, mod[1], moe_w_gate[1], moe_w_up[1], moe_w_down[1], ln_g[1, 1], ln_b[1, 1],
                    seq_len=seq_len)
    return xf.reshape(batch, seq_len, d)
```

```python
import functools

import jax
import jax.numpy as jnp
from jax import lax
from jax.experimental import pallas as pl
from jax.experimental.pallas import tpu as pltpu

F32 = jnp.float32
BF16 = jnp.bfloat16

V7X_LANES = 128
V7X_SUBLANES = 8
V7X_VMEM_BYTES = 64 * 1024 * 1024

DEPTH = 2
SSM_HEAD_DIM = 64
SSM_STATE = 128
SSM_GROUPS = 8
SSM_HEADS_PER_GROUP = 4
SSM_CONV = 4
SSM_CHUNK = 256
ATTN_HEADS = 16
ATTN_HEAD_DIM = 64
MOBA_BLOCK = 256
MOBA_TOPK = 3
N_EXPERTS = 32
N_EXPERT_GROUPS = 4
EXPERTS_PER_GROUP = N_EXPERTS // N_EXPERT_GROUPS
DEEPNORM_ALPHA = (2.0 * DEPTH) ** 0.25
LN_EPS = 1e-5
RMS_EPS = 1e-5
NEG_INF = -1e30

GROUP_W = SSM_HEADS_PER_GROUP * SSM_HEAD_DIM
CONV_HALO = V7X_SUBLANES
MOE_ROW_TILE = 256
TOKEN_TILE = 256
DMA_UNROLL = 8


def _cparams(semantics, vmem_mib):
    assert vmem_mib * 1024 * 1024 <= V7X_VMEM_BYTES
    return pltpu.CompilerParams(dimension_semantics=semantics, vmem_limit_bytes=vmem_mib * 1024 * 1024)


def _sds(shape, dtype):
    return jax.ShapeDtypeStruct(shape, dtype)


def _dot(a, b):
    return jnp.dot(a, b, preferred_element_type=F32)


def _dot_nt(a, b):
    return lax.dot_general(a, b, (((1,), (1,)), ((), ())), preferred_element_type=F32)


def _dot_tn(a, b):
    return lax.dot_general(a, b, (((0,), (0,)), ((), ())), preferred_element_type=F32)


def _split3(a):
    hi = a.astype(BF16)
    r1 = a - hi.astype(F32)
    mid = r1.astype(BF16)
    lo = (r1 - mid.astype(F32)).astype(BF16)
    return hi, mid, lo


def _silu(x):
    return x * jax.nn.sigmoid(x)


def _layer_norm(v, gamma, beta):
    mu = jnp.mean(v, axis=-1, keepdims=True)
    d = v - mu
    var = jnp.mean(d * d, axis=-1, keepdims=True)
    return d * lax.rsqrt(var + LN_EPS) * gamma + beta


def _ada_kernel(c_ref, w_ref, b_ref, o_ref):
    cs = _silu(c_ref[...])
    o_ref[0] = jnp.dot(cs, w_ref[0], preferred_element_type=F32, precision=lax.Precision.HIGHEST) + b_ref[0]


def _ada_mod(c, w_ada, b_ada):
    depth, d, n = w_ada.shape
    b = c.shape[0]
    tn = 1024
    out = pl.pallas_call(
        _ada_kernel,
        out_shape=_sds((depth, b, n), F32),
        grid=(depth, n // tn),
        in_specs=[pl.BlockSpec((b, d), lambda l, j: (0, 0)),
                  pl.BlockSpec((1, d, tn), lambda l, j: (l, 0, j)),
                  pl.BlockSpec((1, 1, tn), lambda l, j: (l, 0, j))],
        out_specs=pl.BlockSpec((1, b, tn), lambda l, j: (l, 0, j)),
        compiler_params=_cparams(("arbitrary", "arbitrary"), 32),
        name="ada_mod",
    )(c, w_ada, b_ada.reshape(depth, 1, n))
    return out.reshape(depth, b, 6, d)


def _modulate_into(hm_ref, x_ref, mod_ref, sc, sh):
    m = mod_ref[0]
    hm_ref[...] = (x_ref[...] * (1.0 + m[sc:sc + 1, :]) + m[sh:sh + 1, :]).astype(BF16)


def _mm_mod_kernel(x_ref, mod_ref, w_ref, o_ref, hm_ref, *, sc, sh):
    @pl.when(pl.program_id(1) == 0)
    def _():
        _modulate_into(hm_ref, x_ref, mod_ref, sc, sh)

    o_ref[...] = _dot(hm_ref[...], w_ref[...]).astype(o_ref.dtype)


def _mm_mod(x, mod, w, *, sc, sh, seq_len, out_dtype, tm=2048, tn=1536):
    t, k = x.shape
    n = w.shape[1]
    tn = min(tn, n)
    assert seq_len % tm == 0 and n % tn == 0
    tiles_per_batch = seq_len // tm
    return pl.pallas_call(
        functools.partial(_mm_mod_kernel, sc=sc, sh=sh),
        out_shape=_sds((t, n), out_dtype),
        grid=(t // tm, n // tn),
        in_specs=[pl.BlockSpec((tm, k), lambda i, j: (i, 0)),
                  pl.BlockSpec((1, 6, k), lambda i, j: (i // tiles_per_batch, 0, 0)),
                  pl.BlockSpec((k, tn), lambda i, j: (0, j))],
        out_specs=pl.BlockSpec((tm, tn), lambda i, j: (i, j)),
        scratch_shapes=[pltpu.VMEM((tm, k), BF16)],
        compiler_params=_cparams(("arbitrary", "arbitrary"), 48),
        name="mm_mod",
    )(x, mod, w)


def _ssd_kernel(z_ref, xp_ref, bcp_ref, dtr_ref, cw_ref, cb_ref, dtb_ref, alog_ref, dsk_ref, nw_ref,
                o_ref, ubuf, act, state, acum_t):
    lc = SSM_CHUNK
    d_inner = xp_ref.shape[1]
    n_slabs = ubuf.shape[0]
    x_slabs = d_inner // V7X_LANES
    chunk = pl.program_id(1)

    @pl.when(chunk == 0)
    def _():
        ubuf[:, 0:CONV_HALO, :] = jnp.zeros((n_slabs, CONV_HALO, V7X_LANES), F32)
        state[...] = jnp.zeros_like(state)

    def conv_slabs(src_ref, first_slab, count):
        def slab_body(j, carry):
            src_off = pl.multiple_of(j * V7X_LANES, V7X_LANES)
            slab = first_slab + j
            off = pl.multiple_of(slab * V7X_LANES, V7X_LANES)
            u = src_ref[:, pl.ds(src_off, V7X_LANES)].astype(F32)
            ubuf[slab, CONV_HALO:CONV_HALO + lc, :] = u
            acc = cb_ref[:, pl.ds(off, V7X_LANES)] + cw_ref[SSM_CONV - 1:SSM_CONV, pl.ds(off, V7X_LANES)] * u
            for k in range(SSM_CONV - 1):
                tap = CONV_HALO - (SSM_CONV - 1) + k
                acc = acc + cw_ref[k:k + 1, pl.ds(off, V7X_LANES)] * ubuf[slab, tap:tap + lc, :]
            act[:, pl.ds(off, V7X_LANES)] = _silu(acc)
            ubuf[slab, 0:CONV_HALO, :] = u[lc - CONV_HALO:lc, :]
            return carry

        lax.fori_loop(0, count, slab_body, 0)

    conv_slabs(xp_ref, 0, x_slabs)
    conv_slabs(bcp_ref, x_slabs, n_slabs - x_slabs)

    dt = jax.nn.softplus(dtr_ref[...] + dtb_ref[...])
    da = dt * (-jnp.exp(alog_ref[...]))
    row = lax.broadcasted_iota(jnp.int32, (lc, lc), 0)
    col = lax.broadcasted_iota(jnp.int32, (lc, lc), 1)
    causal = col <= row
    tril = causal.astype(BF16)
    acum = sum(_dot(tril, part) for part in _split3(da))
    acum_t[...] = acum.T
    acum_parts = _split3(acum)
    dt_bf16 = dt.astype(BF16)
    head_of_col = lax.broadcasted_iota(jnp.int32, (lc, GROUP_W), 1) // SSM_HEAD_DIM
    sel_row = lax.broadcasted_iota(jnp.int32, (V7X_LANES, GROUP_W), 0)
    sel_col = lax.broadcasted_iota(jnp.int32, (V7X_LANES, GROUP_W), 1) // SSM_HEAD_DIM

    def group_body(g, carry):
        xo = pl.multiple_of(g * GROUP_W, GROUP_W)
        bo = pl.multiple_of(d_inner + g * SSM_STATE, SSM_STATE)
        co = pl.multiple_of(d_inner + SSM_GROUPS * SSM_STATE + g * SSM_STATE, SSM_STATE)
        x_g = act[:, pl.ds(xo, GROUP_W)]
        b_g = act[:, pl.ds(bo, SSM_STATE)].astype(BF16)
        c_g = act[:, pl.ds(co, SSM_STATE)].astype(BF16)
        sel = (sel_row == SSM_HEADS_PER_GROUP * g + sel_col).astype(BF16)
        ab = sum(_dot(part, sel) for part in acum_parts)
        dtb = _dot(dt_bf16, sel)
        alast = ab[lc - 1:lc, :]
        cb = _dot_nt(c_g, b_g)
        xdt = x_g * dtb
        y = jnp.zeros((lc, GROUP_W), F32)
        for r in range(SSM_HEADS_PER_GROUP):
            acol = ab[:, r * SSM_HEAD_DIM:r * SSM_HEAD_DIM + 1]
            arow = acum_t[pl.ds(SSM_HEADS_PER_GROUP * g + r, 1), :]
            decay = jnp.exp(jnp.where(causal, acol - arow, -jnp.inf))
            w = (cb * decay).astype(BF16)
            x_r = jnp.where(head_of_col == r, xdt, 0.0).astype(BF16)
            y = y + _dot(w, x_r)
        st = state[:, pl.ds(xo, GROUP_W)]
        y = y + _dot(c_g, st.astype(BF16)) * jnp.exp(ab)
        to_end = jnp.exp(alast - ab) * dtb
        xw = (x_g * to_end).astype(BF16)
        state[:, pl.ds(xo, GROUP_W)] = st * jnp.exp(alast) + _dot_tn(b_g, xw)
        y = y + dsk_ref[:, pl.ds(xo, GROUP_W)] * x_g
        gated = y * _silu(z_ref[:, pl.ds(xo, GROUP_W)].astype(F32))
        ms = jnp.mean(gated * gated, axis=-1, keepdims=True)
        o_ref[:, pl.ds(xo, GROUP_W)] = (gated * lax.rsqrt(ms + RMS_EPS) * nw_ref[:, pl.ds(xo, GROUP_W)]).astype(BF16)
        return carry

    lax.fori_loop(0, SSM_GROUPS, group_body, 0, unroll=4)


def _ssd(proj, dt_raw, conv_w, conv_b, dt_bias, a_log, d_skip, norm_w, *, batch, seq_len):
    t = proj.shape[0]
    d_inner = SSM_GROUPS * GROUP_W
    conv_dim = d_inner + 2 * SSM_GROUPS * SSM_STATE
    assert proj.shape[1] == d_inner + conv_dim and conv_dim == 2 * d_inner
    nc = seq_len // SSM_CHUNK
    lc = SSM_CHUNK
    pad = V7X_LANES - dt_bias.shape[0]
    heads = dt_bias.shape[0]
    row_map = lambda b, c: b * nc + c
    const = lambda b, c: (0, 0)
    return pl.pallas_call(
        _ssd_kernel,
        out_shape=_sds((t, d_inner), BF16),
        grid=(batch, nc),
        in_specs=[pl.BlockSpec((lc, d_inner), lambda b, c: (row_map(b, c), 0)),
                  pl.BlockSpec((lc, d_inner), lambda b, c: (row_map(b, c), 1)),
                  pl.BlockSpec((lc, d_inner), lambda b, c: (row_map(b, c), 2)),
                  pl.BlockSpec((lc, V7X_LANES), lambda b, c: (row_map(b, c), 0)),
                  pl.BlockSpec((SSM_CONV, conv_dim), const),
                  pl.BlockSpec((1, conv_dim), const),
                  pl.BlockSpec((1, V7X_LANES), const),
                  pl.BlockSpec((1, V7X_LANES), const),
                  pl.BlockSpec((1, d_inner), const),
                  pl.BlockSpec((1, d_inner), const)],
        out_specs=pl.BlockSpec((lc, d_inner), lambda b, c: (row_map(b, c), 0)),
        scratch_shapes=[pltpu.VMEM((conv_dim // V7X_LANES, CONV_HALO + lc, V7X_LANES), F32),
                        pltpu.VMEM((lc, conv_dim), F32),
                        pltpu.VMEM((SSM_STATE, d_inner), F32),
                        pltpu.VMEM((V7X_LANES, lc), F32)],
        compiler_params=_cparams(("arbitrary", "arbitrary"), 48),
        name="ssd_chunk_scan",
    )(proj, proj, proj, dt_raw, conv_w, conv_b.reshape(1, conv_dim),
      jnp.pad(dt_bias, (0, pad)).reshape(1, V7X_LANES), jnp.pad(a_log, (0, pad)).reshape(1, V7X_LANES),
      jnp.repeat(d_skip, SSM_HEAD_DIM).reshape(1, heads * SSM_HEAD_DIM), norm_w.reshape(1, d_inner))


def _route(logits, carry_ref):
    tm = logits.shape[0]
    lane = lax.broadcasted_iota(jnp.int32, (tm, V7X_LANES), 1)
    lane_f = lane.astype(F32)
    lg = jnp.where(lane < N_EXPERTS, logits, -jnp.inf)
    e = jnp.exp(lg - jnp.max(lg, axis=-1, keepdims=True))
    far = float(V7X_LANES)
    best = None
    for g in range(N_EXPERT_GROUPS):
        in_g = (lane >= g * EXPERTS_PER_GROUP) & (lane < (g + 1) * EXPERTS_PER_GROUP)
        eg = jnp.where(in_g, e, -1.0)
        m1 = jnp.max(eg, axis=-1, keepdims=True)
        i1 = jnp.min(jnp.where(eg == m1, lane_f, far), axis=-1, keepdims=True)
        eg2 = jnp.where(lane_f == i1, -1.0, eg)
        m2 = jnp.max(eg2, axis=-1, keepdims=True)
        i2 = jnp.min(jnp.where(eg2 == m2, lane_f, far), axis=-1, keepdims=True)
        cand = (m1 + m2, m1, m2, i1, i2)
        if best is None:
            best = cand
        else:
            take = cand[0] > best[0]
            best = tuple(jnp.where(take, c, b) for c, b in zip(cand, best))
    _, m1, m2, i1, i2 = best
    denom = m1 + m2
    hit1 = lane_f == i1
    hit2 = lane_f == i2
    onehot = (hit1 | hit2).astype(BF16)
    row = lax.broadcasted_iota(jnp.int32, (tm, tm), 0)
    col = lax.broadcasted_iota(jnp.int32, (tm, tm), 1)
    before = (col < row).astype(BF16)
    rank = _dot(before, onehot) + carry_ref[0:1, :]
    r1 = jnp.sum(jnp.where(hit1, rank, 0.0), axis=-1, keepdims=True)
    r2 = jnp.sum(jnp.where(hit2, rank, 0.0), axis=-1, keepdims=True)
    carry_ref[0:1, :] = carry_ref[0:1, :] + jnp.sum(onehot.astype(F32), axis=0, keepdims=True)
    rec = jnp.zeros((tm, V7X_LANES), F32)
    for k, val in enumerate((i1, i2, m1 / denom, m2 / denom, r1, r2)):
        rec = jnp.where(lane == k, val, rec)
    return rec


def _proj_ln_route_kernel(a_ref, w_ref, xres_ref, mod_ref, lng_ref, lnb_ref, wr_ref, br_ref,
                          x_ref, route_ref, cnt_ref, carry_ref, *, gate_idx, sc, sh):
    @pl.when(pl.program_id(0) == 0)
    def _():
        carry_ref[...] = jnp.zeros_like(carry_ref)

    m = mod_ref[0]
    y = _dot(a_ref[...], w_ref[...])
    v = DEEPNORM_ALPHA * xres_ref[...] + (1.0 + m[gate_idx:gate_idx + 1, :]) * y
    x1 = _layer_norm(v, lng_ref[...], lnb_ref[...])
    x_ref[...] = x1
    hm = (x1 * (1.0 + m[sc:sc + 1, :]) + m[sh:sh + 1, :]).astype(BF16)
    logits = _dot(hm, wr_ref[...]) + br_ref[...]
    route_ref[...] = _route(logits, carry_ref)
    cnt_ref[...] = carry_ref[...]


def _proj_ln_route(a, w, xres, mod, ln_g, ln_b, w_router, b_router, *, seq_len):
    t, k = a.shape
    d = w.shape[1]
    tm = TOKEN_TILE
    tiles_per_batch = seq_len // tm
    const = lambda i: (0, 0)
    wr = jnp.pad(w_router, ((0, 0), (0, V7X_LANES - N_EXPERTS))).astype(BF16)
    br = jnp.pad(b_router, (0, V7X_LANES - N_EXPERTS)).reshape(1, V7X_LANES)
    return pl.pallas_call(
        functools.partial(_proj_ln_route_kernel, gate_idx=2, sc=4, sh=3),
        out_shape=(_sds((t, d), F32), _sds((t, V7X_LANES), F32), _sds((V7X_SUBLANES, V7X_LANES), F32)),
        grid=(t // tm,),
        in_specs=[pl.BlockSpec((tm, k), lambda i: (i, 0)),
                  pl.BlockSpec((k, d), const),
                  pl.BlockSpec((tm, d), lambda i: (i, 0)),
                  pl.BlockSpec((1, 6, d), lambda i: (i // tiles_per_batch, 0, 0)),
                  pl.BlockSpec((1, d), const),
                  pl.BlockSpec((1, d), const),
                  pl.BlockSpec((d, V7X_LANES), const),
                  pl.BlockSpec((1, V7X_LANES), const)],
        out_specs=(pl.BlockSpec((tm, d), lambda i: (i, 0)),
                   pl.BlockSpec((tm, V7X_LANES), lambda i: (i, 0)),
                   pl.BlockSpec((V7X_SUBLANES, V7X_LANES), const)),
        scratch_shapes=[pltpu.VMEM((V7X_SUBLANES, V7X_LANES), F32)],
        compiler_params=_cparams(("arbitrary",), 40),
        name="proj_ln_route",
    )(a, w, xres, mod, ln_g.reshape(1, d), ln_b.reshape(1, d), wr, br)


def _record_copy(src_ref, src_slot, dst_ref, dst_slot, sem, n):
    src = src_ref.at[pl.ds(pl.multiple_of(src_slot * n, n), n)]
    dst = dst_ref.at[pl.ds(pl.multiple_of(dst_slot * n, n), n)]
    return pltpu.make_async_copy(src, dst, sem)


def _to_records(rec_ref, rows):
    t = rows.shape[0]
    n = rows.shape[1] // V7X_LANES
    for s in range(n):
        rec_ref[pl.ds(s, t, stride=n), :] = rows[:, s * V7X_LANES:(s + 1) * V7X_LANES]


def _from_records(rec_ref, t, n):
    return jnp.concatenate([rec_ref[pl.ds(s, t, stride=n), :] for s in range(n)], axis=1)


def _dispatch_kernel(p1_ref, p2_ref, x_ref, mod_ref, xs_init_ref, xs_ref, hm_ref, sem, *, sc, sh):
    del xs_init_ref
    tt, d = x_ref.shape
    n = d // V7X_LANES
    m = mod_ref[0]
    _to_records(hm_ref, x_ref[...] * (1.0 + m[sc:sc + 1, :]) + m[sh:sh + 1, :])

    def issue(blk, carry):
        for u in range(DMA_UNROLL):
            r = blk * DMA_UNROLL + u
            _record_copy(hm_ref, r, xs_ref, p1_ref[0, 0, r], sem, n).start(priority=0)
            _record_copy(hm_ref, r, xs_ref, p2_ref[0, 0, r], sem, n).start(priority=1)
        return carry

    def drain(blk, carry):
        for _ in range(2 * DMA_UNROLL):
            _record_copy(hm_ref, 0, xs_ref, 0, sem, n).wait()
        return carry

    lax.fori_loop(0, tt // DMA_UNROLL, issue, 0)
    lax.fori_loop(0, tt // DMA_UNROLL, drain, 0)


def _slot_spec(tt):
    return pl.BlockSpec((1, 1, tt), lambda i: (i, 0, 0), memory_space=pltpu.SMEM)


def _dispatch(x, mod, pos1, pos2, n_slots, *, seq_len):
    t, d = x.shape
    n = d // V7X_LANES
    tt = TOKEN_TILE
    tiles_per_batch = seq_len // tt
    return pl.pallas_call(
        functools.partial(_dispatch_kernel, sc=4, sh=3),
        out_shape=_sds((n_slots * n, V7X_LANES), F32),
        grid=(t // tt,),
        in_specs=[_slot_spec(tt), _slot_spec(tt),
                  pl.BlockSpec((tt, d), lambda i: (i, 0)),
                  pl.BlockSpec((1, 6, d), lambda i: (i // tiles_per_batch, 0, 0)),
                  pl.BlockSpec(memory_space=pl.ANY)],
        out_specs=pl.BlockSpec(memory_space=pl.ANY),
        scratch_shapes=[pltpu.VMEM((tt * n, V7X_LANES), F32), pltpu.SemaphoreType.DMA(())],
        input_output_aliases={4: 0},
        compiler_params=_cparams(("arbitrary",), 32),
        name="moe_dispatch",
    )(pos1.reshape(t // tt, 1, tt), pos2.reshape(t // tt, 1, tt), x, mod, jnp.zeros((n_slots * n, V7X_LANES), F32))


def _ffn_kernel(te_ref, nu_ref, x_ref, wg_ref, wu_ref, wd_ref, o_ref, wg_bf, wu_bf, wd_bf):
    i = pl.program_id(0)
    d, f = wg_bf.shape
    n = d // V7X_LANES
    tm = x_ref.shape[0] // n
    used = i < nu_ref[0]
    new_expert = jnp.logical_or(i == 0, te_ref[i] != te_ref[jnp.maximum(i - 1, 0)])

    @pl.when(jnp.logical_and(used, new_expert))
    def _():
        wg_bf[...] = wg_ref[0, 0].astype(BF16)
        wu_bf[...] = wu_ref[0, 0].astype(BF16)
        wd_bf[...] = wd_ref[0, 0].astype(BF16)

    @pl.when(used)
    def _():
        x = _from_records(x_ref, tm, n).astype(BF16)
        hg = _dot(x, wg_bf[...])
        hu = _dot(x, wu_bf[...])
        _to_records(o_ref, _dot((_silu(hg) * hu).astype(BF16), wd_bf[...]))

    @pl.when(jnp.logical_not(used))
    def _():
        o_ref[...] = jnp.zeros_like(o_ref)


def _ffn(xs, tile_expert, n_used, w_gate, w_up, w_down, *, layer):
    _, _, d, f = w_gate.shape
    n = d // V7X_LANES
    tm = MOE_ROW_TILE
    w_map = lambda i, te, nu: (layer, te[i], 0, 0)
    grid_spec = pltpu.PrefetchScalarGridSpec(
        num_scalar_prefetch=2,
        grid=(xs.shape[0] // (tm * n),),
        in_specs=[pl.BlockSpec((tm * n, V7X_LANES), lambda i, te, nu: (i, 0)),
                  pl.BlockSpec((1, 1, d, f), w_map),
                  pl.BlockSpec((1, 1, d, f), w_map),
                  pl.BlockSpec((1, 1, f, d), w_map)],
        out_specs=pl.BlockSpec((tm * n, V7X_LANES), lambda i, te, nu: (i, 0)),
        scratch_shapes=[pltpu.VMEM((d, f), BF16), pltpu.VMEM((d, f), BF16), pltpu.VMEM((f, d), BF16)],
    )
    return pl.pallas_call(
        _ffn_kernel,
        out_shape=_sds(xs.shape, F32),
        grid_spec=grid_spec,
        compiler_params=_cparams(("arbitrary",), 48),
        name="moe_ffn",
    )(tile_expert, n_used, xs, w_gate, w_up, w_down)


def _combine_ln_kernel(p1_ref, p2_ref, xres_ref, route_ref, mod_ref, lng_ref, lnb_ref, ye_ref, o_ref, buf, sem,
                       *, gate_idx):
    tt, d = xres_ref.shape
    n = d // V7X_LANES

    def issue(blk, carry):
        for u in range(DMA_UNROLL):
            r = blk * DMA_UNROLL + u
            _record_copy(ye_ref, p1_ref[0, 0, r], buf.at[0], r, sem, n).start(priority=0)
            _record_copy(ye_ref, p2_ref[0, 0, r], buf.at[1], r, sem, n).start(priority=1)
        return carry

    def drain(blk, carry):
        for _ in range(2 * DMA_UNROLL):
            _record_copy(ye_ref, 0, buf.at[0], 0, sem, n).wait()
        return carry

    lax.fori_loop(0, tt // DMA_UNROLL, issue, 0)
    lax.fori_loop(0, tt // DMA_UNROLL, drain, 0)
    m = mod_ref[0]
    rec = route_ref[...]
    y = rec[:, 2:3] * _from_records(buf.at[0], tt, n) + rec[:, 3:4] * _from_records(buf.at[1], tt, n)
    v = DEEPNORM_ALPHA * xres_ref[...] + (1.0 + m[gate_idx:gate_idx + 1, :]) * y
    o_ref[...] = _layer_norm(v, lng_ref[...], lnb_ref[...])


def _combine_ln(xres, route, pos1, pos2, ye, mod, ln_g, ln_b, *, seq_len):
    t, d = xres.shape
    n = d // V7X_LANES
    tt = TOKEN_TILE
    tiles_per_batch = seq_len // tt
    const = lambda i: (0, 0)
    return pl.pallas_call(
        functools.partial(_combine_ln_kernel, gate_idx=5),
        out_shape=_sds((t, d), F32),
        grid=(t // tt,),
        in_specs=[_slot_spec(tt), _slot_spec(tt),
                  pl.BlockSpec((tt, d), lambda i: (i, 0)),
                  pl.BlockSpec((tt, V7X_LANES), lambda i: (i, 0)),
                  pl.BlockSpec((1, 6, d), lambda i: (i // tiles_per_batch, 0, 0)),
                  pl.BlockSpec((1, d), const),
                  pl.BlockSpec((1, d), const),
                  pl.BlockSpec(memory_space=pl.ANY)],
        out_specs=pl.BlockSpec((tt, d), lambda i: (i, 0)),
        scratch_shapes=[pltpu.VMEM((2, tt * n, V7X_LANES), F32), pltpu.SemaphoreType.DMA(())],
        compiler_params=_cparams(("arbitrary",), 32),
        name="moe_combine_ln",
    )(pos1.reshape(t // tt, 1, tt), pos2.reshape(t // tt, 1, tt), xres, route, mod,
      ln_g.reshape(1, d), ln_b.reshape(1, d), ye)


def _moe_tables(route, counts):
    t = route.shape[0]
    tm = MOE_ROW_TILE
    max_tiles = (2 * t) // tm + N_EXPERTS
    cnt = counts[0, :N_EXPERTS].astype(jnp.int32)
    tiles_e = (cnt + tm - 1) // tm
    tile_end = jnp.cumsum(tiles_e)
    row_off = (tile_end - tiles_e) * tm
    pos1 = row_off[route[:, 0].astype(jnp.int32)] + route[:, 4].astype(jnp.int32)
    pos2 = row_off[route[:, 1].astype(jnp.int32)] + route[:, 5].astype(jnp.int32)
    n_used = tile_end[-1:]
    tile_ids = jnp.minimum(jnp.arange(max_tiles, dtype=jnp.int32), n_used - 1)
    tile_expert = jnp.sum(tile_ids[:, None] >= tile_end[None, :], axis=1).astype(jnp.int32)
    return pos1, pos2, tile_expert, n_used.astype(jnp.int32), max_tiles * tm


def _moe_layer(x1, route, counts, mod, w_gate, w_up, w_down, ln_g, ln_b, *, layer, seq_len):
    pos1, pos2, tile_expert, n_used, n_slots = _moe_tables(route, counts)
    xs = _dispatch(x1, mod, pos1, pos2, n_slots, seq_len=seq_len)
    ye = _ffn(xs, tile_expert, n_used, w_gate, w_up, w_down, layer=layer)
    return _combine_ln(x1, route, pos1, pos2, ye, mod, ln_g, ln_b, seq_len=seq_len)


def _qkv_body(x_ref, mod_ref, w_ref, o_ref, hm_ref, *, sc, sh, tiles_per_batch, tiles_per_kind):
    @pl.when(pl.program_id(1) == 0)
    def _():
        _modulate_into(hm_ref, x_ref, mod_ref, sc, sh)

    tm = x_ref.shape[0]
    kind = pl.program_id(1) // tiles_per_kind
    acc = _dot(hm_ref[...], w_ref[...])
    acc = acc * jnp.where(kind == 0, ATTN_HEAD_DIM ** -0.5, 1.0)
    lane = lax.broadcasted_iota(jnp.int32, (tm, V7X_LANES), 1)
    low = lane < ATTN_HEAD_DIM
    row = lax.broadcasted_iota(jnp.int32, (tm, V7X_LANES), 0)
    pos = lax.rem(pl.program_id(0), tiles_per_batch) * tm + row
    k_extra = (lane == ATTN_HEAD_DIM + lax.div(pos, MOBA_BLOCK)).astype(F32)
    v_extra = (lane == ATTN_HEAD_DIM).astype(F32)
    extra = jnp.where(kind == 1, k_extra, jnp.where(kind == 2, v_extra, 0.0))
    for p in range(acc.shape[1] // V7X_LANES):
        t2 = acc[:, p * V7X_LANES:(p + 1) * V7X_LANES]
        even = jnp.where(low, t2, extra)
        odd = jnp.where(low, pltpu.roll(t2, ATTN_HEAD_DIM, axis=1), extra)
        o_ref[0, 0, 2 * p] = even.astype(BF16)
        o_ref[0, 0, 2 * p + 1] = odd.astype(BF16)


def _qkv(x, mod, w, *, batch, seq_len, tm=1024, tn=512):
    t, k = x.shape
    n = w.shape[1]
    heads_per_tile = tn // ATTN_HEAD_DIM
    tiles_per_batch = seq_len // tm
    tiles_per_kind = ATTN_HEADS // heads_per_tile
    assert n == 3 * ATTN_HEADS * ATTN_HEAD_DIM and seq_len % tm == 0
    return pl.pallas_call(
        functools.partial(_qkv_body, sc=1, sh=0, tiles_per_batch=tiles_per_batch, tiles_per_kind=tiles_per_kind),
        out_shape=_sds((3, batch, ATTN_HEADS, seq_len, V7X_LANES), BF16),
        grid=(t // tm, n // tn),
        in_specs=[pl.BlockSpec((tm, k), lambda i, j: (i, 0)),
                  pl.BlockSpec((1, 6, k), lambda i, j: (i // tiles_per_batch, 0, 0)),
                  pl.BlockSpec((k, tn), lambda i, j: (0, j))],
        out_specs=pl.BlockSpec((1, 1, heads_per_tile, tm, V7X_LANES),
                               lambda i, j: (j // tiles_per_kind, i // tiles_per_batch, j % tiles_per_kind,
                                             i % tiles_per_batch, 0)),
        scratch_shapes=[pltpu.VMEM((tm, k), BF16)],
        compiler_params=_cparams(("arbitrary", "arbitrary"), 40),
        name="qkv_proj",
    )(x, mod, w)


ATTN_HEADS_PER_STEP = 2
ATTN_Q_TILE = 1024
GATE_ROWS = 512


def _attn_kernel(q_ref, k_ref, v_ref, pm_ref, o_ref, km_ref, qa_ref):
    tq = ATTN_Q_TILE
    seq_len = q_ref.shape[2]
    n_blocks = seq_len // MOBA_BLOCK
    heads = range(ATTN_HEADS_PER_STEP)
    qi = pl.program_id(2)

    @pl.when(qi == 0)
    def _():
        for hh in heads:
            km_ref[hh] = _dot(pm_ref[...], k_ref[0, hh]).astype(BF16)
        lane = lax.broadcasted_iota(jnp.int32, (GATE_ROWS, V7X_LANES), 1)
        prefix = (lax.broadcasted_iota(jnp.int32, (V7X_LANES, V7X_LANES), 0)
                  <= lax.broadcasted_iota(jnp.int32, (V7X_LANES, V7X_LANES), 1)).astype(BF16)
        blk = lane - ATTN_HEAD_DIM
        mask_lanes = (blk >= 0) & (blk < n_blocks)
        row = lax.broadcasted_iota(jnp.int32, (GATE_ROWS, V7X_LANES), 0)

        def gate_rows(ci, carry):
            r0 = pl.multiple_of(ci * GATE_ROWS, GATE_ROWS)
            past = (blk >= 0) & (blk < lax.div(r0 + row, MOBA_BLOCK))
            for hh in heads:
                q = q_ref[0, hh, pl.ds(r0, GATE_ROWS), :]
                gate = jnp.where(past, _dot_nt(q, km_ref[hh]), -jnp.inf)
                chosen = jnp.zeros((GATE_ROWS, V7X_LANES), jnp.bool_)
                for _ in range(MOBA_TOPK):
                    at_max = gate == jnp.max(gate, axis=-1, keepdims=True)
                    pick = at_max & (_dot(at_max.astype(BF16), prefix) == 1.0)
                    chosen = chosen | pick
                    gate = jnp.where(pick, -jnp.inf, gate)
                own = blk == lax.div(r0 + row, MOBA_BLOCK)
                blocked = mask_lanes & jnp.logical_not((chosen & past) | own)
                qa_ref[hh, pl.ds(r0, GATE_ROWS), :] = (q.astype(F32) + jnp.where(blocked, NEG_INF, 0.0)).astype(BF16)
            return carry

        lax.fori_loop(0, seq_len // GATE_ROWS, gate_rows, 0)

    q0 = pl.multiple_of(qi * tq, tq)

    def online_update(m_i, acc, s, v):
        m_new = jnp.maximum(m_i, jnp.max(s, axis=-1, keepdims=True))
        p = jnp.exp(s - m_new)
        return m_new, acc * jnp.exp(m_i - m_new) + _dot(p.astype(BF16), v)

    def kv_step(j, carry):
        start = pl.multiple_of(j * tq, tq)
        out = []
        for hh in heads:
            m_i, acc = carry[hh]
            s = _dot_nt(qa_ref[hh, pl.ds(q0, tq), :], k_ref[0, hh, pl.ds(start, tq), :])
            out.append(online_update(m_i, acc, s, v_ref[0, hh, pl.ds(start, tq), :]))
        return tuple(out)

    init = tuple((jnp.full((tq, 1), NEG_INF, F32), jnp.zeros((tq, V7X_LANES), F32)) for _ in heads)
    carry = lax.fori_loop(0, qi, kv_step, init)

    row = lax.broadcasted_iota(jnp.int32, (tq, tq), 0)
    col = lax.broadcasted_iota(jnp.int32, (tq, tq), 1)
    outs = []
    for hh in heads:
        m_i, acc = carry[hh]
        s = _dot_nt(qa_ref[hh, pl.ds(q0, tq), :], k_ref[0, hh, pl.ds(q0, tq), :])
        _, acc = online_update(m_i, acc, jnp.where(col <= row, s, NEG_INF), v_ref[0, hh, pl.ds(q0, tq), :])
        outs.append(acc / acc[:, ATTN_HEAD_DIM:ATTN_HEAD_DIM + 1])
    lane = lax.broadcasted_iota(jnp.int32, (tq, V7X_LANES), 1)
    for pair in range(ATTN_HEADS_PER_STEP // 2):
        both = jnp.where(lane < ATTN_HEAD_DIM, outs[2 * pair], pltpu.roll(outs[2 * pair + 1], ATTN_HEAD_DIM, axis=1))
        o_ref[0, :, pair * V7X_LANES:(pair + 1) * V7X_LANES] = both.astype(BF16)


def _attention(qkv, *, batch, seq_len):
    nb = seq_len // MOBA_BLOCK
    tq = ATTN_Q_TILE
    hps = ATTN_HEADS_PER_STEP
    assert hps % 2 == 0 and 2 * ATTN_HEAD_DIM == V7X_LANES and nb <= V7X_LANES - ATTN_HEAD_DIM
    assert seq_len % GATE_ROWS == 0 and seq_len % tq == 0 and tq % MOBA_BLOCK == 0
    rows = jnp.arange(V7X_LANES, dtype=jnp.int32)[:, None] - ATTN_HEAD_DIM
    cols = jnp.arange(seq_len, dtype=jnp.int32)[None, :] // MOBA_BLOCK
    pool = jnp.where(rows == cols, 1.0 / MOBA_BLOCK, 0.0).astype(BF16)
    def head_spec(kind):
        return pl.BlockSpec((None, 1, hps, seq_len, V7X_LANES), lambda b, h, i: (kind, b, h, 0, 0))

    return pl.pallas_call(
        _attn_kernel,
        out_shape=_sds((batch, seq_len, ATTN_HEADS * ATTN_HEAD_DIM), BF16),
        grid=(batch, ATTN_HEADS // hps, seq_len // tq),
        in_specs=[head_spec(0), head_spec(1), head_spec(2),
                  pl.BlockSpec((V7X_LANES, seq_len), lambda b, h, i: (0, 0))],
        out_specs=pl.BlockSpec((1, tq, hps * ATTN_HEAD_DIM), lambda b, h, i: (b, i, h)),
        scratch_shapes=[pltpu.VMEM((hps, V7X_LANES, V7X_LANES), BF16),
                        pltpu.VMEM((hps, seq_len, V7X_LANES), BF16)],
        compiler_params=_cparams(("arbitrary", "arbitrary", "arbitrary"), 48),
        name="moba_attention",
    )(qkv, qkv, qkv, pool)


def kernel(x, c, w_ada, b_ada, ln_g, ln_b, ssm_w_in, ssm_conv_w, ssm_conv_b, ssm_dt_bias, ssm_a_log, ssm_d,
           ssm_norm_w, ssm_w_out, attn_w_qkv, attn_w_o, w_router, b_router, moe_w_gate, moe_w_up, moe_w_down):
    batch, seq_len, d = x.shape
    assert seq_len % 512 == 0 and seq_len % SSM_CHUNK == 0 and seq_len % MOBA_BLOCK == 0
    t = batch * seq_len
    mod = _ada_mod(c, w_ada, b_ada)
    xf = x.reshape(t, d)

    d_inner = SSM_GROUPS * GROUP_W
    conv_dim = d_inner + 2 * SSM_GROUPS * SSM_STATE
    w_in = ssm_w_in[0]
    heads = ssm_dt_bias.shape[1]
    w_main = w_in[:, :d_inner + conv_dim].astype(BF16)
    w_dt = jnp.pad(w_in[:, d_inner + conv_dim:], ((0, 0), (0, V7X_LANES - heads))).astype(BF16)
    proj = _mm_mod(xf, mod[0], w_main, sc=1, sh=0, seq_len=seq_len, out_dtype=BF16)
    dt_raw = _mm_mod(xf, mod[0], w_dt, sc=1, sh=0, seq_len=seq_len, out_dtype=F32)
    y = _ssd(proj, dt_raw, ssm_conv_w[0], ssm_conv_b[0], ssm_dt_bias[0], ssm_a_log[0], ssm_d[0], ssm_norm_w[0],
             batch=batch, seq_len=seq_len)
    x1, route, counts = _proj_ln_route(y, ssm_w_out[0].astype(BF16), xf, mod[0], ln_g[0, 0], ln_b[0, 0],
                                       w_router, b_router, seq_len=seq_len)
    xf = _moe_layer(x1, route, counts, mod[0], moe_w_gate, moe_w_up, moe_w_down, ln_g[0, 1], ln_b[0, 1],
                    layer=0, seq_len=seq_len)

    w_qkv = attn_w_qkv[0].astype(BF16)
    qkv = _qkv(xf, mod[1], w_qkv, batch=batch, seq_len=seq_len)
    o = _attention(qkv, batch=batch, seq_len=seq_len).reshape(t, d)
    x1, route, counts = _proj_ln_route(o, attn_w_o[0].astype(BF16), xf, mod[1], ln_g[1, 0], ln_b[1, 0],
                                       w_router, b_router, seq_len=seq_len)
    xf = _moe_layer(x1, route, counts, mod[1], moe_w_gate, moe_w_up, moe_w_down, ln_g[1, 1], ln_b[1, 1],
                    layer=1, seq_len=seq_len)
    return xf.reshape(batch, seq_len, d)
```

```python
import functools

import jax
import jax.numpy as jnp
from jax import lax
from jax.experimental import pallas as pl
from jax.experimental.pallas import tpu as pltpu

F32 = jnp.float32
BF16 = jnp.bfloat16

V7X_LANES = 128
V7X_SUBLANES = 8
V7X_VMEM_BYTES = 64 * 1024 * 1024

DEPTH = 2
SSM_HEAD_DIM = 64
SSM_STATE = 128
SSM_GROUPS = 8
SSM_HEADS_PER_GROUP = 4
SSM_CONV = 4
SSM_CHUNK = 256
ATTN_HEADS = 16
ATTN_HEAD_DIM = 64
MOBA_BLOCK = 256
MOBA_TOPK = 3
N_EXPERTS = 32
N_EXPERT_GROUPS = 4
EXPERTS_PER_GROUP = N_EXPERTS // N_EXPERT_GROUPS
DEEPNORM_ALPHA = (2.0 * DEPTH) ** 0.25
LN_EPS = 1e-5
RMS_EPS = 1e-5
NEG_INF = -1e30

GROUP_W = SSM_HEADS_PER_GROUP * SSM_HEAD_DIM
CONV_HALO = V7X_SUBLANES
MOE_ROW_TILE = 256
TOKEN_TILE = 256
DMA_UNROLL = 8


def _cparams(semantics, vmem_mib):
    assert vmem_mib * 1024 * 1024 <= V7X_VMEM_BYTES
    return pltpu.CompilerParams(dimension_semantics=semantics, vmem_limit_bytes=vmem_mib * 1024 * 1024)


def _sds(shape, dtype):
    return jax.ShapeDtypeStruct(shape, dtype)


def _dot(a, b):
    return jnp.dot(a, b, preferred_element_type=F32)


def _dot_nt(a, b):
    return lax.dot_general(a, b, (((1,), (1,)), ((), ())), preferred_element_type=F32)


def _dot_tn(a, b):
    return lax.dot_general(a, b, (((0,), (0,)), ((), ())), preferred_element_type=F32)


def _split3(a):
    hi = a.astype(BF16)
    r1 = a - hi.astype(F32)
    mid = r1.astype(BF16)
    lo = (r1 - mid.astype(F32)).astype(BF16)
    return hi, mid, lo


def _silu(x):
    return x * jax.nn.sigmoid(x)


def _layer_norm(v, gamma, beta):
    mu = jnp.mean(v, axis=-1, keepdims=True)
    d = v - mu
    var = jnp.mean(d * d, axis=-1, keepdims=True)
    return d * lax.rsqrt(var + LN_EPS) * gamma + beta


def _ada_kernel(c_ref, w_ref, b_ref, o_ref):
    cs = _silu(c_ref[...])
    o_ref[0] = jnp.dot(cs, w_ref[0], preferred_element_type=F32, precision=lax.Precision.HIGHEST) + b_ref[0]


def _ada_mod(c, w_ada, b_ada):
    depth, d, n = w_ada.shape
    b = c.shape[0]
    tn = 1024
    out = pl.pallas_call(
        _ada_kernel,
        out_shape=_sds((depth, b, n), F32),
        grid=(depth, n // tn),
        in_specs=[pl.BlockSpec((b, d), lambda l, j: (0, 0)),
                  pl.BlockSpec((1, d, tn), lambda l, j: (l, 0, j)),
                  pl.BlockSpec((1, 1, tn), lambda l, j: (l, 0, j))],
        out_specs=pl.BlockSpec((1, b, tn), lambda l, j: (l, 0, j)),
        compiler_params=_cparams(("arbitrary", "arbitrary"), 32),
        name="ada_mod",
    )(c, w_ada, b_ada.reshape(depth, 1, n))
    return out.reshape(depth, b, 6, d)


def _modulate_into(hm_ref, x_ref, mod_ref, sc, sh):
    m = mod_ref[0]
    hm_ref[...] = (x_ref[...] * (1.0 + m[sc:sc + 1, :]) + m[sh:sh + 1, :]).astype(BF16)


def _mm_mod_kernel(x_ref, mod_ref, w_ref, o_ref, hm_ref, *, sc, sh):
    @pl.when(pl.program_id(1) == 0)
    def _():
        _modulate_into(hm_ref, x_ref, mod_ref, sc, sh)

    o_ref[...] = _dot(hm_ref[...], w_ref[...]).astype(o_ref.dtype)


def _mm_mod(x, mod, w, *, sc, sh, seq_len, out_dtype, tm=2048, tn=1536):
    t, k = x.shape
    n = w.shape[1]
    tn = min(tn, n)
    assert seq_len % tm == 0 and n % tn == 0
    tiles_per_batch = seq_len // tm
    return pl.pallas_call(
        functools.partial(_mm_mod_kernel, sc=sc, sh=sh),
        out_shape=_sds((t, n), out_dtype),
        grid=(t // tm, n // tn),
        in_specs=[pl.BlockSpec((tm, k), lambda i, j: (i, 0)),
                  pl.BlockSpec((1, 6, k), lambda i, j: (i // tiles_per_batch, 0, 0)),
                  pl.BlockSpec((k, tn), lambda i, j: (0, j))],
        out_specs=pl.BlockSpec((tm, tn), lambda i, j: (i, j)),
        scratch_shapes=[pltpu.VMEM((tm, k), BF16)],
        compiler_params=_cparams(("arbitrary", "arbitrary"), 48),
        name="mm_mod",
    )(x, mod, w)


def _ssd_kernel(z_ref, xp_ref, bcp_ref, dtr_ref, cw_ref, cb_ref, dtb_ref, alog_ref, dsk_ref, nw_ref,
                o_ref, ubuf, act, state, acum_t):
    lc = SSM_CHUNK
    d_inner = xp_ref.shape[1]
    n_slabs = ubuf.shape[0]
    x_slabs = d_inner // V7X_LANES
    chunk = pl.program_id(1)

    @pl.when(chunk == 0)
    def _():
        ubuf[:, 0:CONV_HALO, :] = jnp.zeros((n_slabs, CONV_HALO, V7X_LANES), F32)
        state[...] = jnp.zeros_like(state)

    def conv_slabs(src_ref, first_slab, count):
        def slab_body(j, carry):
            src_off = pl.multiple_of(j * V7X_LANES, V7X_LANES)
            slab = first_slab + j
            off = pl.multiple_of(slab * V7X_LANES, V7X_LANES)
            u = src_ref[:, pl.ds(src_off, V7X_LANES)].astype(F32)
            ubuf[slab, CONV_HALO:CONV_HALO + lc, :] = u
            acc = cb_ref[:, pl.ds(off, V7X_LANES)] + cw_ref[SSM_CONV - 1:SSM_CONV, pl.ds(off, V7X_LANES)] * u
            for k in range(SSM_CONV - 1):
                tap = CONV_HALO - (SSM_CONV - 1) + k
                acc = acc + cw_ref[k:k + 1, pl.ds(off, V7X_LANES)] * ubuf[slab, tap:tap + lc, :]
            act[:, pl.ds(off, V7X_LANES)] = _silu(acc)
            ubuf[slab, 0:CONV_HALO, :] = u[lc - CONV_HALO:lc, :]
            return carry

        lax.fori_loop(0, count, slab_body, 0)

    conv_slabs(xp_ref, 0, x_slabs)
    conv_slabs(bcp_ref, x_slabs, n_slabs - x_slabs)

    dt = jax.nn.softplus(dtr_ref[...] + dtb_ref[...])
    da = dt * (-jnp.exp(alog_ref[...]))
    row = lax.broadcasted_iota(jnp.int32, (lc, lc), 0)
    col = lax.broadcasted_iota(jnp.int32, (lc, lc), 1)
    causal = col <= row
    tril = causal.astype(BF16)
    acum = sum(_dot(tril, part) for part in _split3(da))
    acum_t[...] = acum.T
    acum_parts = _split3(acum)
    dt_bf16 = dt.astype(BF16)
    head_of_col = lax.broadcasted_iota(jnp.int32, (lc, GROUP_W), 1) // SSM_HEAD_DIM
    sel_row = lax.broadcasted_iota(jnp.int32, (V7X_LANES, GROUP_W), 0)
    sel_col = lax.broadcasted_iota(jnp.int32, (V7X_LANES, GROUP_W), 1) // SSM_HEAD_DIM

    def group_body(g, carry):
        xo = pl.multiple_of(g * GROUP_W, GROUP_W)
        bo = pl.multiple_of(d_inner + g * SSM_STATE, SSM_STATE)
        co = pl.multiple_of(d_inner + SSM_GROUPS * SSM_STATE + g * SSM_STATE, SSM_STATE)
        x_g = act[:, pl.ds(xo, GROUP_W)]
        b_g = act[:, pl.ds(bo, SSM_STATE)].astype(BF16)
        c_g = act[:, pl.ds(co, SSM_STATE)].astype(BF16)
        sel = (sel_row == SSM_HEADS_PER_GROUP * g + sel_col).astype(BF16)
        ab = sum(_dot(part, sel) for part in acum_parts)
        dtb = _dot(dt_bf16, sel)
        alast = ab[lc - 1:lc, :]
        cb = _dot_nt(c_g, b_g)
        xdt = x_g * dtb
        y = jnp.zeros((lc, GROUP_W), F32)
        for r in range(SSM_HEADS_PER_GROUP):
            acol = ab[:, r * SSM_HEAD_DIM:r * SSM_HEAD_DIM + 1]
            arow = acum_t[pl.ds(SSM_HEADS_PER_GROUP * g + r, 1), :]
            decay = jnp.exp(jnp.where(causal, acol - arow, -jnp.inf))
            w = (cb * decay).astype(BF16)
            x_r = jnp.where(head_of_col == r, xdt, 0.0).astype(BF16)
            y = y + _dot(w, x_r)
        st = state[:, pl.ds(xo, GROUP_W)]
        y = y + _dot(c_g, st.astype(BF16)) * jnp.exp(ab)
        to_end = jnp.exp(alast - ab) * dtb
        xw = (x_g * to_end).astype(BF16)
        state[:, pl.ds(xo, GROUP_W)] = st * jnp.exp(alast) + _dot_tn(b_g, xw)
        y = y + dsk_ref[:, pl.ds(xo, GROUP_W)] * x_g
        gated = y * _silu(z_ref[:, pl.ds(xo, GROUP_W)].astype(F32))
        ms = jnp.mean(gated * gated, axis=-1, keepdims=True)
        o_ref[:, pl.ds(xo, GROUP_W)] = (gated * lax.rsqrt(ms + RMS_EPS) * nw_ref[:, pl.ds(xo, GROUP_W)]).astype(BF16)
        return carry

    lax.fori_loop(0, SSM_GROUPS, group_body, 0, unroll=4)


def _ssd(proj, dt_raw, conv_w, conv_b, dt_bias, a_log, d_skip, norm_w, *, batch, seq_len):
    t = proj.shape[0]
    d_inner = SSM_GROUPS * GROUP_W
    conv_dim = d_inner + 2 * SSM_GROUPS * SSM_STATE
    assert proj.shape[1] == d_inner + conv_dim and conv_dim == 2 * d_inner
    nc = seq_len // SSM_CHUNK
    lc = SSM_CHUNK
    pad = V7X_LANES - dt_bias.shape[0]
    heads = dt_bias.shape[0]
    row_map = lambda b, c: b * nc + c
    const = lambda b, c: (0, 0)
    return pl.pallas_call(
        _ssd_kernel,
        out_shape=_sds((t, d_inner), BF16),
        grid=(batch, nc),
        in_specs=[pl.BlockSpec((lc, d_inner), lambda b, c: (row_map(b, c), 0)),
                  pl.BlockSpec((lc, d_inner), lambda b, c: (row_map(b, c), 1)),
                  pl.BlockSpec((lc, d_inner), lambda b, c: (row_map(b, c), 2)),
                  pl.BlockSpec((lc, V7X_LANES), lambda b, c: (row_map(b, c), 0)),
                  pl.BlockSpec((SSM_CONV, conv_dim), const),
                  pl.BlockSpec((1, conv_dim), const),
                  pl.BlockSpec((1, V7X_LANES), const),
                  pl.BlockSpec((1, V7X_LANES), const),
                  pl.BlockSpec((1, d_inner), const),
                  pl.BlockSpec((1, d_inner), const)],
        out_specs=pl.BlockSpec((lc, d_inner), lambda b, c: (row_map(b, c), 0)),
        scratch_shapes=[pltpu.VMEM((conv_dim // V7X_LANES, CONV_HALO + lc, V7X_LANES), F32),
                        pltpu.VMEM((lc, conv_dim), F32),
                        pltpu.VMEM((SSM_STATE, d_inner), F32),
                        pltpu.VMEM((V7X_LANES, lc), F32)],
        compiler_params=_cparams(("arbitrary", "arbitrary"), 48),
        name="ssd_chunk_scan",
    )(proj, proj, proj, dt_raw, conv_w, conv_b.reshape(1, conv_dim),
      jnp.pad(dt_bias, (0, pad)).reshape(1, V7X_LANES), jnp.pad(a_log, (0, pad)).reshape(1, V7X_LANES),
      jnp.repeat(d_skip, SSM_HEAD_DIM).reshape(1, heads * SSM_HEAD_DIM), norm_w.reshape(1, d_inner))


def _route(logits, carry_ref):
    tm = logits.shape[0]
    lane = lax.broadcasted_iota(jnp.int32, (tm, V7X_LANES), 1)
    lane_f = lane.astype(F32)
    lg = jnp.where(lane < N_EXPERTS, logits, -jnp.inf)
    e = jnp.exp(lg - jnp.max(lg, axis=-1, keepdims=True))
    far = float(V7X_LANES)
    best = None
    for g in range(N_EXPERT_GROUPS):
        in_g = (lane >= g * EXPERTS_PER_GROUP) & (lane < (g + 1) * EXPERTS_PER_GROUP)
        eg = jnp.where(in_g, e, -1.0)
        m1 = jnp.max(eg, axis=-1, keepdims=True)
        i1 = jnp.min(jnp.where(eg == m1, lane_f, far), axis=-1, keepdims=True)
        eg2 = jnp.where(lane_f == i1, -1.0, eg)
        m2 = jnp.max(eg2, axis=-1, keepdims=True)
        i2 = jnp.min(jnp.where(eg2 == m2, lane_f, far), axis=-1, keepdims=True)
        cand = (m1 + m2, m1, m2, i1, i2)
        if best is None:
            best = cand
        else:
            take = cand[0] > best[0]
            best = tuple(jnp.where(take, c, b) for c, b in zip(cand, best))
    _, m1, m2, i1, i2 = best
    denom = m1 + m2
    hit1 = lane_f == i1
    hit2 = lane_f == i2
    onehot = (hit1 | hit2).astype(BF16)
    row = lax.broadcasted_iota(jnp.int32, (tm, tm), 0)
    col = lax.broadcasted_iota(jnp.int32, (tm, tm), 1)
    before = (col < row).astype(BF16)
    rank = _dot(before, onehot) + carry_ref[0:1, :]
    r1 = jnp.sum(jnp.where(hit1, rank, 0.0), axis=-1, keepdims=True)
    r2 = jnp.sum(jnp.where(hit2, rank, 0.0), axis=-1, keepdims=True)
    carry_ref[0:1, :] = carry_ref[0:1, :] + jnp.sum(onehot.astype(F32), axis=0, keepdims=True)
    rec = jnp.zeros((tm, V7X_LANES), F32)
    for k, val in enumerate((i1, i2, m1 / denom, m2 / denom, r1, r2)):
        rec = jnp.where(lane == k, val, rec)
    return rec


def _proj_ln_route_kernel(a_ref, w_ref, xres_ref, mod_ref, lng_ref, lnb_ref, wr_ref, br_ref,
                          x_ref, route_ref, cnt_ref, carry_ref, *, gate_idx, sc, sh):
    @pl.when(pl.program_id(0) == 0)
    def _():
        carry_ref[...] = jnp.zeros_like(carry_ref)

    m = mod_ref[0]
    y = _dot(a_ref[...], w_ref[...])
    v = DEEPNORM_ALPHA * xres_ref[...] + (1.0 + m[gate_idx:gate_idx + 1, :]) * y
    x1 = _layer_norm(v, lng_ref[...], lnb_ref[...])
    x_ref[...] = x1
    hm = (x1 * (1.0 + m[sc:sc + 1, :]) + m[sh:sh + 1, :]).astype(BF16)
    logits = _dot(hm, wr_ref[...]) + br_ref[...]
    route_ref[...] = _route(logits, carry_ref)
    cnt_ref[...] = carry_ref[...]


def _proj_ln_route(a, w, xres, mod, ln_g, ln_b, w_router, b_router, *, seq_len):
    t, k = a.shape
    d = w.shape[1]
    tm = TOKEN_TILE
    tiles_per_batch = seq_len // tm
    const = lambda i: (0, 0)
    wr = jnp.pad(w_router, ((0, 0), (0, V7X_LANES - N_EXPERTS))).astype(BF16)
    br = jnp.pad(b_router, (0, V7X_LANES - N_EXPERTS)).reshape(1, V7X_LANES)
    return pl.pallas_call(
        functools.partial(_proj_ln_route_kernel, gate_idx=2, sc=4, sh=3),
        out_shape=(_sds((t, d), F32), _sds((t, V7X_LANES), F32), _sds((V7X_SUBLANES, V7X_LANES), F32)),
        grid=(t // tm,),
        in_specs=[pl.BlockSpec((tm, k), lambda i: (i, 0)),
                  pl.BlockSpec((k, d), const),
                  pl.BlockSpec((tm, d), lambda i: (i, 0)),
                  pl.BlockSpec((1, 6, d), lambda i: (i // tiles_per_batch, 0, 0)),
                  pl.BlockSpec((1, d), const),
                  pl.BlockSpec((1, d), const),
                  pl.BlockSpec((d, V7X_LANES), const),
                  pl.BlockSpec((1, V7X_LANES), const)],
        out_specs=(pl.BlockSpec((tm, d), lambda i: (i, 0)),
                   pl.BlockSpec((tm, V7X_LANES), lambda i: (i, 0)),
                   pl.BlockSpec((V7X_SUBLANES, V7X_LANES), const)),
        scratch_shapes=[pltpu.VMEM((V7X_SUBLANES, V7X_LANES), F32)],
        compiler_params=_cparams(("arbitrary",), 40),
        name="proj_ln_route",
    )(a, w, xres, mod, ln_g.reshape(1, d), ln_b.reshape(1, d), wr, br)


def _record_copy(src_ref, src_slot, dst_ref, dst_slot, sem, n):
    src = src_ref.at[pl.ds(pl.multiple_of(src_slot * n, n), n)]
    dst = dst_ref.at[pl.ds(pl.multiple_of(dst_slot * n, n), n)]
    return pltpu.make_async_copy(src, dst, sem)


def _to_records(rec_ref, rows):
    t = rows.shape[0]
    n = rows.shape[1] // V7X_LANES
    for s in range(n):
        rec_ref[pl.ds(s, t, stride=n), :] = rows[:, s * V7X_LANES:(s + 1) * V7X_LANES]


def _from_records(rec_ref, t, n):
    return jnp.concatenate([rec_ref[pl.ds(s, t, stride=n), :] for s in range(n)], axis=1)


def _dispatch_kernel(p1_ref, p2_ref, x_ref, mod_ref, xs_init_ref, xs_ref, hm_ref, sem, *, sc, sh):
    del xs_init_ref
    tt, d = x_ref.shape
    n = d // V7X_LANES
    i = pl.program_id(0)
    n_steps = pl.num_programs(0)
    cur = lax.rem(i, 2)

    def drain(buf):
        def body(blk, carry):
            for _ in range(2 * DMA_UNROLL):
                _record_copy(hm_ref.at[buf], 0, xs_ref, 0, sem.at[buf], n).wait()
            return carry

        lax.fori_loop(0, tt // DMA_UNROLL, body, 0)

    @pl.when(i >= 2)
    def _():
        drain(cur)

    m = mod_ref[0]
    _to_records(hm_ref.at[cur], x_ref[...] * (1.0 + m[sc:sc + 1, :]) + m[sh:sh + 1, :])

    def issue(blk, carry):
        for u in range(DMA_UNROLL):
            r = blk * DMA_UNROLL + u
            _record_copy(hm_ref.at[cur], r, xs_ref, p1_ref[0, r], sem.at[cur], n).start(priority=0)
            _record_copy(hm_ref.at[cur], r, xs_ref, p2_ref[0, r], sem.at[cur], n).start(priority=1)
        return carry

    lax.fori_loop(0, tt // DMA_UNROLL, issue, 0)

    @pl.when(i == n_steps - 1)
    def _():
        drain(cur)

    @pl.when(jnp.logical_and(i == n_steps - 1, i >= 1))
    def _():
        drain(1 - cur)


def _slot_spec(tt, copy):
    return pl.BlockSpec((None, None, 1, tt), lambda i: (i, copy, 0, 0), memory_space=pltpu.SMEM)


def _dispatch(x, mod, slots, n_slots, *, seq_len):
    t, d = x.shape
    n = d // V7X_LANES
    tt = TOKEN_TILE
    tiles_per_batch = seq_len // tt
    return pl.pallas_call(
        functools.partial(_dispatch_kernel, sc=4, sh=3),
        out_shape=_sds((n_slots * n, V7X_LANES), F32),
        grid=(t // tt,),
        in_specs=[_slot_spec(tt, 0), _slot_spec(tt, 1),
                  pl.BlockSpec((tt, d), lambda i: (i, 0)),
                  pl.BlockSpec((1, 6, d), lambda i: (i // tiles_per_batch, 0, 0)),
                  pl.BlockSpec(memory_space=pl.ANY)],
        out_specs=pl.BlockSpec(memory_space=pl.ANY),
        scratch_shapes=[pltpu.VMEM((2, tt * n, V7X_LANES), F32), pltpu.SemaphoreType.DMA((2,))],
        input_output_aliases={4: 0},
        compiler_params=_cparams(("arbitrary",), 32),
        name="moe_dispatch",
    )(slots, slots, x, mod, jnp.zeros((n_slots * n, V7X_LANES), F32))


def _ffn_kernel(te_ref, nu_ref, x_ref, wg_ref, wu_ref, wd_ref, o_ref, wg_bf, wu_bf, wd_bf):
    i = pl.program_id(0)
    d, f = wg_bf.shape
    n = d // V7X_LANES
    tm = x_ref.shape[0] // n
    used = i < nu_ref[0]
    new_expert = jnp.logical_or(i == 0, te_ref[i] != te_ref[jnp.maximum(i - 1, 0)])

    @pl.when(jnp.logical_and(used, new_expert))
    def _():
        wg_bf[...] = wg_ref[0, 0].astype(BF16)
        wu_bf[...] = wu_ref[0, 0].astype(BF16)
        wd_bf[...] = wd_ref[0, 0].astype(BF16)

    @pl.when(used)
    def _():
        x = _from_records(x_ref, tm, n).astype(BF16)
        hg = _dot(x, wg_bf[...])
        hu = _dot(x, wu_bf[...])
        _to_records(o_ref, _dot((_silu(hg) * hu).astype(BF16), wd_bf[...]))

    @pl.when(jnp.logical_not(used))
    def _():
        o_ref[...] = jnp.zeros_like(o_ref)


def _ffn(xs, tile_expert, n_used, w_gate, w_up, w_down, *, layer):
    _, _, d, f = w_gate.shape
    n = d // V7X_LANES
    tm = MOE_ROW_TILE
    w_map = lambda i, te, nu: (layer, te[i], 0, 0)
    grid_spec = pltpu.PrefetchScalarGridSpec(
        num_scalar_prefetch=2,
        grid=(xs.shape[0] // (tm * n),),
        in_specs=[pl.BlockSpec((tm * n, V7X_LANES), lambda i, te, nu: (i, 0)),
                  pl.BlockSpec((1, 1, d, f), w_map),
                  pl.BlockSpec((1, 1, d, f), w_map),
                  pl.BlockSpec((1, 1, f, d), w_map)],
        out_specs=pl.BlockSpec((tm * n, V7X_LANES), lambda i, te, nu: (i, 0)),
        scratch_shapes=[pltpu.VMEM((d, f), BF16), pltpu.VMEM((d, f), BF16), pltpu.VMEM((f, d), BF16)],
    )
    return pl.pallas_call(
        _ffn_kernel,
        out_shape=_sds(xs.shape, F32),
        grid_spec=grid_spec,
        compiler_params=_cparams(("arbitrary",), 48),
        name="moe_ffn",
    )(tile_expert, n_used, xs, w_gate, w_up, w_down)


def _combine_ln_kernel(p1_ref, p2_ref, p1_next_ref, p2_next_ref, xres_ref, route_ref, mod_ref, lng_ref, lnb_ref,
                       ye_ref, o_ref, buf, sem, *, gate_idx):
    tt, d = xres_ref.shape
    n = d // V7X_LANES
    i = pl.program_id(0)
    cur = lax.rem(i, 2)

    def fetch(pa_ref, pb_ref, slot):
        def body(blk, carry):
            for u in range(DMA_UNROLL):
                r = blk * DMA_UNROLL + u
                _record_copy(ye_ref, pa_ref[0, r], buf.at[slot, 0], r, sem.at[slot], n).start(priority=0)
                _record_copy(ye_ref, pb_ref[0, r], buf.at[slot, 1], r, sem.at[slot], n).start(priority=1)
            return carry

        lax.fori_loop(0, tt // DMA_UNROLL, body, 0)

    @pl.when(i == 0)
    def _():
        fetch(p1_ref, p2_ref, 0)

    @pl.when(i + 1 < pl.num_programs(0))
    def _():
        fetch(p1_next_ref, p2_next_ref, 1 - cur)

    def drain(blk, carry):
        for _ in range(2 * DMA_UNROLL):
            _record_copy(ye_ref, 0, buf.at[cur, 0], 0, sem.at[cur], n).wait()
        return carry

    lax.fori_loop(0, tt // DMA_UNROLL, drain, 0)
    m = mod_ref[0]
    rec = route_ref[...]
    y = (rec[:, 2:3] * _from_records(buf.at[cur, 0], tt, n) + rec[:, 3:4] * _from_records(buf.at[cur, 1], tt, n))
    v = DEEPNORM_ALPHA * xres_ref[...] + (1.0 + m[gate_idx:gate_idx + 1, :]) * y
    o_ref[...] = _layer_norm(v, lng_ref[...], lnb_ref[...])


def _combine_ln(xres, route, slots, ye, mod, ln_g, ln_b, *, seq_len):
    t, d = xres.shape
    n = d // V7X_LANES
    tt = TOKEN_TILE
    tiles_per_batch = seq_len // tt
    n_steps = t // tt
    const = lambda i: (0, 0)

    def next_slot_spec(copy):
        return pl.BlockSpec((None, None, 1, tt), lambda i: (jnp.minimum(i + 1, n_steps - 1), copy, 0, 0),
                            memory_space=pltpu.SMEM)

    return pl.pallas_call(
        functools.partial(_combine_ln_kernel, gate_idx=5),
        out_shape=_sds((t, d), F32),
        grid=(n_steps,),
        in_specs=[_slot_spec(tt, 0), _slot_spec(tt, 1), next_slot_spec(0), next_slot_spec(1),
                  pl.BlockSpec((tt, d), lambda i: (i, 0)),
                  pl.BlockSpec((tt, V7X_LANES), lambda i: (i, 0)),
                  pl.BlockSpec((1, 6, d), lambda i: (i // tiles_per_batch, 0, 0)),
                  pl.BlockSpec((1, d), const),
                  pl.BlockSpec((1, d), const),
                  pl.BlockSpec(memory_space=pl.ANY)],
        out_specs=pl.BlockSpec((tt, d), lambda i: (i, 0)),
        scratch_shapes=[pltpu.VMEM((2, 2, tt * n, V7X_LANES), F32), pltpu.SemaphoreType.DMA((2,))],
        compiler_params=_cparams(("arbitrary",), 32),
        name="moe_combine_ln",
    )(slots, slots, slots, slots, xres, route, mod, ln_g.reshape(1, d), ln_b.reshape(1, d), ye)


def _slot_kernel(route_ref, off_ref, o_ref):
    tt = o_ref.shape[-1]
    lane = lax.broadcasted_iota(jnp.int32, (tt, V7X_LANES), 1)
    lane_f = lane.astype(F32)
    for s in range(o_ref.shape[0]):
        rec = route_ref[s * tt:(s + 1) * tt, :]
        off1 = jnp.sum(jnp.where(lane_f == rec[:, 0:1], off_ref[...], 0.0), axis=-1, keepdims=True)
        off2 = jnp.sum(jnp.where(lane_f == rec[:, 1:2], off_ref[...], 0.0), axis=-1, keepdims=True)
        both = jnp.where(lane == 0, off1 + rec[:, 4:5], jnp.where(lane == 1, off2 + rec[:, 5:6], 0.0))
        both_t = both.T
        o_ref[s, 0] = both_t[0:1, :].astype(jnp.int32)
        o_ref[s, 1] = both_t[1:2, :].astype(jnp.int32)


def _slots(route, row_off):
    t = route.shape[0]
    tt = TOKEN_TILE
    tiles_per_step = 8
    assert t % (tt * tiles_per_step) == 0
    off = jnp.pad(row_off.astype(F32), (0, V7X_LANES - N_EXPERTS)).reshape(1, V7X_LANES)
    return pl.pallas_call(
        _slot_kernel,
        out_shape=_sds((t // tt, 2, 1, tt), jnp.int32),
        grid=(t // (tt * tiles_per_step),),
        in_specs=[pl.BlockSpec((tt * tiles_per_step, V7X_LANES), lambda i: (i, 0)),
                  pl.BlockSpec((1, V7X_LANES), lambda i: (0, 0))],
        out_specs=pl.BlockSpec((tiles_per_step, 2, 1, tt), lambda i: (i, 0, 0, 0)),
        compiler_params=_cparams(("arbitrary",), 32),
        name="moe_slots",
    )(route, off)


def _moe_tables(counts, n_tokens):
    tm = MOE_ROW_TILE
    max_tiles = (2 * n_tokens) // tm + N_EXPERTS
    cnt = counts[0, :N_EXPERTS].astype(jnp.int32)
    tiles_e = (cnt + tm - 1) // tm
    tile_end = jnp.cumsum(tiles_e)
    row_off = (tile_end - tiles_e) * tm
    n_used = tile_end[-1:]
    tile_ids = jnp.minimum(jnp.arange(max_tiles, dtype=jnp.int32), n_used - 1)
    tile_expert = jnp.sum(tile_ids[:, None] >= tile_end[None, :], axis=1).astype(jnp.int32)
    return row_off, tile_expert, n_used.astype(jnp.int32), max_tiles * tm


def _moe_layer(x1, route, counts, mod, w_gate, w_up, w_down, ln_g, ln_b, *, layer, seq_len):
    row_off, tile_expert, n_used, n_slots = _moe_tables(counts, route.shape[0])
    slots = _slots(route, row_off)
    xs = _dispatch(x1, mod, slots, n_slots, seq_len=seq_len)
    ye = _ffn(xs, tile_expert, n_used, w_gate, w_up, w_down, layer=layer)
    return _combine_ln(x1, route, slots, ye, mod, ln_g, ln_b, seq_len=seq_len)


def _qkv_body(x_ref, mod_ref, w_ref, o_ref, hm_ref, *, sc, sh, tiles_per_batch, tiles_per_kind):
    @pl.when(pl.program_id(1) == 0)
    def _():
        _modulate_into(hm_ref, x_ref, mod_ref, sc, sh)

    tm = x_ref.shape[0]
    kind = pl.program_id(1) // tiles_per_kind
    acc = _dot(hm_ref[...], w_ref[...])
    acc = acc * jnp.where(kind == 0, ATTN_HEAD_DIM ** -0.5, 1.0)
    lane = lax.broadcasted_iota(jnp.int32, (tm, V7X_LANES), 1)
    low = lane < ATTN_HEAD_DIM
    row = lax.broadcasted_iota(jnp.int32, (tm, V7X_LANES), 0)
    pos = lax.rem(pl.program_id(0), tiles_per_batch) * tm + row
    k_extra = (lane == ATTN_HEAD_DIM + lax.div(pos, MOBA_BLOCK)).astype(F32)
    v_extra = (lane == ATTN_HEAD_DIM).astype(F32)
    extra = jnp.where(kind == 1, k_extra, jnp.where(kind == 2, v_extra, 0.0))
    for p in range(acc.shape[1] // V7X_LANES):
        t2 = acc[:, p * V7X_LANES:(p + 1) * V7X_LANES]
        even = jnp.where(low, t2, extra)
        odd = jnp.where(low, pltpu.roll(t2, ATTN_HEAD_DIM, axis=1), extra)
        o_ref[0, 0, 2 * p] = even.astype(BF16)
        o_ref[0, 0, 2 * p + 1] = odd.astype(BF16)


def _qkv(x, mod, w, *, batch, seq_len, tm=1024, tn=512):
    t, k = x.shape
    n = w.shape[1]
    heads_per_tile = tn // ATTN_HEAD_DIM
    tiles_per_batch = seq_len // tm
    tiles_per_kind = ATTN_HEADS // heads_per_tile
    assert n == 3 * ATTN_HEADS * ATTN_HEAD_DIM and seq_len % tm == 0
    return pl.pallas_call(
        functools.partial(_qkv_body, sc=1, sh=0, tiles_per_batch=tiles_per_batch, tiles_per_kind=tiles_per_kind),
        out_shape=_sds((3, batch, ATTN_HEADS, seq_len, V7X_LANES), BF16),
        grid=(t // tm, n // tn),
        in_specs=[pl.BlockSpec((tm, k), lambda i, j: (i, 0)),
                  pl.BlockSpec((1, 6, k), lambda i, j: (i // tiles_per_batch, 0, 0)),
                  pl.BlockSpec((k, tn), lambda i, j: (0, j))],
        out_specs=pl.BlockSpec((1, 1, heads_per_tile, tm, V7X_LANES),
                               lambda i, j: (j // tiles_per_kind, i // tiles_per_batch, j % tiles_per_kind,
                                             i % tiles_per_batch, 0)),
        scratch_shapes=[pltpu.VMEM((tm, k), BF16)],
        compiler_params=_cparams(("arbitrary", "arbitrary"), 40),
        name="qkv_proj",
    )(x, mod, w)


ATTN_HEADS_PER_STEP = 2
ATTN_Q_TILE = 1024
GATE_ROWS = 512


def _attn_kernel(q_ref, k_ref, v_ref, pm_ref, o_ref, km_ref, qa_ref):
    tq = ATTN_Q_TILE
    seq_len = q_ref.shape[2]
    n_blocks = seq_len // MOBA_BLOCK
    heads = range(ATTN_HEADS_PER_STEP)
    qi = pl.program_id(2)

    @pl.when(qi == 0)
    def _():
        for hh in heads:
            km_ref[hh] = _dot(pm_ref[...], k_ref[0, hh]).astype(BF16)
        lane = lax.broadcasted_iota(jnp.int32, (GATE_ROWS, V7X_LANES), 1)
        lane_f = lane.astype(F32)
        blk = lane - ATTN_HEAD_DIM
        mask_lanes = (blk >= 0) & (blk < n_blocks)
        row = lax.broadcasted_iota(jnp.int32, (GATE_ROWS, V7X_LANES), 0)

        def gate_rows(ci, carry, *, select):
            r0 = pl.multiple_of(ci * GATE_ROWS, GATE_ROWS)
            past = (blk >= 0) & (blk < lax.div(r0 + row, MOBA_BLOCK))
            for hh in heads:
                q = q_ref[0, hh, pl.ds(r0, GATE_ROWS), :]
                chosen = past
                if select:
                    gate = jnp.where(past, _dot_nt(q, km_ref[hh]), -jnp.inf)
                    chosen = jnp.zeros((GATE_ROWS, V7X_LANES), jnp.bool_)
                    for _ in range(MOBA_TOPK):
                        mx = jnp.max(gate, axis=-1, keepdims=True)
                        idx = jnp.min(jnp.where(gate == mx, lane_f, float(V7X_LANES)), axis=-1, keepdims=True)
                        pick = lane_f == idx
                        chosen = chosen | pick
                        gate = jnp.where(pick, -jnp.inf, gate)
                own = blk == lax.div(r0 + row, MOBA_BLOCK)
                blocked = mask_lanes & jnp.logical_not((chosen & past) | own)
                qa_ref[hh, pl.ds(r0, GATE_ROWS), :] = (q.astype(F32) + jnp.where(blocked, NEG_INF, 0.0)).astype(BF16)
            return carry

        keep_all = min(seq_len, (MOBA_TOPK + 1) * MOBA_BLOCK) // GATE_ROWS
        lax.fori_loop(0, keep_all, functools.partial(gate_rows, select=False), 0)
        lax.fori_loop(keep_all, seq_len // GATE_ROWS, functools.partial(gate_rows, select=True), 0)

    q0 = pl.multiple_of(qi * tq, tq)

    def online_update(m_i, acc, s, v):
        m_new = jnp.maximum(m_i, jnp.max(s, axis=-1, keepdims=True))
        p = jnp.exp(s - m_new)
        return m_new, acc * jnp.exp(m_i - m_new) + _dot(p.astype(BF16), v)

    def kv_step(j, carry):
        start = pl.multiple_of(j * tq, tq)
        out = []
        for hh in heads:
            m_i, acc = carry[hh]
            s = _dot_nt(qa_ref[hh, pl.ds(q0, tq), :], k_ref[0, hh, pl.ds(start, tq), :])
            out.append(online_update(m_i, acc, s, v_ref[0, hh, pl.ds(start, tq), :]))
        return tuple(out)

    init = tuple((jnp.full((tq, 1), NEG_INF, F32), jnp.zeros((tq, V7X_LANES), F32)) for _ in heads)
    carry = lax.fori_loop(0, qi, kv_step, init)

    row = lax.broadcasted_iota(jnp.int32, (tq, tq), 0)
    col = lax.broadcasted_iota(jnp.int32, (tq, tq), 1)
    outs = []
    for hh in heads:
        m_i, acc = carry[hh]
        s = _dot_nt(qa_ref[hh, pl.ds(q0, tq), :], k_ref[0, hh, pl.ds(q0, tq), :])
        _, acc = online_update(m_i, acc, jnp.where(col <= row, s, NEG_INF), v_ref[0, hh, pl.ds(q0, tq), :])
        outs.append(acc / acc[:, ATTN_HEAD_DIM:ATTN_HEAD_DIM + 1])
    lane = lax.broadcasted_iota(jnp.int32, (tq, V7X_LANES), 1)
    for pair in range(ATTN_HEADS_PER_STEP // 2):
        both = jnp.where(lane < ATTN_HEAD_DIM, outs[2 * pair], pltpu.roll(outs[2 * pair + 1], ATTN_HEAD_DIM, axis=1))
        o_ref[0, :, pair * V7X_LANES:(pair + 1) * V7X_LANES] = both.astype(BF16)


def _attention(qkv, *, batch, seq_len):
    nb = seq_len // MOBA_BLOCK
    tq = ATTN_Q_TILE
    hps = ATTN_HEADS_PER_STEP
    assert hps % 2 == 0 and 2 * ATTN_HEAD_DIM == V7X_LANES and nb <= V7X_LANES - ATTN_HEAD_DIM
    assert seq_len % GATE_ROWS == 0 and seq_len % tq == 0 and tq % MOBA_BLOCK == 0
    rows = jnp.arange(V7X_LANES, dtype=jnp.int32)[:, None] - ATTN_HEAD_DIM
    cols = jnp.arange(seq_len, dtype=jnp.int32)[None, :] // MOBA_BLOCK
    pool = jnp.where(rows == cols, 1.0 / MOBA_BLOCK, 0.0).astype(BF16)
    def head_spec(kind):
        return pl.BlockSpec((None, 1, hps, seq_len, V7X_LANES), lambda b, h, i: (kind, b, h, 0, 0))

    return pl.pallas_call(
        _attn_kernel,
        out_shape=_sds((batch, seq_len, ATTN_HEADS * ATTN_HEAD_DIM), BF16),
        grid=(batch, ATTN_HEADS // hps, seq_len // tq),
        in_specs=[head_spec(0), head_spec(1), head_spec(2),
                  pl.BlockSpec((V7X_LANES, seq_len), lambda b, h, i: (0, 0))],
        out_specs=pl.BlockSpec((1, tq, hps * ATTN_HEAD_DIM), lambda b, h, i: (b, i, h)),
        scratch_shapes=[pltpu.VMEM((hps, V7X_LANES, V7X_LANES), BF16),
                        pltpu.VMEM((hps, seq_len, V7X_LANES), BF16)],
        compiler_params=_cparams(("arbitrary", "arbitrary", "arbitrary"), 48),
        name="moba_attention",
    )(qkv, qkv, qkv, pool)


def kernel(x, c, w_ada, b_ada, ln_g, ln_b, ssm_w_in, ssm_conv_w, ssm_conv_b, ssm_dt_bias, ssm_a_log, ssm_d,
           ssm_norm_w, ssm_w_out, attn_w_qkv, attn_w_o, w_router, b_router, moe_w_gate, moe_w_up, moe_w_down):
    batch, seq_len, d = x.shape
    assert seq_len % 512 == 0 and seq_len % SSM_CHUNK == 0 and seq_len % MOBA_BLOCK == 0
    t = batch * seq_len
    mod = _ada_mod(c, w_ada, b_ada)
    xf = x.reshape(t, d)

    d_inner = SSM_GROUPS * GROUP_W
    conv_dim = d_inner + 2 * SSM_GROUPS * SSM_STATE
    w_in = ssm_w_in[0]
    heads = ssm_dt_bias.shape[1]
    w_main = w_in[:, :d_inner + conv_dim].astype(BF16)
    w_dt = jnp.pad(w_in[:, d_inner + conv_dim:], ((0, 0), (0, V7X_LANES - heads))).astype(BF16)
    proj = _mm_mod(xf, mod[0], w_main, sc=1, sh=0, seq_len=seq_len, out_dtype=BF16)
    dt_raw = _mm_mod(xf, mod[0], w_dt, sc=1, sh=0, seq_len=seq_len, out_dtype=F32)
    y = _ssd(proj, dt_raw, ssm_conv_w[0], ssm_conv_b[0], ssm_dt_bias[0], ssm_a_log[0], ssm_d[0], ssm_norm_w[0],
             batch=batch, seq_len=seq_len)
    x1, route, counts = _proj_ln_route(y, ssm_w_out[0].astype(BF16), xf, mod[0], ln_g[0, 0], ln_b[0, 0],
                                       w_router, b_router, seq_len=seq_len)
    xf = _moe_layer(x1, route, counts, mod[0], moe_w_gate, moe_w_up, moe_w_down, ln_g[0, 1], ln_b[0, 1],
                    layer=0, seq_len=seq_len)

    w_qkv = attn_w_qkv[0].astype(BF16)
    qkv = _qkv(xf, mod[1], w_qkv, batch=batch, seq_len=seq_len)
    o = _attention(qkv, batch=batch, seq_len=seq_len).reshape(t, d)
    x1, route, counts = _proj_ln_route(o, attn_w_o[0].astype(BF16), xf, mod[1], ln_g[1, 0], ln_b[1, 0],
                                       w_router, b_router, seq_len=seq_len)
    xf = _moe_layer(x1, route, counts, mod[1], moe_w_gate, moe_w_up, moe_w_down, ln_g[1, 1], ln_b[1, 1],
                    layer=1, seq_len=seq_len)
    return xf.reshape(batch, seq_len, d)
```

```python
import functools

import jax
import jax.numpy as jnp
from jax import lax
from jax.experimental import pallas as pl
from jax.experimental.pallas import tpu as pltpu

F32 = jnp.float32
BF16 = jnp.bfloat16

V7X_LANES = 128
V7X_SUBLANES = 8
V7X_VMEM_BYTES = 64 * 1024 * 1024

DEPTH = 2
SSM_HEAD_DIM = 64
SSM_STATE = 128
SSM_GROUPS = 8
SSM_HEADS_PER_GROUP = 4
SSM_CONV = 4
SSM_CHUNK = 256
ATTN_HEADS = 16
ATTN_HEAD_DIM = 64
MOBA_BLOCK = 256
MOBA_TOPK = 3
N_EXPERTS = 32
N_EXPERT_GROUPS = 4
EXPERTS_PER_GROUP = N_EXPERTS // N_EXPERT_GROUPS
DEEPNORM_ALPHA = (2.0 * DEPTH) ** 0.25
LN_EPS = 1e-5
RMS_EPS = 1e-5
NEG_INF = -1e30

GROUP_W = SSM_HEADS_PER_GROUP * SSM_HEAD_DIM
CONV_HALO = V7X_SUBLANES
MOE_ROW_TILE = 512
TOKEN_TILE = 256
DMA_UNROLL = 8


def _cparams(semantics, vmem_mib):
    assert vmem_mib * 1024 * 1024 <= V7X_VMEM_BYTES
    return pltpu.CompilerParams(dimension_semantics=semantics, vmem_limit_bytes=vmem_mib * 1024 * 1024)


def _sds(shape, dtype):
    return jax.ShapeDtypeStruct(shape, dtype)


def _dot(a, b):
    return jnp.dot(a, b, preferred_element_type=F32)


def _dot_nt(a, b):
    return lax.dot_general(a, b, (((1,), (1,)), ((), ())), preferred_element_type=F32)


def _dot_tn(a, b):
    return lax.dot_general(a, b, (((0,), (0,)), ((), ())), preferred_element_type=F32)


def _split3(a):
    hi = a.astype(BF16)
    r1 = a - hi.astype(F32)
    mid = r1.astype(BF16)
    lo = (r1 - mid.astype(F32)).astype(BF16)
    return hi, mid, lo


def _silu(x):
    return x * jax.nn.sigmoid(x)


def _layer_norm(v, gamma, beta):
    mu = jnp.mean(v, axis=-1, keepdims=True)
    d = v - mu
    var = jnp.mean(d * d, axis=-1, keepdims=True)
    return d * lax.rsqrt(var + LN_EPS) * gamma + beta


def _ada_kernel(c_ref, w_ref, b_ref, o_ref):
    cs = _silu(c_ref[...])
    o_ref[0] = jnp.dot(cs, w_ref[0], preferred_element_type=F32, precision=lax.Precision.HIGHEST) + b_ref[0]


def _ada_mod(c, w_ada, b_ada):
    depth, d, n = w_ada.shape
    b = c.shape[0]
    tn = 1024
    out = pl.pallas_call(
        _ada_kernel,
        out_shape=_sds((depth, b, n), F32),
        grid=(depth, n // tn),
        in_specs=[pl.BlockSpec((b, d), lambda l, j: (0, 0)),
                  pl.BlockSpec((1, d, tn), lambda l, j: (l, 0, j)),
                  pl.BlockSpec((1, 1, tn), lambda l, j: (l, 0, j))],
        out_specs=pl.BlockSpec((1, b, tn), lambda l, j: (l, 0, j)),
        compiler_params=_cparams(("arbitrary", "arbitrary"), 32),
        name="ada_mod",
    )(c, w_ada, b_ada.reshape(depth, 1, n))
    return out.reshape(depth, b, 6, d)


def _modulate_into(hm_ref, x_ref, mod_ref, sc, sh):
    m = mod_ref[0]
    hm_ref[...] = (x_ref[...] * (1.0 + m[sc:sc + 1, :]) + m[sh:sh + 1, :]).astype(BF16)


def _mm_mod_kernel(x_ref, mod_ref, w_ref, o_ref, hm_ref, *, sc, sh):
    @pl.when(pl.program_id(1) == 0)
    def _():
        _modulate_into(hm_ref, x_ref, mod_ref, sc, sh)

    o_ref[...] = _dot(hm_ref[...], w_ref[...]).astype(o_ref.dtype)


def _mm_mod(x, mod, w, *, sc, sh, seq_len, out_dtype, tm=2048, tn=1536):
    t, k = x.shape
    n = w.shape[1]
    tn = min(tn, n)
    assert seq_len % tm == 0 and n % tn == 0
    tiles_per_batch = seq_len // tm
    return pl.pallas_call(
        functools.partial(_mm_mod_kernel, sc=sc, sh=sh),
        out_shape=_sds((t, n), out_dtype),
        grid=(t // tm, n // tn),
        in_specs=[pl.BlockSpec((tm, k), lambda i, j: (i, 0)),
                  pl.BlockSpec((1, 6, k), lambda i, j: (i // tiles_per_batch, 0, 0)),
                  pl.BlockSpec((k, tn), lambda i, j: (0, j))],
        out_specs=pl.BlockSpec((tm, tn), lambda i, j: (i, j)),
        scratch_shapes=[pltpu.VMEM((tm, k), BF16)],
        compiler_params=_cparams(("arbitrary", "arbitrary"), 48),
        name="mm_mod",
    )(x, mod, w)


def _ssd_kernel(z_ref, xp_ref, bcp_ref, dtr_ref, cw_ref, cb_ref, dtb_ref, alog_ref, dsk_ref, nw_ref,
                o_ref, ubuf, act, state, acum_t):
    lc = SSM_CHUNK
    d_inner = xp_ref.shape[1]
    n_slabs = ubuf.shape[0]
    x_slabs = d_inner // V7X_LANES
    chunk = pl.program_id(1)

    @pl.when(chunk == 0)
    def _():
        ubuf[:, 0:CONV_HALO, :] = jnp.zeros((n_slabs, CONV_HALO, V7X_LANES), F32)
        state[...] = jnp.zeros_like(state)

    def conv_slabs(src_ref, first_slab, count):
        def slab_body(j, carry):
            src_off = pl.multiple_of(j * V7X_LANES, V7X_LANES)
            slab = first_slab + j
            off = pl.multiple_of(slab * V7X_LANES, V7X_LANES)
            u = src_ref[:, pl.ds(src_off, V7X_LANES)].astype(F32)
            ubuf[slab, CONV_HALO:CONV_HALO + lc, :] = u
            acc = cb_ref[:, pl.ds(off, V7X_LANES)] + cw_ref[SSM_CONV - 1:SSM_CONV, pl.ds(off, V7X_LANES)] * u
            for k in range(SSM_CONV - 1):
                tap = CONV_HALO - (SSM_CONV - 1) + k
                acc = acc + cw_ref[k:k + 1, pl.ds(off, V7X_LANES)] * ubuf[slab, tap:tap + lc, :]
            act[:, pl.ds(off, V7X_LANES)] = _silu(acc)
            ubuf[slab, 0:CONV_HALO, :] = u[lc - CONV_HALO:lc, :]
            return carry

        lax.fori_loop(0, count, slab_body, 0)

    conv_slabs(xp_ref, 0, x_slabs)
    conv_slabs(bcp_ref, x_slabs, n_slabs - x_slabs)

    dt = jax.nn.softplus(dtr_ref[...] + dtb_ref[...])
    da = dt * (-jnp.exp(alog_ref[...]))
    row = lax.broadcasted_iota(jnp.int32, (lc, lc), 0)
    col = lax.broadcasted_iota(jnp.int32, (lc, lc), 1)
    causal = col <= row
    tril = causal.astype(BF16)
    acum = sum(_dot(tril, part) for part in _split3(da))
    acum_t[...] = acum.T
    acum_parts = _split3(acum)
    dt_bf16 = dt.astype(BF16)
    head_of_col = lax.broadcasted_iota(jnp.int32, (lc, GROUP_W), 1) // SSM_HEAD_DIM
    sel_row = lax.broadcasted_iota(jnp.int32, (V7X_LANES, GROUP_W), 0)
    sel_col = lax.broadcasted_iota(jnp.int32, (V7X_LANES, GROUP_W), 1) // SSM_HEAD_DIM

    def group_body(g, carry):
        xo = pl.multiple_of(g * GROUP_W, GROUP_W)
        bo = pl.multiple_of(d_inner + g * SSM_STATE, SSM_STATE)
        co = pl.multiple_of(d_inner + SSM_GROUPS * SSM_STATE + g * SSM_STATE, SSM_STATE)
        x_g = act[:, pl.ds(xo, GROUP_W)]
        b_g = act[:, pl.ds(bo, SSM_STATE)].astype(BF16)
        c_g = act[:, pl.ds(co, SSM_STATE)].astype(BF16)
        sel = (sel_row == SSM_HEADS_PER_GROUP * g + sel_col).astype(BF16)
        ab = sum(_dot(part, sel) for part in acum_parts)
        dtb = _dot(dt_bf16, sel)
        alast = ab[lc - 1:lc, :]
        cb = _dot_nt(c_g, b_g)
        xdt = x_g * dtb
        y = jnp.zeros((lc, GROUP_W), F32)
        for r in range(SSM_HEADS_PER_GROUP):
            acol = ab[:, r * SSM_HEAD_DIM:r * SSM_HEAD_DIM + 1]
            arow = acum_t[pl.ds(SSM_HEADS_PER_GROUP * g + r, 1), :]
            decay = jnp.exp(jnp.where(causal, acol - arow, -jnp.inf))
            w = (cb * decay).astype(BF16)
            x_r = jnp.where(head_of_col == r, xdt, 0.0).astype(BF16)
            y = y + _dot(w, x_r)
        st = state[:, pl.ds(xo, GROUP_W)]
        y = y + _dot(c_g, st.astype(BF16)) * jnp.exp(ab)
        to_end = jnp.exp(alast - ab) * dtb
        xw = (x_g * to_end).astype(BF16)
        state[:, pl.ds(xo, GROUP_W)] = st * jnp.exp(alast) + _dot_tn(b_g, xw)
        y = y + dsk_ref[:, pl.ds(xo, GROUP_W)] * x_g
        gated = y * _silu(z_ref[:, pl.ds(xo, GROUP_W)].astype(F32))
        ms = jnp.mean(gated * gated, axis=-1, keepdims=True)
        o_ref[:, pl.ds(xo, GROUP_W)] = (gated * lax.rsqrt(ms + RMS_EPS) * nw_ref[:, pl.ds(xo, GROUP_W)]).astype(BF16)
        return carry

    lax.fori_loop(0, SSM_GROUPS, group_body, 0, unroll=4)


def _ssd(proj, dt_raw, conv_w, conv_b, dt_bias, a_log, d_skip, norm_w, *, batch, seq_len):
    t = proj.shape[0]
    d_inner = SSM_GROUPS * GROUP_W
    conv_dim = d_inner + 2 * SSM_GROUPS * SSM_STATE
    assert proj.shape[1] == d_inner + conv_dim and conv_dim == 2 * d_inner
    nc = seq_len // SSM_CHUNK
    lc = SSM_CHUNK
    pad = V7X_LANES - dt_bias.shape[0]
    heads = dt_bias.shape[0]
    row_map = lambda b, c: b * nc + c
    const = lambda b, c: (0, 0)
    return pl.pallas_call(
        _ssd_kernel,
        out_shape=_sds((t, d_inner), BF16),
        grid=(batch, nc),
        in_specs=[pl.BlockSpec((lc, d_inner), lambda b, c: (row_map(b, c), 0)),
                  pl.BlockSpec((lc, d_inner), lambda b, c: (row_map(b, c), 1)),
                  pl.BlockSpec((lc, d_inner), lambda b, c: (row_map(b, c), 2)),
                  pl.BlockSpec((lc, V7X_LANES), lambda b, c: (row_map(b, c), 0)),
                  pl.BlockSpec((SSM_CONV, conv_dim), const),
                  pl.BlockSpec((1, conv_dim), const),
                  pl.BlockSpec((1, V7X_LANES), const),
                  pl.BlockSpec((1, V7X_LANES), const),
                  pl.BlockSpec((1, d_inner), const),
                  pl.BlockSpec((1, d_inner), const)],
        out_specs=pl.BlockSpec((lc, d_inner), lambda b, c: (row_map(b, c), 0)),
        scratch_shapes=[pltpu.VMEM((conv_dim // V7X_LANES, CONV_HALO + lc, V7X_LANES), F32),
                        pltpu.VMEM((lc, conv_dim), F32),
                        pltpu.VMEM((SSM_STATE, d_inner), F32),
                        pltpu.VMEM((V7X_LANES, lc), F32)],
        compiler_params=_cparams(("arbitrary", "arbitrary"), 48),
        name="ssd_chunk_scan",
    )(proj, proj, proj, dt_raw, conv_w, conv_b.reshape(1, conv_dim),
      jnp.pad(dt_bias, (0, pad)).reshape(1, V7X_LANES), jnp.pad(a_log, (0, pad)).reshape(1, V7X_LANES),
      jnp.repeat(d_skip, SSM_HEAD_DIM).reshape(1, heads * SSM_HEAD_DIM), norm_w.reshape(1, d_inner))


def _route(logits_t, carry_ref):
    tm = logits_t.shape[1]
    lg = logits_t[0:N_EXPERTS, :]
    eid = lax.broadcasted_iota(jnp.int32, (N_EXPERTS, tm), 0).astype(F32)
    e = jnp.exp(lg - jnp.max(lg, axis=0, keepdims=True))
    far = float(N_EXPERTS)
    best = None
    for g in range(N_EXPERT_GROUPS):
        eg = e[g * EXPERTS_PER_GROUP:(g + 1) * EXPERTS_PER_GROUP, :]
        ig = (lax.broadcasted_iota(jnp.int32, (EXPERTS_PER_GROUP, tm), 0) + g * EXPERTS_PER_GROUP).astype(F32)
        m1 = jnp.max(eg, axis=0, keepdims=True)
        i1 = jnp.min(jnp.where(eg == m1, ig, far), axis=0, keepdims=True)
        eg2 = jnp.where(ig == i1, -1.0, eg)
        m2 = jnp.max(eg2, axis=0, keepdims=True)
        i2 = jnp.min(jnp.where(eg2 == m2, ig, far), axis=0, keepdims=True)
        cand = (m1 + m2, m1, m2, i1, i2)
        if best is None:
            best = cand
        else:
            take = cand[0] > best[0]
            best = tuple(jnp.where(take, c, b) for c, b in zip(cand, best))
    _, m1, m2, i1, i2 = best
    denom = m1 + m2
    hit1 = eid == i1
    hit2 = eid == i2
    onehot = (hit1 | hit2).astype(F32)
    row = lax.broadcasted_iota(jnp.int32, (tm, tm), 0)
    col = lax.broadcasted_iota(jnp.int32, (tm, tm), 1)
    earlier = (row < col).astype(BF16)
    rank = _dot(onehot.astype(BF16), earlier) + carry_ref[:, 0:1]
    r1 = jnp.sum(jnp.where(hit1, rank, 0.0), axis=0, keepdims=True)
    r2 = jnp.sum(jnp.where(hit2, rank, 0.0), axis=0, keepdims=True)
    carry_ref[...] = carry_ref[...] + jnp.sum(onehot, axis=1, keepdims=True)
    sub = lax.broadcasted_iota(jnp.int32, (V7X_SUBLANES, tm), 0)
    rec_t = jnp.zeros((V7X_SUBLANES, tm), F32)
    for k, val in enumerate((i1, i2, m1 / denom, m2 / denom, r1, r2)):
        rec_t = jnp.where(sub == k, val, rec_t)
    rec_t = jnp.concatenate([rec_t, jnp.zeros((V7X_LANES - V7X_SUBLANES, tm), F32)], axis=0)
    return rec_t.T


def _proj_ln_route_kernel(a_ref, w_ref, xres_ref, mod_ref, lng_ref, lnb_ref, wr_ref, br_ref,
                          x_ref, route_ref, cnt_ref, carry_ref, *, gate_idx, sc, sh):
    @pl.when(pl.program_id(0) == 0)
    def _():
        carry_ref[...] = jnp.zeros_like(carry_ref)

    m = mod_ref[0]
    y = _dot(a_ref[...], w_ref[...])
    v = DEEPNORM_ALPHA * xres_ref[...] + (1.0 + m[gate_idx:gate_idx + 1, :]) * y
    x1 = _layer_norm(v, lng_ref[...], lnb_ref[...])
    x_ref[...] = x1
    hm = (x1 * (1.0 + m[sc:sc + 1, :]) + m[sh:sh + 1, :]).astype(BF16)
    logits_t = _dot_nt(wr_ref[...], hm) + br_ref[...]
    route_ref[...] = _route(logits_t, carry_ref)
    cnt_ref[...] = carry_ref[...]


def _proj_ln_route(a, w, xres, mod, ln_g, ln_b, w_router, b_router, *, seq_len):
    t, k = a.shape
    d = w.shape[1]
    tm = TOKEN_TILE
    tiles_per_batch = seq_len // tm
    const = lambda i: (0, 0)
    wr = jnp.pad(w_router.T, ((0, V7X_LANES - N_EXPERTS), (0, 0))).astype(BF16)
    br = jnp.pad(b_router, (0, V7X_LANES - N_EXPERTS)).reshape(V7X_LANES, 1)
    return pl.pallas_call(
        functools.partial(_proj_ln_route_kernel, gate_idx=2, sc=4, sh=3),
        out_shape=(_sds((t, d), F32), _sds((t, V7X_LANES), F32), _sds((N_EXPERTS, V7X_LANES), F32)),
        grid=(t // tm,),
        in_specs=[pl.BlockSpec((tm, k), lambda i: (i, 0)),
                  pl.BlockSpec((k, d), const),
                  pl.BlockSpec((tm, d), lambda i: (i, 0)),
                  pl.BlockSpec((1, 6, d), lambda i: (i // tiles_per_batch, 0, 0)),
                  pl.BlockSpec((1, d), const),
                  pl.BlockSpec((1, d), const),
                  pl.BlockSpec((V7X_LANES, d), const),
                  pl.BlockSpec((V7X_LANES, 1), const)],
        out_specs=(pl.BlockSpec((tm, d), lambda i: (i, 0)),
                   pl.BlockSpec((tm, V7X_LANES), lambda i: (i, 0)),
                   pl.BlockSpec((N_EXPERTS, V7X_LANES), const)),
        scratch_shapes=[pltpu.VMEM((N_EXPERTS, V7X_LANES), F32)],
        compiler_params=_cparams(("arbitrary",), 40),
        name="proj_ln_route",
    )(a, w, xres, mod, ln_g.reshape(1, d), ln_b.reshape(1, d), wr, br)


def _record_copy(src_ref, src_slot, dst_ref, dst_slot, sem, n):
    src = src_ref.at[pl.ds(pl.multiple_of(src_slot * n, n), n)]
    dst = dst_ref.at[pl.ds(pl.multiple_of(dst_slot * n, n), n)]
    return pltpu.make_async_copy(src, dst, sem)


def _to_records(rec_ref, rows):
    t = rows.shape[0]
    n = rows.shape[1] // V7X_LANES
    for s in range(n):
        rec_ref[pl.ds(s, t, stride=n), :] = rows[:, s * V7X_LANES:(s + 1) * V7X_LANES]


def _from_records(rec_ref, t, n):
    return jnp.concatenate([rec_ref[pl.ds(s, t, stride=n), :] for s in range(n)], axis=1)


def _dispatch_kernel(tail_ref, p1_ref, p2_ref, x_ref, mod_ref, xs_ref, hm_ref, zero_ref, sem, zero_sem, *, sc, sh):
    tt, d = x_ref.shape
    n = d // V7X_LANES
    i = pl.program_id(0)
    n_steps = pl.num_programs(0)
    cur = lax.rem(i, 2)

    @pl.when(i == 0)
    def _():
        zero_ref[...] = jnp.zeros_like(zero_ref)
        rows = zero_ref.shape[0]

        def tail_copy(e):
            start = pl.multiple_of(jnp.maximum(tail_ref[0, e], 0) * n, rows)
            return pltpu.make_async_copy(zero_ref, xs_ref.at[pl.ds(start, rows)], zero_sem)

        def fill(e, carry):
            @pl.when(tail_ref[0, e] >= 0)
            def _():
                tail_copy(e).start()
            return carry

        def fill_wait(e, carry):
            @pl.when(tail_ref[0, e] >= 0)
            def _():
                tail_copy(e).wait()
            return carry

        def spare_copy(j):
            return pltpu.make_async_copy(zero_ref, xs_ref.at[pl.ds(pl.multiple_of(j * rows, rows), rows)], zero_sem)

        def spare_fill(j, carry):
            spare_copy(j).start()
            return carry

        def spare_wait(j, carry):
            spare_copy(j).wait()
            return carry

        n_tiles = xs_ref.shape[0] // rows
        lax.fori_loop(0, N_EXPERTS, fill, 0)
        lax.fori_loop(tail_ref[0, N_EXPERTS], n_tiles, spare_fill, 0)
        lax.fori_loop(0, N_EXPERTS, fill_wait, 0)
        lax.fori_loop(tail_ref[0, N_EXPERTS], n_tiles, spare_wait, 0)

    def drain(buf):
        def body(blk, carry):
            for _ in range(2 * DMA_UNROLL):
                _record_copy(hm_ref.at[buf], 0, xs_ref, 0, sem.at[buf], n).wait()
            return carry

        lax.fori_loop(0, tt // DMA_UNROLL, body, 0)

    @pl.when(i >= 2)
    def _():
        drain(cur)

    m = mod_ref[0]
    _to_records(hm_ref.at[cur], x_ref[...] * (1.0 + m[sc:sc + 1, :]) + m[sh:sh + 1, :])

    def issue(blk, carry):
        for u in range(DMA_UNROLL):
            r = blk * DMA_UNROLL + u
            _record_copy(hm_ref.at[cur], r, xs_ref, p1_ref[0, r], sem.at[cur], n).start(priority=0)
            _record_copy(hm_ref.at[cur], r, xs_ref, p2_ref[0, r], sem.at[cur], n).start(priority=1)
        return carry

    lax.fori_loop(0, tt // DMA_UNROLL, issue, 0)

    @pl.when(i == n_steps - 1)
    def _():
        drain(cur)

    @pl.when(jnp.logical_and(i == n_steps - 1, i >= 1))
    def _():
        drain(1 - cur)


def _slot_spec(tt, copy):
    return pl.BlockSpec((None, None, 1, tt), lambda i: (i, copy, 0, 0), memory_space=pltpu.SMEM)


def _dispatch(x, mod, slots, tail_slot, n_slots, *, seq_len):
    t, d = x.shape
    n = d // V7X_LANES
    tt = TOKEN_TILE
    tiles_per_batch = seq_len // tt
    return pl.pallas_call(
        functools.partial(_dispatch_kernel, sc=4, sh=3),
        out_shape=_sds((n_slots * n, V7X_LANES), F32),
        grid=(t // tt,),
        in_specs=[pl.BlockSpec((1, N_EXPERTS + 1), lambda i: (0, 0), memory_space=pltpu.SMEM),
                  _slot_spec(tt, 0), _slot_spec(tt, 1),
                  pl.BlockSpec((tt, d), lambda i: (i, 0)),
                  pl.BlockSpec((1, 6, d), lambda i: (i // tiles_per_batch, 0, 0))],
        out_specs=pl.BlockSpec(memory_space=pl.ANY),
        scratch_shapes=[pltpu.VMEM((2, tt * n, V7X_LANES), F32), pltpu.VMEM((MOE_ROW_TILE * n, V7X_LANES), F32),
                        pltpu.SemaphoreType.DMA((2,)), pltpu.SemaphoreType.DMA(())],
        compiler_params=_cparams(("arbitrary",), 32),
        name="moe_dispatch",
    )(tail_slot.reshape(1, N_EXPERTS + 1), slots, slots, x, mod)


def _ffn_kernel(te_ref, nu_ref, x_ref, wg_ref, wu_ref, wd_ref, o_ref, wg_bf, wu_bf, wd_bf):
    i = pl.program_id(0)
    d, f = wg_bf.shape
    n = d // V7X_LANES
    tm = x_ref.shape[0] // n
    used = i < nu_ref[0]
    new_expert = jnp.logical_or(i == 0, te_ref[i] != te_ref[jnp.maximum(i - 1, 0)])

    @pl.when(jnp.logical_and(used, new_expert))
    def _():
        wg_bf[...] = wg_ref[0, 0].astype(BF16)
        wu_bf[...] = wu_ref[0, 0].astype(BF16)
        wd_bf[...] = wd_ref[0, 0].astype(BF16)

    @pl.when(used)
    def _():
        x = _from_records(x_ref, tm, n).astype(BF16)
        hg = _dot(x, wg_bf[...])
        hu = _dot(x, wu_bf[...])
        _to_records(o_ref, _dot((_silu(hg) * hu).astype(BF16), wd_bf[...]))

    @pl.when(jnp.logical_not(used))
    def _():
        o_ref[...] = jnp.zeros_like(o_ref)


def _ffn(xs, tile_expert, n_used, w_gate, w_up, w_down, *, layer):
    _, _, d, f = w_gate.shape
    n = d // V7X_LANES
    tm = MOE_ROW_TILE
    w_map = lambda i, te, nu: (layer, te[i], 0, 0)
    grid_spec = pltpu.PrefetchScalarGridSpec(
        num_scalar_prefetch=2,
        grid=(xs.shape[0] // (tm * n),),
        in_specs=[pl.BlockSpec((tm * n, V7X_LANES), lambda i, te, nu: (jnp.minimum(i, nu[0] - 1), 0)),
                  pl.BlockSpec((1, 1, d, f), w_map),
                  pl.BlockSpec((1, 1, d, f), w_map),
                  pl.BlockSpec((1, 1, f, d), w_map)],
        out_specs=pl.BlockSpec((tm * n, V7X_LANES), lambda i, te, nu: (i, 0)),
        scratch_shapes=[pltpu.VMEM((d, f), BF16), pltpu.VMEM((d, f), BF16), pltpu.VMEM((f, d), BF16)],
    )
    return pl.pallas_call(
        _ffn_kernel,
        out_shape=_sds(xs.shape, F32),
        grid_spec=grid_spec,
        compiler_params=_cparams(("arbitrary",), 48),
        name="moe_ffn",
    )(tile_expert, n_used, xs, w_gate, w_up, w_down)


def _combine_ln_kernel(p1_ref, p2_ref, p1_next_ref, p2_next_ref, xres_ref, route_ref, mod_ref, lng_ref, lnb_ref,
                       ye_ref, o_ref, buf, sem, *, gate_idx):
    tt, d = xres_ref.shape
    n = d // V7X_LANES
    i = pl.program_id(0)
    cur = lax.rem(i, 2)

    def fetch(pa_ref, pb_ref, slot):
        def body(blk, carry):
            for u in range(DMA_UNROLL):
                r = blk * DMA_UNROLL + u
                _record_copy(ye_ref, pa_ref[0, r], buf.at[slot, 0], r, sem.at[slot], n).start(priority=0)
                _record_copy(ye_ref, pb_ref[0, r], buf.at[slot, 1], r, sem.at[slot], n).start(priority=1)
            return carry

        lax.fori_loop(0, tt // DMA_UNROLL, body, 0)

    @pl.when(i == 0)
    def _():
        fetch(p1_ref, p2_ref, 0)

    @pl.when(i + 1 < pl.num_programs(0))
    def _():
        fetch(p1_next_ref, p2_next_ref, 1 - cur)

    def drain(blk, carry):
        for _ in range(2 * DMA_UNROLL):
            _record_copy(ye_ref, 0, buf.at[cur, 0], 0, sem.at[cur], n).wait()
        return carry

    lax.fori_loop(0, tt // DMA_UNROLL, drain, 0)
    m = mod_ref[0]
    rec = route_ref[...]
    y = (rec[:, 2:3] * _from_records(buf.at[cur, 0], tt, n) + rec[:, 3:4] * _from_records(buf.at[cur, 1], tt, n))
    v = DEEPNORM_ALPHA * xres_ref[...] + (1.0 + m[gate_idx:gate_idx + 1, :]) * y
    o_ref[...] = _layer_norm(v, lng_ref[...], lnb_ref[...])


def _combine_ln(xres, route, slots, ye, mod, ln_g, ln_b, *, seq_len):
    t, d = xres.shape
    n = d // V7X_LANES
    tt = TOKEN_TILE
    tiles_per_batch = seq_len // tt
    n_steps = t // tt
    const = lambda i: (0, 0)

    def next_slot_spec(copy):
        return pl.BlockSpec((None, None, 1, tt), lambda i: (jnp.minimum(i + 1, n_steps - 1), copy, 0, 0),
                            memory_space=pltpu.SMEM)

    return pl.pallas_call(
        functools.partial(_combine_ln_kernel, gate_idx=5),
        out_shape=_sds((t, d), F32),
        grid=(n_steps,),
        in_specs=[_slot_spec(tt, 0), _slot_spec(tt, 1), next_slot_spec(0), next_slot_spec(1),
                  pl.BlockSpec((tt, d), lambda i: (i, 0)),
                  pl.BlockSpec((tt, V7X_LANES), lambda i: (i, 0)),
                  pl.BlockSpec((1, 6, d), lambda i: (i // tiles_per_batch, 0, 0)),
                  pl.BlockSpec((1, d), const),
                  pl.BlockSpec((1, d), const),
                  pl.BlockSpec(memory_space=pl.ANY)],
        out_specs=pl.BlockSpec((tt, d), lambda i: (i, 0)),
        scratch_shapes=[pltpu.VMEM((2, 2, tt * n, V7X_LANES), F32), pltpu.SemaphoreType.DMA((2,))],
        compiler_params=_cparams(("arbitrary",), 32),
        name="moe_combine_ln",
    )(slots, slots, slots, slots, xres, route, mod, ln_g.reshape(1, d), ln_b.reshape(1, d), ye)


def _slot_kernel(route_ref, off_ref, o_ref):
    tt = o_ref.shape[-1]
    lane = lax.broadcasted_iota(jnp.int32, (tt, V7X_LANES), 1)
    lane_f = lane.astype(F32)
    for s in range(o_ref.shape[0]):
        rec = route_ref[s * tt:(s + 1) * tt, :]
        off1 = jnp.sum(jnp.where(lane_f == rec[:, 0:1], off_ref[...], 0.0), axis=-1, keepdims=True)
        off2 = jnp.sum(jnp.where(lane_f == rec[:, 1:2], off_ref[...], 0.0), axis=-1, keepdims=True)
        both = jnp.where(lane == 0, off1 + rec[:, 4:5], jnp.where(lane == 1, off2 + rec[:, 5:6], 0.0))
        both_t = both.T
        o_ref[s, 0] = both_t[0:1, :].astype(jnp.int32)
        o_ref[s, 1] = both_t[1:2, :].astype(jnp.int32)


def _slots(route, row_off):
    t = route.shape[0]
    tt = TOKEN_TILE
    tiles_per_step = 8
    assert t % (tt * tiles_per_step) == 0
    off = jnp.pad(row_off.astype(F32), (0, V7X_LANES - N_EXPERTS)).reshape(1, V7X_LANES)
    return pl.pallas_call(
        _slot_kernel,
        out_shape=_sds((t // tt, 2, 1, tt), jnp.int32),
        grid=(t // (tt * tiles_per_step),),
        in_specs=[pl.BlockSpec((tt * tiles_per_step, V7X_LANES), lambda i: (i, 0)),
                  pl.BlockSpec((1, V7X_LANES), lambda i: (0, 0))],
        out_specs=pl.BlockSpec((tiles_per_step, 2, 1, tt), lambda i: (i, 0, 0, 0)),
        compiler_params=_cparams(("arbitrary",), 32),
        name="moe_slots",
    )(route, off)


def _moe_tables(counts, n_tokens):
    tm = MOE_ROW_TILE
    max_tiles = (2 * n_tokens) // tm + N_EXPERTS
    cnt = counts[:, 0].astype(jnp.int32)
    tiles_e = (cnt + tm - 1) // tm
    tile_end = jnp.cumsum(tiles_e)
    row_off = (tile_end - tiles_e) * tm
    n_used = tile_end[-1:]
    tail_slot = jnp.concatenate([jnp.where(tiles_e > 0, (tile_end - 1) * tm, -1), n_used]).astype(jnp.int32)
    tile_ids = jnp.minimum(jnp.arange(max_tiles, dtype=jnp.int32), n_used - 1)
    tile_expert = jnp.sum(tile_ids[:, None] >= tile_end[None, :], axis=1).astype(jnp.int32)
    return row_off, tail_slot, tile_expert, n_used.astype(jnp.int32), max_tiles * tm


def _moe_layer(x1, route, counts, mod, w_gate, w_up, w_down, ln_g, ln_b, *, layer, seq_len):
    row_off, tail_slot, tile_expert, n_used, n_slots = _moe_tables(counts, route.shape[0])
    slots = _slots(route, row_off)
    xs = _dispatch(x1, mod, slots, tail_slot, n_slots, seq_len=seq_len)
    ye = _ffn(xs, tile_expert, n_used, w_gate, w_up, w_down, layer=layer)
    return _combine_ln(x1, route, slots, ye, mod, ln_g, ln_b, seq_len=seq_len)


def _qkv_body(x_ref, mod_ref, w_ref, o_ref, hm_ref, *, sc, sh, tiles_per_batch, tiles_per_kind):
    @pl.when(pl.program_id(1) == 0)
    def _():
        _modulate_into(hm_ref, x_ref, mod_ref, sc, sh)

    tm = x_ref.shape[0]
    kind = pl.program_id(1) // tiles_per_kind
    acc = _dot(hm_ref[...], w_ref[...])
    acc = acc * jnp.where(kind == 0, ATTN_HEAD_DIM ** -0.5, 1.0)
    lane = lax.broadcasted_iota(jnp.int32, (tm, V7X_LANES), 1)
    low = lane < ATTN_HEAD_DIM
    row = lax.broadcasted_iota(jnp.int32, (tm, V7X_LANES), 0)
    pos = lax.rem(pl.program_id(0), tiles_per_batch) * tm + row
    k_extra = (lane == ATTN_HEAD_DIM + lax.div(pos, MOBA_BLOCK)).astype(F32)
    v_extra = (lane == ATTN_HEAD_DIM).astype(F32)
    extra = jnp.where(kind == 1, k_extra, jnp.where(kind == 2, v_extra, 0.0))
    for p in range(acc.shape[1] // V7X_LANES):
        t2 = acc[:, p * V7X_LANES:(p + 1) * V7X_LANES]
        even = jnp.where(low, t2, extra)
        odd = jnp.where(low, pltpu.roll(t2, ATTN_HEAD_DIM, axis=1), extra)
        o_ref[0, 0, 2 * p] = even.astype(BF16)
        o_ref[0, 0, 2 * p + 1] = odd.astype(BF16)


def _qkv(x, mod, w, *, batch, seq_len, tm=1024, tn=512):
    t, k = x.shape
    n = w.shape[1]
    heads_per_tile = tn // ATTN_HEAD_DIM
    tiles_per_batch = seq_len // tm
    tiles_per_kind = ATTN_HEADS // heads_per_tile
    assert n == 3 * ATTN_HEADS * ATTN_HEAD_DIM and seq_len % tm == 0
    return pl.pallas_call(
        functools.partial(_qkv_body, sc=1, sh=0, tiles_per_batch=tiles_per_batch, tiles_per_kind=tiles_per_kind),
        out_shape=_sds((3, batch, ATTN_HEADS, seq_len, V7X_LANES), BF16),
        grid=(t // tm, n // tn),
        in_specs=[pl.BlockSpec((tm, k), lambda i, j: (i, 0)),
                  pl.BlockSpec((1, 6, k), lambda i, j: (i // tiles_per_batch, 0, 0)),
                  pl.BlockSpec((k, tn), lambda i, j: (0, j))],
        out_specs=pl.BlockSpec((1, 1, heads_per_tile, tm, V7X_LANES),
                               lambda i, j: (j // tiles_per_kind, i // tiles_per_batch, j % tiles_per_kind,
                                             i % tiles_per_batch, 0)),
        scratch_shapes=[pltpu.VMEM((tm, k), BF16)],
        compiler_params=_cparams(("arbitrary", "arbitrary"), 40),
        name="qkv_proj",
    )(x, mod, w)


ATTN_HEADS_PER_STEP = 2
ATTN_Q_TILE = 1024
GATE_ROWS = 512


def _attn_kernel(q_ref, k_ref, v_ref, pm_ref, o_ref, km_ref, qa_ref):
    tq = ATTN_Q_TILE
    seq_len = q_ref.shape[2]
    n_blocks = seq_len // MOBA_BLOCK
    heads = range(ATTN_HEADS_PER_STEP)
    qi = pl.program_id(2)

    @pl.when(qi == 0)
    def _():
        for hh in heads:
            km_ref[hh] = _dot(pm_ref[...], k_ref[0, hh]).astype(BF16)
        lane = lax.broadcasted_iota(jnp.int32, (GATE_ROWS, V7X_LANES), 1)
        lane_f = lane.astype(F32)
        blk = lane - ATTN_HEAD_DIM
        mask_lanes = (blk >= 0) & (blk < n_blocks)
        row = lax.broadcasted_iota(jnp.int32, (GATE_ROWS, V7X_LANES), 0)

        def gate_rows(ci, carry, *, select):
            r0 = pl.multiple_of(ci * GATE_ROWS, GATE_ROWS)
            past = (blk >= 0) & (blk < lax.div(r0 + row, MOBA_BLOCK))
            for hh in heads:
                q = q_ref[0, hh, pl.ds(r0, GATE_ROWS), :]
                chosen = past
                if select:
                    gate = jnp.where(past, _dot_nt(q, km_ref[hh]), -jnp.inf)
                    chosen = jnp.zeros((GATE_ROWS, V7X_LANES), jnp.bool_)
                    for _ in range(MOBA_TOPK):
                        mx = jnp.max(gate, axis=-1, keepdims=True)
                        idx = jnp.min(jnp.where(gate == mx, lane_f, float(V7X_LANES)), axis=-1, keepdims=True)
                        pick = lane_f == idx
                        chosen = chosen | pick
                        gate = jnp.where(pick, -jnp.inf, gate)
                own = blk == lax.div(r0 + row, MOBA_BLOCK)
                blocked = mask_lanes & jnp.logical_not((chosen & past) | own)
                qa_ref[hh, pl.ds(r0, GATE_ROWS), :] = (q.astype(F32) + jnp.where(blocked, NEG_INF, 0.0)).astype(BF16)
            return carry

        keep_all = min(seq_len, (MOBA_TOPK + 1) * MOBA_BLOCK) // GATE_ROWS
        lax.fori_loop(0, keep_all, functools.partial(gate_rows, select=False), 0)
        lax.fori_loop(keep_all, seq_len // GATE_ROWS, functools.partial(gate_rows, select=True), 0)

    q0 = pl.multiple_of(qi * tq, tq)

    def online_update(m_i, acc, s, v):
        m_new = jnp.maximum(m_i, jnp.max(s, axis=-1, keepdims=True))
        p = jnp.exp(s - m_new)
        return m_new, acc * jnp.exp(m_i - m_new) + _dot(p.astype(BF16), v)

    def kv_step(j, carry):
        start = pl.multiple_of(j * tq, tq)
        out = []
        for hh in heads:
            m_i, acc = carry[hh]
            s = _dot_nt(qa_ref[hh, pl.ds(q0, tq), :], k_ref[0, hh, pl.ds(start, tq), :])
            out.append(online_update(m_i, acc, s, v_ref[0, hh, pl.ds(start, tq), :]))
        return tuple(out)

    init = tuple((jnp.full((tq, 1), NEG_INF, F32), jnp.zeros((tq, V7X_LANES), F32)) for _ in heads)
    carry = lax.fori_loop(0, qi, kv_step, init)

    row = lax.broadcasted_iota(jnp.int32, (tq, tq), 0)
    col = lax.broadcasted_iota(jnp.int32, (tq, tq), 1)
    outs = []
    for hh in heads:
        m_i, acc = carry[hh]
        s = _dot_nt(qa_ref[hh, pl.ds(q0, tq), :], k_ref[0, hh, pl.ds(q0, tq), :])
        _, acc = online_update(m_i, acc, jnp.where(col <= row, s, NEG_INF), v_ref[0, hh, pl.ds(q0, tq), :])
        outs.append(acc / acc[:, ATTN_HEAD_DIM:ATTN_HEAD_DIM + 1])
    lane = lax.broadcasted_iota(jnp.int32, (tq, V7X_LANES), 1)
    for pair in range(ATTN_HEADS_PER_STEP // 2):
        both = jnp.where(lane < ATTN_HEAD_DIM, outs[2 * pair], pltpu.roll(outs[2 * pair + 1], ATTN_HEAD_DIM, axis=1))
        o_ref[0, :, pair * V7X_LANES:(pair + 1) * V7X_LANES] = both.astype(BF16)


def _attention(qkv, *, batch, seq_len):
    nb = seq_len // MOBA_BLOCK
    tq = ATTN_Q_TILE
    hps = ATTN_HEADS_PER_STEP
    assert hps % 2 == 0 and 2 * ATTN_HEAD_DIM == V7X_LANES and nb <= V7X_LANES - ATTN_HEAD_DIM
    assert seq_len % GATE_ROWS == 0 and seq_len % tq == 0 and tq % MOBA_BLOCK == 0
    rows = jnp.arange(V7X_LANES, dtype=jnp.int32)[:, None] - ATTN_HEAD_DIM
    cols = jnp.arange(seq_len, dtype=jnp.int32)[None, :] // MOBA_BLOCK
    pool = jnp.where(rows == cols, 1.0 / MOBA_BLOCK, 0.0).astype(BF16)
    def head_spec(kind):
        return pl.BlockSpec((None, 1, hps, seq_len, V7X_LANES), lambda b, h, i: (kind, b, h, 0, 0))

    return pl.pallas_call(
        _attn_kernel,
        out_shape=_sds((batch, seq_len, ATTN_HEADS * ATTN_HEAD_DIM), BF16),
        grid=(batch, ATTN_HEADS // hps, seq_len // tq),
        in_specs=[head_spec(0), head_spec(1), head_spec(2),
                  pl.BlockSpec((V7X_LANES, seq_len), lambda b, h, i: (0, 0))],
        out_specs=pl.BlockSpec((1, tq, hps * ATTN_HEAD_DIM), lambda b, h, i: (b, i, h)),
        scratch_shapes=[pltpu.VMEM((hps, V7X_LANES, V7X_LANES), BF16),
                        pltpu.VMEM((hps, seq_len, V7X_LANES), BF16)],
        compiler_params=_cparams(("arbitrary", "arbitrary", "arbitrary"), 48),
        name="moba_attention",
    )(qkv, qkv, qkv, pool)


def kernel(x, c, w_ada, b_ada, ln_g, ln_b, ssm_w_in, ssm_conv_w, ssm_conv_b, ssm_dt_bias, ssm_a_log, ssm_d,
           ssm_norm_w, ssm_w_out, attn_w_qkv, attn_w_o, w_router, b_router, moe_w_gate, moe_w_up, moe_w_down):
    batch, seq_len, d = x.shape
    assert seq_len % 512 == 0 and seq_len % SSM_CHUNK == 0 and seq_len % MOBA_BLOCK == 0
    t = batch * seq_len
    mod = _ada_mod(c, w_ada, b_ada)
    xf = x.reshape(t, d)

    d_inner = SSM_GROUPS * GROUP_W
    conv_dim = d_inner + 2 * SSM_GROUPS * SSM_STATE
    w_in = ssm_w_in[0]
    heads = ssm_dt_bias.shape[1]
    w_main = w_in[:, :d_inner + conv_dim].astype(BF16)
    w_dt = jnp.pad(w_in[:, d_inner + conv_dim:], ((0, 0), (0, V7X_LANES - heads))).astype(BF16)
    proj = _mm_mod(xf, mod[0], w_main, sc=1, sh=0, seq_len=seq_len, out_dtype=BF16)
    dt_raw = _mm_mod(xf, mod[0], w_dt, sc=1, sh=0, seq_len=seq_len, out_dtype=F32)
    y = _ssd(proj, dt_raw, ssm_conv_w[0], ssm_conv_b[0], ssm_dt_bias[0], ssm_a_log[0], ssm_d[0], ssm_norm_w[0],
             batch=batch, seq_len=seq_len)
    x1, route, counts = _proj_ln_route(y, ssm_w_out[0].astype(BF16), xf, mod[0], ln_g[0, 0], ln_b[0, 0],
                                       w_router, b_router, seq_len=seq_len)
    xf = _moe_layer(x1, route, counts, mod[0], moe_w_gate, moe_w_up, moe_w_down, ln_g[0, 1], ln_b[0, 1],
                    layer=0, seq_len=seq_len)

    w_qkv = attn_w_qkv[0].astype(BF16)
    qkv = _qkv(xf, mod[1], w_qkv, batch=batch, seq_len=seq_len)
    o = _attention(qkv, batch=batch, seq_len=seq_len).reshape(t, d)
    x1, route, counts = _proj_ln_route(o, attn_w_o[0].astype(BF16), xf, mod[1], ln_g[1, 0], ln_b[1, 0],
                                       w_router, b_router, seq_len=seq_len)
    xf = _moe_layer(x1, route, counts, mod[1], moe_w_gate, moe_w_up, moe_w_down, ln_g[1, 1], ln_b[1, 1],
                    layer=1, seq_len=seq_len)
    return xf.reshape(batch, seq_len, d)
```

```python
import functools

import jax
import jax.numpy as jnp
from jax import lax
from jax.experimental import pallas as pl
from jax.experimental.pallas import tpu as pltpu

F32 = jnp.float32
BF16 = jnp.bfloat16

V7X_LANES = 128
V7X_SUBLANES = 8
V7X_VMEM_BYTES = 64 * 1024 * 1024

DEPTH = 2
SSM_HEAD_DIM = 64
SSM_STATE = 128
SSM_GROUPS = 8
SSM_HEADS_PER_GROUP = 4
SSM_CONV = 4
SSM_CHUNK = 256
ATTN_HEADS = 16
ATTN_HEAD_DIM = 64
MOBA_BLOCK = 256
MOBA_TOPK = 3
N_EXPERTS = 32
N_EXPERT_GROUPS = 4
EXPERTS_PER_GROUP = N_EXPERTS // N_EXPERT_GROUPS
DEEPNORM_ALPHA = (2.0 * DEPTH) ** 0.25
LN_EPS = 1e-5
RMS_EPS = 1e-5
NEG_INF = -1e30
LOG2_E = 1.4426950408889634

GROUP_W = SSM_HEADS_PER_GROUP * SSM_HEAD_DIM
CONV_HALO = V7X_SUBLANES
MOE_ROW_TILE = 512
TOKEN_TILE = 256
DMA_UNROLL = 8


def _cparams(semantics, vmem_mib):
    assert vmem_mib * 1024 * 1024 <= V7X_VMEM_BYTES
    return pltpu.CompilerParams(dimension_semantics=semantics, vmem_limit_bytes=vmem_mib * 1024 * 1024)


def _sds(shape, dtype):
    return jax.ShapeDtypeStruct(shape, dtype)


def _dot(a, b):
    return jnp.dot(a, b, preferred_element_type=F32)


def _dot_nt(a, b):
    return lax.dot_general(a, b, (((1,), (1,)), ((), ())), preferred_element_type=F32)


def _dot_tn(a, b):
    return lax.dot_general(a, b, (((0,), (0,)), ((), ())), preferred_element_type=F32)


def _split3(a):
    hi = a.astype(BF16)
    r1 = a - hi.astype(F32)
    mid = r1.astype(BF16)
    lo = (r1 - mid.astype(F32)).astype(BF16)
    return hi, mid, lo


def _silu(x):
    return x * jax.nn.sigmoid(x)


def _layer_norm(v, gamma, beta):
    mu = jnp.mean(v, axis=-1, keepdims=True)
    d = v - mu
    var = jnp.mean(d * d, axis=-1, keepdims=True)
    return d * lax.rsqrt(var + LN_EPS) * gamma + beta


def _ada_kernel(c_ref, w_ref, b_ref, o_ref):
    cs = _silu(c_ref[...])
    o_ref[0] = jnp.dot(cs, w_ref[0], preferred_element_type=F32, precision=lax.Precision.HIGHEST) + b_ref[0]


def _ada_mod(c, w_ada, b_ada):
    depth, d, n = w_ada.shape
    b = c.shape[0]
    tn = 1024
    out = pl.pallas_call(
        _ada_kernel,
        out_shape=_sds((depth, b, n), F32),
        grid=(depth, n // tn),
        in_specs=[pl.BlockSpec((b, d), lambda l, j: (0, 0)),
                  pl.BlockSpec((1, d, tn), lambda l, j: (l, 0, j)),
                  pl.BlockSpec((1, 1, tn), lambda l, j: (l, 0, j))],
        out_specs=pl.BlockSpec((1, b, tn), lambda l, j: (l, 0, j)),
        compiler_params=_cparams(("arbitrary", "arbitrary"), 32),
        name="ada_mod",
    )(c, w_ada, b_ada.reshape(depth, 1, n))
    return out.reshape(depth, b, 6, d)


def _modulate_into(hm_ref, x_ref, mod_ref, sc, sh):
    m = mod_ref[0]
    hm_ref[...] = (x_ref[...] * (1.0 + m[sc:sc + 1, :]) + m[sh:sh + 1, :]).astype(BF16)


def _mm_mod_kernel(x_ref, mod_ref, w_ref, o_ref, hm_ref, *, sc, sh):
    @pl.when(pl.program_id(1) == 0)
    def _():
        _modulate_into(hm_ref, x_ref, mod_ref, sc, sh)

    o_ref[...] = _dot(hm_ref[...], w_ref[...]).astype(o_ref.dtype)


def _mm_mod(x, mod, w, *, sc, sh, seq_len, out_dtype, tm=2048, tn=1536):
    t, k = x.shape
    n = w.shape[1]
    tn = min(tn, n)
    assert seq_len % tm == 0 and n % tn == 0
    tiles_per_batch = seq_len // tm
    return pl.pallas_call(
        functools.partial(_mm_mod_kernel, sc=sc, sh=sh),
        out_shape=_sds((t, n), out_dtype),
        grid=(t // tm, n // tn),
        in_specs=[pl.BlockSpec((tm, k), lambda i, j: (i, 0)),
                  pl.BlockSpec((1, 6, k), lambda i, j: (i // tiles_per_batch, 0, 0)),
                  pl.BlockSpec((k, tn), lambda i, j: (0, j))],
        out_specs=pl.BlockSpec((tm, tn), lambda i, j: (i, j)),
        scratch_shapes=[pltpu.VMEM((tm, k), BF16)],
        compiler_params=_cparams(("arbitrary", "arbitrary"), 48),
        name="mm_mod",
    )(x, mod, w)


def _ssd_kernel(z_ref, xp_ref, bcp_ref, dtr_ref, cw_ref, cb_ref, dtb_ref, alog_ref, dsk_ref, nw_ref,
                o_ref, ubuf, act, state, acum_t):
    lc = SSM_CHUNK
    d_inner = xp_ref.shape[1]
    n_slabs = ubuf.shape[0]
    x_slabs = d_inner // V7X_LANES
    chunk = pl.program_id(1)

    @pl.when(chunk == 0)
    def _():
        ubuf[:, 0:CONV_HALO, :] = jnp.zeros((n_slabs, CONV_HALO, V7X_LANES), F32)
        state[...] = jnp.zeros_like(state)

    def conv_slabs(src_ref, first_slab, count):
        def slab_body(j, carry):
            src_off = pl.multiple_of(j * V7X_LANES, V7X_LANES)
            slab = first_slab + j
            off = pl.multiple_of(slab * V7X_LANES, V7X_LANES)
            u = src_ref[:, pl.ds(src_off, V7X_LANES)].astype(F32)
            ubuf[slab, CONV_HALO:CONV_HALO + lc, :] = u
            acc = cb_ref[:, pl.ds(off, V7X_LANES)] + cw_ref[SSM_CONV - 1:SSM_CONV, pl.ds(off, V7X_LANES)] * u
            for k in range(SSM_CONV - 1):
                tap = CONV_HALO - (SSM_CONV - 1) + k
                acc = acc + cw_ref[k:k + 1, pl.ds(off, V7X_LANES)] * ubuf[slab, tap:tap + lc, :]
            act[:, pl.ds(off, V7X_LANES)] = _silu(acc)
            ubuf[slab, 0:CONV_HALO, :] = u[lc - CONV_HALO:lc, :]
            return carry

        lax.fori_loop(0, count, slab_body, 0)

    conv_slabs(xp_ref, 0, x_slabs)
    conv_slabs(bcp_ref, x_slabs, n_slabs - x_slabs)

    dt = jax.nn.softplus(dtr_ref[...] + dtb_ref[...])
    da = dt * (-LOG2_E * jnp.exp(alog_ref[...]))
    row = lax.broadcasted_iota(jnp.int32, (lc, lc), 0)
    col = lax.broadcasted_iota(jnp.int32, (lc, lc), 1)
    causal = col <= row
    tril = causal.astype(BF16)
    acum = sum(_dot(tril, part) for part in _split3(da))
    acum_t[...] = acum.T
    acum_parts = _split3(acum)
    dt_bf16 = dt.astype(BF16)
    head_of_col = lax.broadcasted_iota(jnp.int32, (lc, GROUP_W), 1) // SSM_HEAD_DIM
    sel_row = lax.broadcasted_iota(jnp.int32, (V7X_LANES, GROUP_W), 0)
    sel_col = lax.broadcasted_iota(jnp.int32, (V7X_LANES, GROUP_W), 1) // SSM_HEAD_DIM

    def group_body(g, carry):
        xo = pl.multiple_of(g * GROUP_W, GROUP_W)
        bo = pl.multiple_of(d_inner + g * SSM_STATE, SSM_STATE)
        co = pl.multiple_of(d_inner + SSM_GROUPS * SSM_STATE + g * SSM_STATE, SSM_STATE)
        x_g = act[:, pl.ds(xo, GROUP_W)]
        b_g = act[:, pl.ds(bo, SSM_STATE)].astype(BF16)
        c_g = act[:, pl.ds(co, SSM_STATE)].astype(BF16)
        sel = (sel_row == SSM_HEADS_PER_GROUP * g + sel_col).astype(BF16)
        ab = sum(_dot(part, sel) for part in acum_parts)
        dtb = _dot(dt_bf16, sel)
        alast = ab[lc - 1:lc, :]
        cb = _dot_nt(c_g, b_g)
        xdt = x_g * dtb
        y = jnp.zeros((lc, GROUP_W), F32)
        for r in range(SSM_HEADS_PER_GROUP):
            acol = ab[:, r * SSM_HEAD_DIM:r * SSM_HEAD_DIM + 1]
            arow = acum_t[pl.ds(SSM_HEADS_PER_GROUP * g + r, 1), :]
            decay = jnp.exp2(jnp.where(causal, acol - arow, -jnp.inf))
            w = (cb * decay).astype(BF16)
            x_r = jnp.where(head_of_col == r, xdt, 0.0).astype(BF16)
            y = y + _dot(w, x_r)
        st = state[:, pl.ds(xo, GROUP_W)]
        y = y + _dot(c_g, st.astype(BF16)) * jnp.exp2(ab)
        to_end = jnp.exp2(alast - ab) * dtb
        xw = (x_g * to_end).astype(BF16)
        state[:, pl.ds(xo, GROUP_W)] = st * jnp.exp2(alast) + _dot_tn(b_g, xw)
        y = y + dsk_ref[:, pl.ds(xo, GROUP_W)] * x_g
        gated = y * _silu(z_ref[:, pl.ds(xo, GROUP_W)].astype(F32))
        ms = jnp.mean(gated * gated, axis=-1, keepdims=True)
        o_ref[:, pl.ds(xo, GROUP_W)] = (gated * lax.rsqrt(ms + RMS_EPS) * nw_ref[:, pl.ds(xo, GROUP_W)]).astype(BF16)
        return carry

    lax.fori_loop(0, SSM_GROUPS, group_body, 0, unroll=4)


def _ssd(proj, dt_raw, conv_w, conv_b, dt_bias, a_log, d_skip, norm_w, *, batch, seq_len):
    t = proj.shape[0]
    d_inner = SSM_GROUPS * GROUP_W
    conv_dim = d_inner + 2 * SSM_GROUPS * SSM_STATE
    assert proj.shape[1] == d_inner + conv_dim and conv_dim == 2 * d_inner
    nc = seq_len // SSM_CHUNK
    lc = SSM_CHUNK
    pad = V7X_LANES - dt_bias.shape[0]
    heads = dt_bias.shape[0]
    row_map = lambda b, c: b * nc + c
    const = lambda b, c: (0, 0)
    return pl.pallas_call(
        _ssd_kernel,
        out_shape=_sds((t, d_inner), BF16),
        grid=(batch, nc),
        in_specs=[pl.BlockSpec((lc, d_inner), lambda b, c: (row_map(b, c), 0)),
                  pl.BlockSpec((lc, d_inner), lambda b, c: (row_map(b, c), 1)),
                  pl.BlockSpec((lc, d_inner), lambda b, c: (row_map(b, c), 2)),
                  pl.BlockSpec((lc, V7X_LANES), lambda b, c: (row_map(b, c), 0)),
                  pl.BlockSpec((SSM_CONV, conv_dim), const),
                  pl.BlockSpec((1, conv_dim), const),
                  pl.BlockSpec((1, V7X_LANES), const),
                  pl.BlockSpec((1, V7X_LANES), const),
                  pl.BlockSpec((1, d_inner), const),
                  pl.BlockSpec((1, d_inner), const)],
        out_specs=pl.BlockSpec((lc, d_inner), lambda b, c: (row_map(b, c), 0)),
        scratch_shapes=[pltpu.VMEM((conv_dim // V7X_LANES, CONV_HALO + lc, V7X_LANES), F32),
                        pltpu.VMEM((lc, conv_dim), F32),
                        pltpu.VMEM((SSM_STATE, d_inner), F32),
                        pltpu.VMEM((V7X_LANES, lc), F32)],
        compiler_params=_cparams(("arbitrary", "arbitrary"), 48),
        name="ssd_chunk_scan",
    )(proj, proj, proj, dt_raw, conv_w, conv_b.reshape(1, conv_dim),
      jnp.pad(dt_bias, (0, pad)).reshape(1, V7X_LANES), jnp.pad(a_log, (0, pad)).reshape(1, V7X_LANES),
      jnp.repeat(d_skip, SSM_HEAD_DIM).reshape(1, heads * SSM_HEAD_DIM), norm_w.reshape(1, d_inner))


def _route(logits_t, carry_ref):
    tm = logits_t.shape[1]
    lg = logits_t[0:N_EXPERTS, :]
    eid = lax.broadcasted_iota(jnp.int32, (N_EXPERTS, tm), 0).astype(F32)
    e = jnp.exp(lg - jnp.max(lg, axis=0, keepdims=True))
    far = float(N_EXPERTS)
    best = None
    for g in range(N_EXPERT_GROUPS):
        eg = e[g * EXPERTS_PER_GROUP:(g + 1) * EXPERTS_PER_GROUP, :]
        ig = (lax.broadcasted_iota(jnp.int32, (EXPERTS_PER_GROUP, tm), 0) + g * EXPERTS_PER_GROUP).astype(F32)
        m1 = jnp.max(eg, axis=0, keepdims=True)
        i1 = jnp.min(jnp.where(eg == m1, ig, far), axis=0, keepdims=True)
        eg2 = jnp.where(ig == i1, -1.0, eg)
        m2 = jnp.max(eg2, axis=0, keepdims=True)
        i2 = jnp.min(jnp.where(eg2 == m2, ig, far), axis=0, keepdims=True)
        cand = (m1 + m2, m1, m2, i1, i2)
        if best is None:
            best = cand
        else:
            take = cand[0] > best[0]
            best = tuple(jnp.where(take, c, b) for c, b in zip(cand, best))
    _, m1, m2, i1, i2 = best
    denom = m1 + m2
    hit1 = eid == i1
    hit2 = eid == i2
    onehot = (hit1 | hit2).astype(F32)
    row = lax.broadcasted_iota(jnp.int32, (tm, tm), 0)
    col = lax.broadcasted_iota(jnp.int32, (tm, tm), 1)
    earlier = (row < col).astype(BF16)
    rank = _dot(onehot.astype(BF16), earlier) + carry_ref[:, 0:1]
    r1 = jnp.sum(jnp.where(hit1, rank, 0.0), axis=0, keepdims=True)
    r2 = jnp.sum(jnp.where(hit2, rank, 0.0), axis=0, keepdims=True)
    carry_ref[...] = carry_ref[...] + jnp.sum(onehot, axis=1, keepdims=True)
    sub = lax.broadcasted_iota(jnp.int32, (V7X_SUBLANES, tm), 0)
    rec_t = jnp.zeros((V7X_SUBLANES, tm), F32)
    for k, val in enumerate((i1, i2, m1 / denom, m2 / denom, r1, r2)):
        rec_t = jnp.where(sub == k, val, rec_t)
    rec_t = jnp.concatenate([rec_t, jnp.zeros((V7X_LANES - V7X_SUBLANES, tm), F32)], axis=0)
    return rec_t.T


def _proj_ln_route_kernel(a_ref, w_ref, xres_ref, mod_ref, lng_ref, lnb_ref, wr_ref, br_ref,
                          x_ref, route_ref, cnt_ref, carry_ref, *, gate_idx, sc, sh):
    @pl.when(pl.program_id(0) == 0)
    def _():
        carry_ref[...] = jnp.zeros_like(carry_ref)

    m = mod_ref[0]
    y = _dot(a_ref[...], w_ref[...])
    v = DEEPNORM_ALPHA * xres_ref[...] + (1.0 + m[gate_idx:gate_idx + 1, :]) * y
    x1 = _layer_norm(v, lng_ref[...], lnb_ref[...])
    x_ref[...] = x1
    hm = (x1 * (1.0 + m[sc:sc + 1, :]) + m[sh:sh + 1, :]).astype(BF16)
    logits_t = _dot_nt(wr_ref[...], hm) + br_ref[...]
    route_ref[...] = _route(logits_t, carry_ref)
    cnt_ref[...] = carry_ref[...]


def _proj_ln_route(a, w, xres, mod, ln_g, ln_b, w_router, b_router, *, seq_len):
    t, k = a.shape
    d = w.shape[1]
    tm = TOKEN_TILE
    tiles_per_batch = seq_len // tm
    const = lambda i: (0, 0)
    wr = jnp.pad(w_router.T, ((0, V7X_LANES - N_EXPERTS), (0, 0))).astype(BF16)
    br = jnp.pad(b_router, (0, V7X_LANES - N_EXPERTS)).reshape(V7X_LANES, 1)
    return pl.pallas_call(
        functools.partial(_proj_ln_route_kernel, gate_idx=2, sc=4, sh=3),
        out_shape=(_sds((t, d), F32), _sds((t, V7X_LANES), F32), _sds((N_EXPERTS, V7X_LANES), F32)),
        grid=(t // tm,),
        in_specs=[pl.BlockSpec((tm, k), lambda i: (i, 0)),
                  pl.BlockSpec((k, d), const),
                  pl.BlockSpec((tm, d), lambda i: (i, 0)),
                  pl.BlockSpec((1, 6, d), lambda i: (i // tiles_per_batch, 0, 0)),
                  pl.BlockSpec((1, d), const),
                  pl.BlockSpec((1, d), const),
                  pl.BlockSpec((V7X_LANES, d), const),
                  pl.BlockSpec((V7X_LANES, 1), const)],
        out_specs=(pl.BlockSpec((tm, d), lambda i: (i, 0)),
                   pl.BlockSpec((tm, V7X_LANES), lambda i: (i, 0)),
                   pl.BlockSpec((N_EXPERTS, V7X_LANES), const)),
        scratch_shapes=[pltpu.VMEM((N_EXPERTS, V7X_LANES), F32)],
        compiler_params=_cparams(("arbitrary",), 40),
        name="proj_ln_route",
    )(a, w, xres, mod, ln_g.reshape(1, d), ln_b.reshape(1, d), wr, br)


def _record_copy(src_ref, src_slot, dst_ref, dst_slot, sem, n):
    src = src_ref.at[pl.ds(pl.multiple_of(src_slot * n, n), n)]
    dst = dst_ref.at[pl.ds(pl.multiple_of(dst_slot * n, n), n)]
    return pltpu.make_async_copy(src, dst, sem)


def _to_records(rec_ref, rows):
    t = rows.shape[0]
    n = rows.shape[1] // V7X_LANES
    for s in range(n):
        rec_ref[pl.ds(s, t, stride=n), :] = rows[:, s * V7X_LANES:(s + 1) * V7X_LANES]


def _from_records(rec_ref, t, n):
    return jnp.concatenate([rec_ref[pl.ds(s, t, stride=n), :] for s in range(n)], axis=1)


def _dispatch_kernel(tail_ref, p1_ref, p2_ref, x_ref, mod_ref, xs_ref, hm_ref, zero_ref, sem, zero_sem, *, sc, sh):
    tt, d = x_ref.shape
    n = d // V7X_LANES
    i = pl.program_id(0)
    n_steps = pl.num_programs(0)
    cur = lax.rem(i, 2)

    @pl.when(i == 0)
    def _():
        zero_ref[...] = jnp.zeros_like(zero_ref)
        rows = zero_ref.shape[0]

        def tail_copy(e):
            start = pl.multiple_of(jnp.maximum(tail_ref[0, e], 0) * n, rows)
            return pltpu.make_async_copy(zero_ref, xs_ref.at[pl.ds(start, rows)], zero_sem)

        def fill(e, carry):
            @pl.when(tail_ref[0, e] >= 0)
            def _():
                tail_copy(e).start()
            return carry

        def fill_wait(e, carry):
            @pl.when(tail_ref[0, e] >= 0)
            def _():
                tail_copy(e).wait()
            return carry

        def spare_copy(j):
            return pltpu.make_async_copy(zero_ref, xs_ref.at[pl.ds(pl.multiple_of(j * rows, rows), rows)], zero_sem)

        def spare_fill(j, carry):
            spare_copy(j).start()
            return carry

        def spare_wait(j, carry):
            spare_copy(j).wait()
            return carry

        n_tiles = xs_ref.shape[0] // rows
        lax.fori_loop(0, N_EXPERTS, fill, 0)
        lax.fori_loop(tail_ref[0, N_EXPERTS], n_tiles, spare_fill, 0)
        lax.fori_loop(0, N_EXPERTS, fill_wait, 0)
        lax.fori_loop(tail_ref[0, N_EXPERTS], n_tiles, spare_wait, 0)

    def drain(buf):
        def body(blk, carry):
            for _ in range(2 * DMA_UNROLL):
                _record_copy(hm_ref.at[buf], 0, xs_ref, 0, sem.at[buf], n).wait()
            return carry

        lax.fori_loop(0, tt // DMA_UNROLL, body, 0)

    @pl.when(i >= 2)
    def _():
        drain(cur)

    m = mod_ref[0]
    _to_records(hm_ref.at[cur], x_ref[...] * (1.0 + m[sc:sc + 1, :]) + m[sh:sh + 1, :])

    def issue(blk, carry):
        for u in range(DMA_UNROLL):
            r = blk * DMA_UNROLL + u
            _record_copy(hm_ref.at[cur], r, xs_ref, p1_ref[0, r], sem.at[cur], n).start(priority=0)
            _record_copy(hm_ref.at[cur], r, xs_ref, p2_ref[0, r], sem.at[cur], n).start(priority=1)
        return carry

    lax.fori_loop(0, tt // DMA_UNROLL, issue, 0)

    @pl.when(i == n_steps - 1)
    def _():
        drain(cur)

    @pl.when(jnp.logical_and(i == n_steps - 1, i >= 1))
    def _():
        drain(1 - cur)


def _slot_spec(tt, copy):
    return pl.BlockSpec((None, None, 1, tt), lambda i: (i, copy, 0, 0), memory_space=pltpu.SMEM)


def _dispatch(x, mod, slots, tail_slot, n_slots, *, seq_len):
    t, d = x.shape
    n = d // V7X_LANES
    tt = TOKEN_TILE
    tiles_per_batch = seq_len // tt
    return pl.pallas_call(
        functools.partial(_dispatch_kernel, sc=4, sh=3),
        out_shape=_sds((n_slots * n, V7X_LANES), F32),
        grid=(t // tt,),
        in_specs=[pl.BlockSpec((1, N_EXPERTS + 1), lambda i: (0, 0), memory_space=pltpu.SMEM),
                  _slot_spec(tt, 0), _slot_spec(tt, 1),
                  pl.BlockSpec((tt, d), lambda i: (i, 0)),
                  pl.BlockSpec((1, 6, d), lambda i: (i // tiles_per_batch, 0, 0))],
        out_specs=pl.BlockSpec(memory_space=pl.ANY),
        scratch_shapes=[pltpu.VMEM((2, tt * n, V7X_LANES), F32), pltpu.VMEM((MOE_ROW_TILE * n, V7X_LANES), F32),
                        pltpu.SemaphoreType.DMA((2,)), pltpu.SemaphoreType.DMA(())],
        compiler_params=_cparams(("arbitrary",), 32),
        name="moe_dispatch",
    )(tail_slot.reshape(1, N_EXPERTS + 1), slots, slots, x, mod)


def _ffn_kernel(te_ref, nu_ref, x_ref, wg_ref, wu_ref, wd_ref, o_ref, wg_bf, wu_bf, wd_bf):
    i = pl.program_id(0)
    d, f = wg_bf.shape
    n = d // V7X_LANES
    tm = x_ref.shape[0] // n
    used = i < nu_ref[0]
    new_expert = jnp.logical_or(i == 0, te_ref[i] != te_ref[jnp.maximum(i - 1, 0)])

    @pl.when(jnp.logical_and(used, new_expert))
    def _():
        wg_bf[...] = wg_ref[0, 0].astype(BF16)
        wu_bf[...] = wu_ref[0, 0].astype(BF16)
        wd_bf[...] = wd_ref[0, 0].astype(BF16)

    @pl.when(used)
    def _():
        x = _from_records(x_ref, tm, n).astype(BF16)
        hg = _dot(x, wg_bf[...])
        hu = _dot(x, wu_bf[...])
        _to_records(o_ref, _dot((_silu(hg) * hu).astype(BF16), wd_bf[...]))

    @pl.when(jnp.logical_not(used))
    def _():
        o_ref[...] = jnp.zeros_like(o_ref)


def _ffn(xs, tile_expert, n_used, w_gate, w_up, w_down, *, layer):
    _, _, d, f = w_gate.shape
    n = d // V7X_LANES
    tm = MOE_ROW_TILE
    w_map = lambda i, te, nu: (layer, te[i], 0, 0)
    grid_spec = pltpu.PrefetchScalarGridSpec(
        num_scalar_prefetch=2,
        grid=(xs.shape[0] // (tm * n),),
        in_specs=[pl.BlockSpec((tm * n, V7X_LANES), lambda i, te, nu: (jnp.minimum(i, nu[0] - 1), 0)),
                  pl.BlockSpec((1, 1, d, f), w_map),
                  pl.BlockSpec((1, 1, d, f), w_map),
                  pl.BlockSpec((1, 1, f, d), w_map)],
        out_specs=pl.BlockSpec((tm * n, V7X_LANES), lambda i, te, nu: (i, 0)),
        scratch_shapes=[pltpu.VMEM((d, f), BF16), pltpu.VMEM((d, f), BF16), pltpu.VMEM((f, d), BF16)],
    )
    return pl.pallas_call(
        _ffn_kernel,
        out_shape=_sds(xs.shape, F32),
        grid_spec=grid_spec,
        compiler_params=_cparams(("arbitrary",), 48),
        name="moe_ffn",
    )(tile_expert, n_used, xs, w_gate, w_up, w_down)


def _combine_ln_kernel(p1_ref, p2_ref, p1_next_ref, p2_next_ref, xres_ref, route_ref, mod_ref, lng_ref, lnb_ref,
                       ye_ref, o_ref, buf, sem, *, gate_idx):
    tt, d = xres_ref.shape
    n = d // V7X_LANES
    i = pl.program_id(0)
    cur = lax.rem(i, 2)

    def fetch(pa_ref, pb_ref, slot):
        def body(blk, carry):
            for u in range(DMA_UNROLL):
                r = blk * DMA_UNROLL + u
                _record_copy(ye_ref, pa_ref[0, r], buf.at[slot, 0], r, sem.at[slot], n).start(priority=0)
                _record_copy(ye_ref, pb_ref[0, r], buf.at[slot, 1], r, sem.at[slot], n).start(priority=1)
            return carry

        lax.fori_loop(0, tt // DMA_UNROLL, body, 0)

    @pl.when(i == 0)
    def _():
        fetch(p1_ref, p2_ref, 0)

    @pl.when(i + 1 < pl.num_programs(0))
    def _():
        fetch(p1_next_ref, p2_next_ref, 1 - cur)

    def drain(blk, carry):
        for _ in range(2 * DMA_UNROLL):
            _record_copy(ye_ref, 0, buf.at[cur, 0], 0, sem.at[cur], n).wait()
        return carry

    lax.fori_loop(0, tt // DMA_UNROLL, drain, 0)
    m = mod_ref[0]
    rec = route_ref[...]
    y = (rec[:, 2:3] * _from_records(buf.at[cur, 0], tt, n) + rec[:, 3:4] * _from_records(buf.at[cur, 1], tt, n))
    v = DEEPNORM_ALPHA * xres_ref[...] + (1.0 + m[gate_idx:gate_idx + 1, :]) * y
    o_ref[...] = _layer_norm(v, lng_ref[...], lnb_ref[...])


def _combine_ln(xres, route, slots, ye, mod, ln_g, ln_b, *, seq_len):
    t, d = xres.shape
    n = d // V7X_LANES
    tt = TOKEN_TILE
    tiles_per_batch = seq_len // tt
    n_steps = t // tt
    const = lambda i: (0, 0)

    def next_slot_spec(copy):
        return pl.BlockSpec((None, None, 1, tt), lambda i: (jnp.minimum(i + 1, n_steps - 1), copy, 0, 0),
                            memory_space=pltpu.SMEM)

    return pl.pallas_call(
        functools.partial(_combine_ln_kernel, gate_idx=5),
        out_shape=_sds((t, d), F32),
        grid=(n_steps,),
        in_specs=[_slot_spec(tt, 0), _slot_spec(tt, 1), next_slot_spec(0), next_slot_spec(1),
                  pl.BlockSpec((tt, d), lambda i: (i, 0)),
                  pl.BlockSpec((tt, V7X_LANES), lambda i: (i, 0)),
                  pl.BlockSpec((1, 6, d), lambda i: (i // tiles_per_batch, 0, 0)),
                  pl.BlockSpec((1, d), const),
                  pl.BlockSpec((1, d), const),
                  pl.BlockSpec(memory_space=pl.ANY)],
        out_specs=pl.BlockSpec((tt, d), lambda i: (i, 0)),
        scratch_shapes=[pltpu.VMEM((2, 2, tt * n, V7X_LANES), F32), pltpu.SemaphoreType.DMA((2,))],
        compiler_params=_cparams(("arbitrary",), 32),
        name="moe_combine_ln",
    )(slots, slots, slots, slots, xres, route, mod, ln_g.reshape(1, d), ln_b.reshape(1, d), ye)


def _slot_kernel(route_ref, off_ref, o_ref):
    tt = o_ref.shape[-1]
    lane = lax.broadcasted_iota(jnp.int32, (tt, V7X_LANES), 1)
    lane_f = lane.astype(F32)
    for s in range(o_ref.shape[0]):
        rec = route_ref[s * tt:(s + 1) * tt, :]
        off1 = jnp.sum(jnp.where(lane_f == rec[:, 0:1], off_ref[...], 0.0), axis=-1, keepdims=True)
        off2 = jnp.sum(jnp.where(lane_f == rec[:, 1:2], off_ref[...], 0.0), axis=-1, keepdims=True)
        both = jnp.where(lane == 0, off1 + rec[:, 4:5], jnp.where(lane == 1, off2 + rec[:, 5:6], 0.0))
        both_t = both.T
        o_ref[s, 0] = both_t[0:1, :].astype(jnp.int32)
        o_ref[s, 1] = both_t[1:2, :].astype(jnp.int32)


def _slots(route, row_off):
    t = route.shape[0]
    tt = TOKEN_TILE
    tiles_per_step = 8
    assert t % (tt * tiles_per_step) == 0
    off = jnp.pad(row_off.astype(F32), (0, V7X_LANES - N_EXPERTS)).reshape(1, V7X_LANES)
    return pl.pallas_call(
        _slot_kernel,
        out_shape=_sds((t // tt, 2, 1, tt), jnp.int32),
        grid=(t // (tt * tiles_per_step),),
        in_specs=[pl.BlockSpec((tt * tiles_per_step, V7X_LANES), lambda i: (i, 0)),
                  pl.BlockSpec((1, V7X_LANES), lambda i: (0, 0))],
        out_specs=pl.BlockSpec((tiles_per_step, 2, 1, tt), lambda i: (i, 0, 0, 0)),
        compiler_params=_cparams(("arbitrary",), 32),
        name="moe_slots",
    )(route, off)


def _moe_tables(counts, n_tokens):
    tm = MOE_ROW_TILE
    max_tiles = (2 * n_tokens) // tm + N_EXPERTS
    cnt = counts[:, 0].astype(jnp.int32)
    tiles_e = (cnt + tm - 1) // tm
    tile_end = jnp.cumsum(tiles_e)
    row_off = (tile_end - tiles_e) * tm
    n_used = tile_end[-1:]
    tail_slot = jnp.concatenate([jnp.where(tiles_e > 0, (tile_end - 1) * tm, -1), n_used]).astype(jnp.int32)
    tile_ids = jnp.minimum(jnp.arange(max_tiles, dtype=jnp.int32), n_used - 1)
    tile_expert = jnp.sum(tile_ids[:, None] >= tile_end[None, :], axis=1).astype(jnp.int32)
    return row_off, tail_slot, tile_expert, n_used.astype(jnp.int32), max_tiles * tm


def _moe_layer(x1, route, counts, mod, w_gate, w_up, w_down, ln_g, ln_b, *, layer, seq_len):
    row_off, tail_slot, tile_expert, n_used, n_slots = _moe_tables(counts, route.shape[0])
    slots = _slots(route, row_off)
    xs = _dispatch(x1, mod, slots, tail_slot, n_slots, seq_len=seq_len)
    ye = _ffn(xs, tile_expert, n_used, w_gate, w_up, w_down, layer=layer)
    return _combine_ln(x1, route, slots, ye, mod, ln_g, ln_b, seq_len=seq_len)


def _qkv_body(x_ref, mod_ref, w_ref, o_ref, hm_ref, *, sc, sh, tiles_per_batch, tiles_per_kind):
    @pl.when(pl.program_id(1) == 0)
    def _():
        _modulate_into(hm_ref, x_ref, mod_ref, sc, sh)

    tm = x_ref.shape[0]
    kind = pl.program_id(1) // tiles_per_kind
    lane = lax.broadcasted_iota(jnp.int32, (tm, V7X_LANES), 1)
    low = lane < ATTN_HEAD_DIM
    row = lax.broadcasted_iota(jnp.int32, (tm, V7X_LANES), 0)
    pos = lax.rem(pl.program_id(0), tiles_per_batch) * tm + row
    k_extra = (lane == ATTN_HEAD_DIM + lax.div(pos, MOBA_BLOCK)).astype(F32)
    v_extra = (lane == ATTN_HEAD_DIM).astype(F32)
    extra = jnp.where(kind == 1, k_extra, jnp.where(kind == 2, v_extra, 0.0))
    half = w_ref.shape[1] // 2
    for c in range(2):
        acc = _dot(hm_ref[...], w_ref[:, c * half:(c + 1) * half])
        for p in range(half // V7X_LANES):
            t2 = acc[:, p * V7X_LANES:(p + 1) * V7X_LANES]
            even = jnp.where(low, t2, extra)
            odd = jnp.where(low, pltpu.roll(t2, ATTN_HEAD_DIM, axis=1), extra)
            head = 2 * (c * (half // V7X_LANES) + p)
            o_ref[0, 0, head] = even.astype(BF16)
            o_ref[0, 0, head + 1] = odd.astype(BF16)


def _qkv(x, mod, w, *, batch, seq_len, tm=1024, tn=512):
    t, k = x.shape
    n = w.shape[1]
    heads_per_tile = tn // ATTN_HEAD_DIM
    tiles_per_batch = seq_len // tm
    tiles_per_kind = ATTN_HEADS // heads_per_tile
    assert n == 3 * ATTN_HEADS * ATTN_HEAD_DIM and seq_len % tm == 0
    return pl.pallas_call(
        functools.partial(_qkv_body, sc=1, sh=0, tiles_per_batch=tiles_per_batch, tiles_per_kind=tiles_per_kind),
        out_shape=_sds((3, batch, ATTN_HEADS, seq_len, V7X_LANES), BF16),
        grid=(t // tm, n // tn),
        in_specs=[pl.BlockSpec((tm, k), lambda i, j: (i, 0)),
                  pl.BlockSpec((1, 6, k), lambda i, j: (i // tiles_per_batch, 0, 0)),
                  pl.BlockSpec((k, tn), lambda i, j: (0, j))],
        out_specs=pl.BlockSpec((1, 1, heads_per_tile, tm, V7X_LANES),
                               lambda i, j: (j // tiles_per_kind, i // tiles_per_batch, j % tiles_per_kind,
                                             i % tiles_per_batch, 0)),
        scratch_shapes=[pltpu.VMEM((tm, k), BF16)],
        compiler_params=_cparams(("arbitrary", "arbitrary"), 40),
        name="qkv_proj",
    )(x, mod, w)


ATTN_HEADS_PER_STEP = 2
ATTN_Q_TILE = 1024
GATE_ROWS = 512


def _attn_kernel(q_ref, k_ref, v_ref, pm_ref, o_ref, km_ref, qa_ref):
    tq = ATTN_Q_TILE
    seq_len = q_ref.shape[2]
    n_blocks = seq_len // MOBA_BLOCK
    heads = range(ATTN_HEADS_PER_STEP)
    qi = pl.program_id(2)

    @pl.when(qi == 0)
    def _():
        for hh in heads:
            km_ref[hh] = _dot(pm_ref[...], k_ref[0, hh]).astype(BF16)
        nb_pad = -(-n_blocks // V7X_SUBLANES) * V7X_SUBLANES
        blk_t = lax.broadcasted_iota(jnp.int32, (nb_pad, GATE_ROWS), 0)
        pos_t = lax.broadcasted_iota(jnp.int32, (nb_pad, GATE_ROWS), 1)

        def gate_rows(ci, carry, *, select):
            r0 = pl.multiple_of(ci * GATE_ROWS, GATE_ROWS)
            own_blk = lax.div(r0 + pos_t, MOBA_BLOCK)
            past = blk_t < own_blk
            for hh in heads:
                q = q_ref[0, hh, pl.ds(r0, GATE_ROWS), :]
                chosen = past
                if select:
                    gates = _dot_nt(km_ref[hh], q)[ATTN_HEAD_DIM:ATTN_HEAD_DIM + nb_pad, :]
                    gates = jnp.where(past, gates, -jnp.inf)
                    rank = jnp.zeros((nb_pad, GATE_ROWS), F32)
                    for i in range(n_blocks):
                        g_i = gates[i:i + 1, :]
                        beats = (g_i > gates) | ((g_i == gates) & (blk_t > i))
                        rank = rank + beats.astype(F32)
                    chosen = past & (rank < MOBA_TOPK)
                mask_t = jnp.where(chosen | (blk_t == own_blk), 0.0, NEG_INF)
                mask_t = jnp.concatenate([jnp.zeros((ATTN_HEAD_DIM, GATE_ROWS), F32), mask_t,
                                          jnp.zeros((V7X_LANES - ATTN_HEAD_DIM - nb_pad, GATE_ROWS), F32)], axis=0)
                qa_ref[hh, pl.ds(r0, GATE_ROWS), :] = (q.astype(F32) + mask_t.T).astype(BF16)
            return carry

        keep_all = min(seq_len, (MOBA_TOPK + 1) * MOBA_BLOCK) // GATE_ROWS
        lax.fori_loop(0, keep_all, functools.partial(gate_rows, select=False), 0)
        lax.fori_loop(keep_all, seq_len // GATE_ROWS, functools.partial(gate_rows, select=True), 0)

    q0 = pl.multiple_of(qi * tq, tq)

    def online_update(m_i, acc, s, v):
        m_new = jnp.maximum(m_i, jnp.max(s, axis=-1, keepdims=True))
        p = jnp.exp(s - m_new)
        return m_new, acc * jnp.exp(m_i - m_new) + _dot(p.astype(BF16), v)

    def kv_step(j, carry):
        start = pl.multiple_of(j * tq, tq)
        out = []
        for hh in heads:
            m_i, acc = carry[hh]
            s = _dot_nt(qa_ref[hh, pl.ds(q0, tq), :], k_ref[0, hh, pl.ds(start, tq), :])
            out.append(online_update(m_i, acc, s, v_ref[0, hh, pl.ds(start, tq), :]))
        return tuple(out)

    init = tuple((jnp.full((tq, 1), NEG_INF, F32), jnp.zeros((tq, V7X_LANES), F32)) for _ in heads)
    carry = lax.fori_loop(0, qi, kv_step, init)

    row = lax.broadcasted_iota(jnp.int32, (tq, tq), 0)
    col = lax.broadcasted_iota(jnp.int32, (tq, tq), 1)
    outs = []
    for hh in heads:
        m_i, acc = carry[hh]
        s = _dot_nt(qa_ref[hh, pl.ds(q0, tq), :], k_ref[0, hh, pl.ds(q0, tq), :])
        _, acc = online_update(m_i, acc, jnp.where(col <= row, s, NEG_INF), v_ref[0, hh, pl.ds(q0, tq), :])
        outs.append(acc / acc[:, ATTN_HEAD_DIM:ATTN_HEAD_DIM + 1])
    lane = lax.broadcasted_iota(jnp.int32, (tq, V7X_LANES), 1)
    for pair in range(ATTN_HEADS_PER_STEP // 2):
        both = jnp.where(lane < ATTN_HEAD_DIM, outs[2 * pair], pltpu.roll(outs[2 * pair + 1], ATTN_HEAD_DIM, axis=1))
        o_ref[0, :, pair * V7X_LANES:(pair + 1) * V7X_LANES] = both.astype(BF16)


def _attention(qkv, *, batch, seq_len):
    nb = seq_len // MOBA_BLOCK
    tq = ATTN_Q_TILE
    hps = ATTN_HEADS_PER_STEP
    assert hps % 2 == 0 and 2 * ATTN_HEAD_DIM == V7X_LANES and nb <= V7X_LANES - ATTN_HEAD_DIM
    assert seq_len % GATE_ROWS == 0 and seq_len % tq == 0 and tq % MOBA_BLOCK == 0
    rows = jnp.arange(V7X_LANES, dtype=jnp.int32)[:, None] - ATTN_HEAD_DIM
    cols = jnp.arange(seq_len, dtype=jnp.int32)[None, :] // MOBA_BLOCK
    pool = jnp.where(rows == cols, 1.0 / MOBA_BLOCK, 0.0).astype(BF16)
    def head_spec(kind):
        return pl.BlockSpec((None, 1, hps, seq_len, V7X_LANES), lambda b, h, i: (kind, b, h, 0, 0))

    return pl.pallas_call(
        _attn_kernel,
        out_shape=_sds((batch, seq_len, ATTN_HEADS * ATTN_HEAD_DIM), BF16),
        grid=(batch, ATTN_HEADS // hps, seq_len // tq),
        in_specs=[head_spec(0), head_spec(1), head_spec(2),
                  pl.BlockSpec((V7X_LANES, seq_len), lambda b, h, i: (0, 0))],
        out_specs=pl.BlockSpec((1, tq, hps * ATTN_HEAD_DIM), lambda b, h, i: (b, i, h)),
        scratch_shapes=[pltpu.VMEM((hps, V7X_LANES, V7X_LANES), BF16),
                        pltpu.VMEM((hps, seq_len, V7X_LANES), BF16)],
        compiler_params=_cparams(("arbitrary", "arbitrary", "arbitrary"), 48),
        name="moba_attention",
    )(qkv, qkv, qkv, pool)


def kernel(x, c, w_ada, b_ada, ln_g, ln_b, ssm_w_in, ssm_conv_w, ssm_conv_b, ssm_dt_bias, ssm_a_log, ssm_d,
           ssm_norm_w, ssm_w_out, attn_w_qkv, attn_w_o, w_router, b_router, moe_w_gate, moe_w_up, moe_w_down):
    batch, seq_len, d = x.shape
    assert seq_len % 512 == 0 and seq_len % SSM_CHUNK == 0 and seq_len % MOBA_BLOCK == 0
    t = batch * seq_len
    mod = _ada_mod(c, w_ada, b_ada)
    xf = x.reshape(t, d)

    d_inner = SSM_GROUPS * GROUP_W
    conv_dim = d_inner + 2 * SSM_GROUPS * SSM_STATE
    w_in = ssm_w_in[0]
    heads = ssm_dt_bias.shape[1]
    w_main = w_in[:, :d_inner + conv_dim].astype(BF16)
    w_dt = jnp.pad(w_in[:, d_inner + conv_dim:], ((0, 0), (0, V7X_LANES - heads))).astype(BF16)
    proj = _mm_mod(xf, mod[0], w_main, sc=1, sh=0, seq_len=seq_len, out_dtype=BF16)
    dt_raw = _mm_mod(xf, mod[0], w_dt, sc=1, sh=0, seq_len=seq_len, out_dtype=F32)
    y = _ssd(proj, dt_raw, ssm_conv_w[0], ssm_conv_b[0], ssm_dt_bias[0], ssm_a_log[0], ssm_d[0], ssm_norm_w[0],
             batch=batch, seq_len=seq_len)
    x1, route, counts = _proj_ln_route(y, ssm_w_out[0].astype(BF16), xf, mod[0], ln_g[0, 0], ln_b[0, 0],
                                       w_router, b_router, seq_len=seq_len)
    xf = _moe_layer(x1, route, counts, mod[0], moe_w_gate, moe_w_up, moe_w_down, ln_g[0, 1], ln_b[0, 1],
                    layer=0, seq_len=seq_len)

    assert ATTN_HEAD_DIM ** -0.5 == 0.125
    col_scale = jnp.where(jnp.arange(3 * d) < d, ATTN_HEAD_DIM ** -0.5, 1.0)
    w_qkv = (attn_w_qkv[0] * col_scale).astype(BF16)
    qkv = _qkv(xf, mod[1], w_qkv, batch=batch, seq_len=seq_len)
    o = _attention(qkv, batch=batch, seq_len=seq_len).reshape(t, d)
    x1, route, counts = _proj_ln_route(o, attn_w_o[0].astype(BF16), xf, mod[1], ln_g[1, 0], ln_b[1, 0],
                                       w_router, b_router, seq_len=seq_len)
    xf = _moe_layer(x1, route, counts, mod[1], moe_w_gate, moe_w_up, moe_w_down, ln_g[1, 1], ln_b[1, 1],
                    layer=1, seq_len=seq_len)
    return xf.reshape(batch, seq_len, d)
```

```python
import functools

import jax
import jax.numpy as jnp
from jax import lax
from jax.experimental import pallas as pl
from jax.experimental.pallas import tpu as pltpu

F32 = jnp.float32
BF16 = jnp.bfloat16

V7X_LANES = 128
V7X_SUBLANES = 8
V7X_VMEM_BYTES = 64 * 1024 * 1024

DEPTH = 2
SSM_HEAD_DIM = 64
SSM_STATE = 128
SSM_GROUPS = 8
SSM_HEADS_PER_GROUP = 4
SSM_CONV = 4
SSM_CHUNK = 256
ATTN_HEADS = 16
ATTN_HEAD_DIM = 64
MOBA_BLOCK = 256
MOBA_TOPK = 3
N_EXPERTS = 32
N_EXPERT_GROUPS = 4
EXPERTS_PER_GROUP = N_EXPERTS // N_EXPERT_GROUPS
DEEPNORM_ALPHA = (2.0 * DEPTH) ** 0.25
LN_EPS = 1e-5
RMS_EPS = 1e-5
NEG_INF = -1e30
LOG2_E = 1.4426950408889634

GROUP_W = SSM_HEADS_PER_GROUP * SSM_HEAD_DIM
CONV_HALO = V7X_SUBLANES
MOE_ROW_TILE = 512
TOKEN_TILE = 256
DMA_UNROLL = 8


def _cparams(semantics, vmem_mib):
    assert vmem_mib * 1024 * 1024 <= V7X_VMEM_BYTES
    return pltpu.CompilerParams(dimension_semantics=semantics, vmem_limit_bytes=vmem_mib * 1024 * 1024)


def _sds(shape, dtype):
    return jax.ShapeDtypeStruct(shape, dtype)


def _dot(a, b):
    return jnp.dot(a, b, preferred_element_type=F32)


def _dot_nt(a, b):
    return lax.dot_general(a, b, (((1,), (1,)), ((), ())), preferred_element_type=F32)


def _dot_tn(a, b):
    return lax.dot_general(a, b, (((0,), (0,)), ((), ())), preferred_element_type=F32)


def _split3(a):
    hi = a.astype(BF16)
    r1 = a - hi.astype(F32)
    mid = r1.astype(BF16)
    lo = (r1 - mid.astype(F32)).astype(BF16)
    return hi, mid, lo


def _silu(x):
    h = 0.5 * x
    return h + h * jnp.tanh(h)


def _layer_norm(v, gamma, beta):
    mu = jnp.mean(v, axis=-1, keepdims=True)
    d = v - mu
    var = jnp.mean(d * d, axis=-1, keepdims=True)
    return d * lax.rsqrt(var + LN_EPS) * gamma + beta


def _ada_kernel(c_ref, w_ref, b_ref, o_ref):
    cs = _silu(c_ref[...])
    o_ref[0] = jnp.dot(cs, w_ref[0], preferred_element_type=F32, precision=lax.Precision.HIGHEST) + b_ref[0]


def _ada_mod(c, w_ada, b_ada):
    depth, d, n = w_ada.shape
    b = c.shape[0]
    tn = 1024
    out = pl.pallas_call(
        _ada_kernel,
        out_shape=_sds((depth, b, n), F32),
        grid=(depth, n // tn),
        in_specs=[pl.BlockSpec((b, d), lambda l, j: (0, 0)),
                  pl.BlockSpec((1, d, tn), lambda l, j: (l, 0, j)),
                  pl.BlockSpec((1, 1, tn), lambda l, j: (l, 0, j))],
        out_specs=pl.BlockSpec((1, b, tn), lambda l, j: (l, 0, j)),
        compiler_params=_cparams(("arbitrary", "arbitrary"), 32),
        name="ada_mod",
    )(c, w_ada, b_ada.reshape(depth, 1, n))
    return out.reshape(depth, b, 6, d)


def _modulate_into(hm_ref, x_ref, mod_ref, sc, sh):
    m = mod_ref[0]
    hm_ref[...] = (x_ref[...] * (1.0 + m[sc:sc + 1, :]) + m[sh:sh + 1, :]).astype(BF16)


def _mm_mod_kernel(x_ref, mod_ref, w_ref, o_ref, hm_ref, *, sc, sh):
    @pl.when(pl.program_id(1) == 0)
    def _():
        _modulate_into(hm_ref, x_ref, mod_ref, sc, sh)

    o_ref[...] = _dot(hm_ref[...], w_ref[...]).astype(o_ref.dtype)


def _mm_mod(x, mod, w, *, sc, sh, seq_len, out_dtype, tm=2048, tn=1536):
    t, k = x.shape
    n = w.shape[1]
    tn = min(tn, n)
    assert seq_len % tm == 0 and n % tn == 0
    tiles_per_batch = seq_len // tm
    return pl.pallas_call(
        functools.partial(_mm_mod_kernel, sc=sc, sh=sh),
        out_shape=_sds((t, n), out_dtype),
        grid=(t // tm, n // tn),
        in_specs=[pl.BlockSpec((tm, k), lambda i, j: (i, 0)),
                  pl.BlockSpec((1, 6, k), lambda i, j: (i // tiles_per_batch, 0, 0)),
                  pl.BlockSpec((k, tn), lambda i, j: (0, j))],
        out_specs=pl.BlockSpec((tm, tn), lambda i, j: (i, j)),
        scratch_shapes=[pltpu.VMEM((tm, k), BF16)],
        compiler_params=_cparams(("arbitrary", "arbitrary"), 48),
        name="mm_mod",
    )(x, mod, w)


def _ssd_kernel(z_ref, xp_ref, bcp_ref, dtr_ref, cw_ref, cb_ref, dtb_ref, alog_ref, dsk_ref, nw_ref,
                o_ref, ubuf, act, state, acum_t):
    lc = SSM_CHUNK
    d_inner = xp_ref.shape[1]
    n_slabs = ubuf.shape[0]
    x_slabs = d_inner // V7X_LANES
    chunk = pl.program_id(1)

    @pl.when(chunk == 0)
    def _():
        ubuf[:, 0:CONV_HALO, :] = jnp.zeros((n_slabs, CONV_HALO, V7X_LANES), F32)
        state[...] = jnp.zeros_like(state)

    def conv_slabs(src_ref, first_slab, count):
        def slab_body(j, carry):
            src_off = pl.multiple_of(j * V7X_LANES, V7X_LANES)
            slab = first_slab + j
            off = pl.multiple_of(slab * V7X_LANES, V7X_LANES)
            u = src_ref[:, pl.ds(src_off, V7X_LANES)].astype(F32)
            ubuf[slab, CONV_HALO:CONV_HALO + lc, :] = u
            acc = cb_ref[:, pl.ds(off, V7X_LANES)] + cw_ref[SSM_CONV - 1:SSM_CONV, pl.ds(off, V7X_LANES)] * u
            for k in range(SSM_CONV - 1):
                tap = CONV_HALO - (SSM_CONV - 1) + k
                acc = acc + cw_ref[k:k + 1, pl.ds(off, V7X_LANES)] * ubuf[slab, tap:tap + lc, :]
            act[:, pl.ds(off, V7X_LANES)] = _silu(acc)
            ubuf[slab, 0:CONV_HALO, :] = u[lc - CONV_HALO:lc, :]
            return carry

        lax.fori_loop(0, count, slab_body, 0)

    conv_slabs(xp_ref, 0, x_slabs)
    conv_slabs(bcp_ref, x_slabs, n_slabs - x_slabs)

    dt = jax.nn.softplus(dtr_ref[...] + dtb_ref[...])
    da = dt * (-LOG2_E * jnp.exp(alog_ref[...]))
    row = lax.broadcasted_iota(jnp.int32, (lc, lc), 0)
    col = lax.broadcasted_iota(jnp.int32, (lc, lc), 1)
    causal = col <= row
    tril = causal.astype(BF16)
    acum = sum(_dot(tril, part) for part in _split3(da))
    acum_t[...] = acum.T
    acum_parts = _split3(acum)
    dt_bf16 = dt.astype(BF16)
    head_of_col = lax.broadcasted_iota(jnp.int32, (lc, GROUP_W), 1) // SSM_HEAD_DIM
    sel_row = lax.broadcasted_iota(jnp.int32, (V7X_LANES, GROUP_W), 0)
    sel_col = lax.broadcasted_iota(jnp.int32, (V7X_LANES, GROUP_W), 1) // SSM_HEAD_DIM

    def group_body(g, carry):
        xo = pl.multiple_of(g * GROUP_W, GROUP_W)
        bo = pl.multiple_of(d_inner + g * SSM_STATE, SSM_STATE)
        co = pl.multiple_of(d_inner + SSM_GROUPS * SSM_STATE + g * SSM_STATE, SSM_STATE)
        x_g = act[:, pl.ds(xo, GROUP_W)]
        b_g = act[:, pl.ds(bo, SSM_STATE)].astype(BF16)
        c_g = act[:, pl.ds(co, SSM_STATE)].astype(BF16)
        sel = (sel_row == SSM_HEADS_PER_GROUP * g + sel_col).astype(BF16)
        ab = sum(_dot(part, sel) for part in acum_parts)
        dtb = _dot(dt_bf16, sel)
        alast = ab[lc - 1:lc, :]
        cb = _dot_nt(c_g, b_g)
        xdt = x_g * dtb
        y = jnp.zeros((lc, GROUP_W), F32)
        for r in range(SSM_HEADS_PER_GROUP):
            acol = ab[:, r * SSM_HEAD_DIM:r * SSM_HEAD_DIM + 1]
            arow = acum_t[pl.ds(SSM_HEADS_PER_GROUP * g + r, 1), :]
            decay = jnp.exp2(jnp.where(causal, acol - arow, -jnp.inf))
            w = (cb * decay).astype(BF16)
            x_r = jnp.where(head_of_col == r, xdt, 0.0).astype(BF16)
            y = y + _dot(w, x_r)
        st = state[:, pl.ds(xo, GROUP_W)]
        y = y + _dot(c_g, st.astype(BF16)) * jnp.exp2(ab)
        to_end = jnp.exp2(alast - ab) * dtb
        xw = (x_g * to_end).astype(BF16)
        state[:, pl.ds(xo, GROUP_W)] = st * jnp.exp2(alast) + _dot_tn(b_g, xw)
        y = y + dsk_ref[:, pl.ds(xo, GROUP_W)] * x_g
        gated = y * _silu(z_ref[:, pl.ds(xo, GROUP_W)].astype(F32))
        ms = jnp.mean(gated * gated, axis=-1, keepdims=True)
        o_ref[:, pl.ds(xo, GROUP_W)] = (gated * lax.rsqrt(ms + RMS_EPS) * nw_ref[:, pl.ds(xo, GROUP_W)]).astype(BF16)
        return carry

    lax.fori_loop(0, SSM_GROUPS, group_body, 0, unroll=4)


def _ssd(proj, dt_raw, conv_w, conv_b, dt_bias, a_log, d_skip, norm_w, *, batch, seq_len):
    t = proj.shape[0]
    d_inner = SSM_GROUPS * GROUP_W
    conv_dim = d_inner + 2 * SSM_GROUPS * SSM_STATE
    assert proj.shape[1] == d_inner + conv_dim and conv_dim == 2 * d_inner
    nc = seq_len // SSM_CHUNK
    lc = SSM_CHUNK
    pad = V7X_LANES - dt_bias.shape[0]
    heads = dt_bias.shape[0]
    row_map = lambda b, c: b * nc + c
    const = lambda b, c: (0, 0)
    return pl.pallas_call(
        _ssd_kernel,
        out_shape=_sds((t, d_inner), BF16),
        grid=(batch, nc),
        in_specs=[pl.BlockSpec((lc, d_inner), lambda b, c: (row_map(b, c), 0)),
                  pl.BlockSpec((lc, d_inner), lambda b, c: (row_map(b, c), 1)),
                  pl.BlockSpec((lc, d_inner), lambda b, c: (row_map(b, c), 2)),
                  pl.BlockSpec((lc, V7X_LANES), lambda b, c: (row_map(b, c), 0)),
                  pl.BlockSpec((SSM_CONV, conv_dim), const),
                  pl.BlockSpec((1, conv_dim), const),
                  pl.BlockSpec((1, V7X_LANES), const),
                  pl.BlockSpec((1, V7X_LANES), const),
                  pl.BlockSpec((1, d_inner), const),
                  pl.BlockSpec((1, d_inner), const)],
        out_specs=pl.BlockSpec((lc, d_inner), lambda b, c: (row_map(b, c), 0)),
        scratch_shapes=[pltpu.VMEM((conv_dim // V7X_LANES, CONV_HALO + lc, V7X_LANES), F32),
                        pltpu.VMEM((lc, conv_dim), F32),
                        pltpu.VMEM((SSM_STATE, d_inner), F32),
                        pltpu.VMEM((V7X_LANES, lc), F32)],
        compiler_params=_cparams(("arbitrary", "arbitrary"), 48),
        name="ssd_chunk_scan",
    )(proj, proj, proj, dt_raw, conv_w, conv_b.reshape(1, conv_dim),
      jnp.pad(dt_bias, (0, pad)).reshape(1, V7X_LANES), jnp.pad(a_log, (0, pad)).reshape(1, V7X_LANES),
      jnp.repeat(d_skip, SSM_HEAD_DIM).reshape(1, heads * SSM_HEAD_DIM), norm_w.reshape(1, d_inner))


def _route(logits_t, carry_ref):
    tm = logits_t.shape[1]
    lg = logits_t[0:N_EXPERTS, :]
    eid = lax.broadcasted_iota(jnp.int32, (N_EXPERTS, tm), 0).astype(F32)
    e = jnp.exp(lg - jnp.max(lg, axis=0, keepdims=True))
    far = float(N_EXPERTS)
    best = None
    for g in range(N_EXPERT_GROUPS):
        eg = e[g * EXPERTS_PER_GROUP:(g + 1) * EXPERTS_PER_GROUP, :]
        ig = (lax.broadcasted_iota(jnp.int32, (EXPERTS_PER_GROUP, tm), 0) + g * EXPERTS_PER_GROUP).astype(F32)
        m1 = jnp.max(eg, axis=0, keepdims=True)
        i1 = jnp.min(jnp.where(eg == m1, ig, far), axis=0, keepdims=True)
        eg2 = jnp.where(ig == i1, -1.0, eg)
        m2 = jnp.max(eg2, axis=0, keepdims=True)
        i2 = jnp.min(jnp.where(eg2 == m2, ig, far), axis=0, keepdims=True)
        cand = (m1 + m2, m1, m2, i1, i2)
        if best is None:
            best = cand
        else:
            take = cand[0] > best[0]
            best = tuple(jnp.where(take, c, b) for c, b in zip(cand, best))
    _, m1, m2, i1, i2 = best
    denom = m1 + m2
    hit1 = eid == i1
    hit2 = eid == i2
    onehot = (hit1 | hit2).astype(F32)
    row = lax.broadcasted_iota(jnp.int32, (tm, tm), 0)
    col = lax.broadcasted_iota(jnp.int32, (tm, tm), 1)
    earlier = (row < col).astype(BF16)
    rank = _dot(onehot.astype(BF16), earlier) + carry_ref[:, 0:1]
    r1 = jnp.sum(jnp.where(hit1, rank, 0.0), axis=0, keepdims=True)
    r2 = jnp.sum(jnp.where(hit2, rank, 0.0), axis=0, keepdims=True)
    carry_ref[...] = carry_ref[...] + jnp.sum(onehot, axis=1, keepdims=True)
    sub = lax.broadcasted_iota(jnp.int32, (V7X_SUBLANES, tm), 0)
    rec_t = jnp.zeros((V7X_SUBLANES, tm), F32)
    for k, val in enumerate((i1, i2, m1 / denom, m2 / denom, r1, r2)):
        rec_t = jnp.where(sub == k, val, rec_t)
    padded = jnp.concatenate([rec_t, jnp.zeros((V7X_LANES - V7X_SUBLANES, tm), F32)], axis=0)
    return padded.T, rec_t


def _proj_ln_route_kernel(a_ref, w_ref, xres_ref, mod_ref, lng_ref, lnb_ref, wr_ref, br_ref,
                          x_ref, route_ref, route_t_ref, cnt_ref, carry_ref, *, gate_idx, sc, sh):
    @pl.when(pl.program_id(0) == 0)
    def _():
        carry_ref[...] = jnp.zeros_like(carry_ref)

    m = mod_ref[0]
    y = _dot(a_ref[...], w_ref[...])
    v = DEEPNORM_ALPHA * xres_ref[...] + (1.0 + m[gate_idx:gate_idx + 1, :]) * y
    x1 = _layer_norm(v, lng_ref[...], lnb_ref[...])
    x_ref[...] = x1
    hm = (x1 * (1.0 + m[sc:sc + 1, :]) + m[sh:sh + 1, :]).astype(BF16)
    logits_t = _dot_nt(wr_ref[...], hm) + br_ref[...]
    route_ref[...], route_t_ref[...] = _route(logits_t, carry_ref)
    cnt_ref[...] = carry_ref[...]


def _proj_ln_route(a, w, xres, mod, ln_g, ln_b, w_router, b_router, *, seq_len):
    t, k = a.shape
    d = w.shape[1]
    tm = TOKEN_TILE
    tiles_per_batch = seq_len // tm
    const = lambda i: (0, 0)
    wr = jnp.pad(w_router.T, ((0, V7X_LANES - N_EXPERTS), (0, 0))).astype(BF16)
    br = jnp.pad(b_router, (0, V7X_LANES - N_EXPERTS)).reshape(V7X_LANES, 1)
    return pl.pallas_call(
        functools.partial(_proj_ln_route_kernel, gate_idx=2, sc=4, sh=3),
        out_shape=(_sds((t, d), F32), _sds((t, V7X_LANES), F32), _sds((V7X_SUBLANES, t), F32),
                   _sds((N_EXPERTS, V7X_LANES), F32)),
        grid=(t // tm,),
        in_specs=[pl.BlockSpec((tm, k), lambda i: (i, 0)),
                  pl.BlockSpec((k, d), const),
                  pl.BlockSpec((tm, d), lambda i: (i, 0)),
                  pl.BlockSpec((1, 6, d), lambda i: (i // tiles_per_batch, 0, 0)),
                  pl.BlockSpec((1, d), const),
                  pl.BlockSpec((1, d), const),
                  pl.BlockSpec((V7X_LANES, d), const),
                  pl.BlockSpec((V7X_LANES, 1), const)],
        out_specs=(pl.BlockSpec((tm, d), lambda i: (i, 0)),
                   pl.BlockSpec((tm, V7X_LANES), lambda i: (i, 0)),
                   pl.BlockSpec((V7X_SUBLANES, tm), lambda i: (0, i)),
                   pl.BlockSpec((N_EXPERTS, V7X_LANES), const)),
        scratch_shapes=[pltpu.VMEM((N_EXPERTS, V7X_LANES), F32)],
        compiler_params=_cparams(("arbitrary",), 40),
        name="proj_ln_route",
    )(a, w, xres, mod, ln_g.reshape(1, d), ln_b.reshape(1, d), wr, br)


def _record_copy(src_ref, src_slot, dst_ref, dst_slot, sem, n):
    src = src_ref.at[pl.ds(pl.multiple_of(src_slot * n, n), n)]
    dst = dst_ref.at[pl.ds(pl.multiple_of(dst_slot * n, n), n)]
    return pltpu.make_async_copy(src, dst, sem)


def _to_records(rec_ref, rows):
    t = rows.shape[0]
    n = rows.shape[1] // V7X_LANES
    for s in range(n):
        rec_ref[pl.ds(s, t, stride=n), :] = rows[:, s * V7X_LANES:(s + 1) * V7X_LANES]


def _from_records(rec_ref, t, n):
    return jnp.concatenate([rec_ref[pl.ds(s, t, stride=n), :] for s in range(n)], axis=1)


def _dispatch_kernel(tail_ref, p1_ref, p2_ref, x_ref, mod_ref, xs_ref, hm_ref, zero_ref, sem, zero_sem, *, sc, sh):
    tt, d = x_ref.shape
    n = d // V7X_LANES
    i = pl.program_id(0)
    n_steps = pl.num_programs(0)
    cur = lax.rem(i, 2)

    @pl.when(i == 0)
    def _():
        zero_ref[...] = jnp.zeros_like(zero_ref)
        rows = zero_ref.shape[0]

        def tail_copy(e):
            start = pl.multiple_of(jnp.maximum(tail_ref[0, e], 0) * n, rows)
            return pltpu.make_async_copy(zero_ref, xs_ref.at[pl.ds(start, rows)], zero_sem)

        def fill(e, carry):
            @pl.when(tail_ref[0, e] >= 0)
            def _():
                tail_copy(e).start()
            return carry

        def fill_wait(e, carry):
            @pl.when(tail_ref[0, e] >= 0)
            def _():
                tail_copy(e).wait()
            return carry

        def spare_copy(j):
            return pltpu.make_async_copy(zero_ref, xs_ref.at[pl.ds(pl.multiple_of(j * rows, rows), rows)], zero_sem)

        def spare_fill(j, carry):
            spare_copy(j).start()
            return carry

        def spare_wait(j, carry):
            spare_copy(j).wait()
            return carry

        n_tiles = xs_ref.shape[0] // rows
        lax.fori_loop(0, N_EXPERTS, fill, 0)
        lax.fori_loop(tail_ref[0, N_EXPERTS], n_tiles, spare_fill, 0)
        lax.fori_loop(0, N_EXPERTS, fill_wait, 0)
        lax.fori_loop(tail_ref[0, N_EXPERTS], n_tiles, spare_wait, 0)

    def drain(buf):
        def body(blk, carry):
            for _ in range(2 * DMA_UNROLL):
                _record_copy(hm_ref.at[buf], 0, xs_ref, 0, sem.at[buf], n).wait()
            return carry

        lax.fori_loop(0, tt // DMA_UNROLL, body, 0)

    @pl.when(i >= 2)
    def _():
        drain(cur)

    m = mod_ref[0]
    _to_records(hm_ref.at[cur], x_ref[...] * (1.0 + m[sc:sc + 1, :]) + m[sh:sh + 1, :])

    def issue(blk, carry):
        for u in range(DMA_UNROLL):
            r = blk * DMA_UNROLL + u
            _record_copy(hm_ref.at[cur], r, xs_ref, p1_ref[0, r], sem.at[cur], n).start(priority=0)
            _record_copy(hm_ref.at[cur], r, xs_ref, p2_ref[0, r], sem.at[cur], n).start(priority=1)
        return carry

    lax.fori_loop(0, tt // DMA_UNROLL, issue, 0)

    @pl.when(i == n_steps - 1)
    def _():
        drain(cur)

    @pl.when(jnp.logical_and(i == n_steps - 1, i >= 1))
    def _():
        drain(1 - cur)


def _slot_spec(tt, copy):
    return pl.BlockSpec((None, None, 1, tt), lambda i: (i, copy, 0, 0), memory_space=pltpu.SMEM)


def _dispatch(x, mod, slots, tail_slot, n_slots, *, seq_len):
    t, d = x.shape
    n = d // V7X_LANES
    tt = TOKEN_TILE
    tiles_per_batch = seq_len // tt
    return pl.pallas_call(
        functools.partial(_dispatch_kernel, sc=4, sh=3),
        out_shape=_sds((n_slots * n, V7X_LANES), F32),
        grid=(t // tt,),
        in_specs=[pl.BlockSpec((1, N_EXPERTS + 1), lambda i: (0, 0), memory_space=pltpu.SMEM),
                  _slot_spec(tt, 0), _slot_spec(tt, 1),
                  pl.BlockSpec((tt, d), lambda i: (i, 0)),
                  pl.BlockSpec((1, 6, d), lambda i: (i // tiles_per_batch, 0, 0))],
        out_specs=pl.BlockSpec(memory_space=pl.ANY),
        scratch_shapes=[pltpu.VMEM((2, tt * n, V7X_LANES), F32), pltpu.VMEM((MOE_ROW_TILE * n, V7X_LANES), F32),
                        pltpu.SemaphoreType.DMA((2,)), pltpu.SemaphoreType.DMA(())],
        compiler_params=_cparams(("arbitrary",), 32),
        name="moe_dispatch",
    )(tail_slot.reshape(1, N_EXPERTS + 1), slots, slots, x, mod)


def _ffn_kernel(te_ref, nu_ref, x_ref, wg_ref, wu_ref, wd_ref, o_ref, wg_bf, wu_bf, wd_bf):
    i = pl.program_id(0)
    d, f = wg_bf.shape
    n = d // V7X_LANES
    tm = x_ref.shape[0] // n
    used = i < nu_ref[0]
    new_expert = jnp.logical_or(i == 0, te_ref[i] != te_ref[jnp.maximum(i - 1, 0)])

    @pl.when(jnp.logical_and(used, new_expert))
    def _():
        wg_bf[...] = wg_ref[0, 0].astype(BF16)
        wu_bf[...] = wu_ref[0, 0].astype(BF16)
        wd_bf[...] = wd_ref[0, 0].astype(BF16)

    @pl.when(used)
    def _():
        x = _from_records(x_ref, tm, n).astype(BF16)
        hg = _dot(x, wg_bf[...])
        hu = _dot(x, wu_bf[...])
        _to_records(o_ref, _dot((_silu(hg) * hu).astype(BF16), wd_bf[...]))

    @pl.when(jnp.logical_not(used))
    def _():
        o_ref[...] = jnp.zeros_like(o_ref)


def _ffn(xs, tile_expert, n_used, w_gate, w_up, w_down, *, layer):
    _, _, d, f = w_gate.shape
    n = d // V7X_LANES
    tm = MOE_ROW_TILE
    w_map = lambda i, te, nu: (layer, te[i], 0, 0)
    grid_spec = pltpu.PrefetchScalarGridSpec(
        num_scalar_prefetch=2,
        grid=(xs.shape[0] // (tm * n),),
        in_specs=[pl.BlockSpec((tm * n, V7X_LANES), lambda i, te, nu: (jnp.minimum(i, nu[0] - 1), 0)),
                  pl.BlockSpec((1, 1, d, f), w_map),
                  pl.BlockSpec((1, 1, d, f), w_map),
                  pl.BlockSpec((1, 1, f, d), w_map)],
        out_specs=pl.BlockSpec((tm * n, V7X_LANES), lambda i, te, nu: (i, 0)),
        scratch_shapes=[pltpu.VMEM((d, f), BF16), pltpu.VMEM((d, f), BF16), pltpu.VMEM((f, d), BF16)],
    )
    return pl.pallas_call(
        _ffn_kernel,
        out_shape=_sds(xs.shape, F32),
        grid_spec=grid_spec,
        compiler_params=_cparams(("arbitrary",), 48),
        name="moe_ffn",
    )(tile_expert, n_used, xs, w_gate, w_up, w_down)


def _combine_ln_kernel(p1_ref, p2_ref, p1_next_ref, p2_next_ref, xres_ref, route_ref, mod_ref, lng_ref, lnb_ref,
                       ye_ref, o_ref, buf, sem, *, gate_idx):
    tt, d = xres_ref.shape
    n = d // V7X_LANES
    i = pl.program_id(0)
    cur = lax.rem(i, 2)

    def fetch(pa_ref, pb_ref, slot):
        def body(blk, carry):
            for u in range(DMA_UNROLL):
                r = blk * DMA_UNROLL + u
                _record_copy(ye_ref, pa_ref[0, r], buf.at[slot, 0], r, sem.at[slot], n).start(priority=0)
                _record_copy(ye_ref, pb_ref[0, r], buf.at[slot, 1], r, sem.at[slot], n).start(priority=1)
            return carry

        lax.fori_loop(0, tt // DMA_UNROLL, body, 0)

    @pl.when(i == 0)
    def _():
        fetch(p1_ref, p2_ref, 0)

    @pl.when(i + 1 < pl.num_programs(0))
    def _():
        fetch(p1_next_ref, p2_next_ref, 1 - cur)

    def drain(blk, carry):
        for _ in range(2 * DMA_UNROLL):
            _record_copy(ye_ref, 0, buf.at[cur, 0], 0, sem.at[cur], n).wait()
        return carry

    lax.fori_loop(0, tt // DMA_UNROLL, drain, 0)
    m = mod_ref[0]
    rec = route_ref[...]
    y = (rec[:, 2:3] * _from_records(buf.at[cur, 0], tt, n) + rec[:, 3:4] * _from_records(buf.at[cur, 1], tt, n))
    v = DEEPNORM_ALPHA * xres_ref[...] + (1.0 + m[gate_idx:gate_idx + 1, :]) * y
    o_ref[...] = _layer_norm(v, lng_ref[...], lnb_ref[...])


def _combine_ln(xres, route, slots, ye, mod, ln_g, ln_b, *, seq_len):
    t, d = xres.shape
    n = d // V7X_LANES
    tt = TOKEN_TILE
    tiles_per_batch = seq_len // tt
    n_steps = t // tt
    const = lambda i: (0, 0)

    def next_slot_spec(copy):
        return pl.BlockSpec((None, None, 1, tt), lambda i: (jnp.minimum(i + 1, n_steps - 1), copy, 0, 0),
                            memory_space=pltpu.SMEM)

    return pl.pallas_call(
        functools.partial(_combine_ln_kernel, gate_idx=5),
        out_shape=_sds((t, d), F32),
        grid=(n_steps,),
        in_specs=[_slot_spec(tt, 0), _slot_spec(tt, 1), next_slot_spec(0), next_slot_spec(1),
                  pl.BlockSpec((tt, d), lambda i: (i, 0)),
                  pl.BlockSpec((tt, V7X_LANES), lambda i: (i, 0)),
                  pl.BlockSpec((1, 6, d), lambda i: (i // tiles_per_batch, 0, 0)),
                  pl.BlockSpec((1, d), const),
                  pl.BlockSpec((1, d), const),
                  pl.BlockSpec(memory_space=pl.ANY)],
        out_specs=pl.BlockSpec((tt, d), lambda i: (i, 0)),
        scratch_shapes=[pltpu.VMEM((2, 2, tt * n, V7X_LANES), F32), pltpu.SemaphoreType.DMA((2,))],
        compiler_params=_cparams(("arbitrary",), 32),
        name="moe_combine_ln",
    )(slots, slots, slots, slots, xres, route, mod, ln_g.reshape(1, d), ln_b.reshape(1, d), ye)


def _slot_kernel(route_t_ref, off_ref, o_ref):
    tt = o_ref.shape[-1]
    n = route_t_ref.shape[1]
    eid = lax.broadcasted_iota(jnp.int32, (N_EXPERTS, n), 0).astype(F32)
    for copy in range(2):
        expert = route_t_ref[copy:copy + 1, :]
        rank = route_t_ref[4 + copy:5 + copy, :]
        off = jnp.sum(jnp.where(eid == expert, off_ref[:, 0:1], 0.0), axis=0, keepdims=True)
        slot = (off + rank).astype(jnp.int32)
        for s in range(o_ref.shape[0]):
            o_ref[s, copy] = slot[:, s * tt:(s + 1) * tt]


def _slots(route_t, row_off):
    t = route_t.shape[1]
    tt = TOKEN_TILE
    tiles_per_step = 8
    assert t % (tt * tiles_per_step) == 0
    off = jnp.broadcast_to(row_off.astype(F32)[:, None], (N_EXPERTS, V7X_LANES))
    return pl.pallas_call(
        _slot_kernel,
        out_shape=_sds((t // tt, 2, 1, tt), jnp.int32),
        grid=(t // (tt * tiles_per_step),),
        in_specs=[pl.BlockSpec((V7X_SUBLANES, tt * tiles_per_step), lambda i: (0, i)),
                  pl.BlockSpec((N_EXPERTS, V7X_LANES), lambda i: (0, 0))],
        out_specs=pl.BlockSpec((tiles_per_step, 2, 1, tt), lambda i: (i, 0, 0, 0)),
        compiler_params=_cparams(("arbitrary",), 32),
        name="moe_slots",
    )(route_t, off)


def _moe_tables(counts, n_tokens):
    tm = MOE_ROW_TILE
    max_tiles = (2 * n_tokens) // tm + N_EXPERTS
    cnt = counts[:, 0].astype(jnp.int32)
    tiles_e = (cnt + tm - 1) // tm
    tile_end = jnp.cumsum(tiles_e)
    row_off = (tile_end - tiles_e) * tm
    n_used = tile_end[-1:]
    tail_slot = jnp.concatenate([jnp.where(tiles_e > 0, (tile_end - 1) * tm, -1), n_used]).astype(jnp.int32)
    tile_ids = jnp.minimum(jnp.arange(max_tiles, dtype=jnp.int32), n_used - 1)
    tile_expert = jnp.sum(tile_ids[:, None] >= tile_end[None, :], axis=1).astype(jnp.int32)
    return row_off, tail_slot, tile_expert, n_used.astype(jnp.int32), max_tiles * tm


def _moe_layer(x1, routing, mod, w_gate, w_up, w_down, ln_g, ln_b, *, layer, seq_len):
    route, route_t, counts = routing
    row_off, tail_slot, tile_expert, n_used, n_slots = _moe_tables(counts, route.shape[0])
    slots = _slots(route_t, row_off)
    xs = _dispatch(x1, mod, slots, tail_slot, n_slots, seq_len=seq_len)
    ye = _ffn(xs, tile_expert, n_used, w_gate, w_up, w_down, layer=layer)
    return _combine_ln(x1, route, slots, ye, mod, ln_g, ln_b, seq_len=seq_len)


def _qkv_body(x_ref, mod_ref, w_ref, o_ref, hm_ref, *, sc, sh, tiles_per_batch, tiles_per_kind):
    @pl.when(pl.program_id(1) == 0)
    def _():
        _modulate_into(hm_ref, x_ref, mod_ref, sc, sh)

    tm = x_ref.shape[0]
    kind = pl.program_id(1) // tiles_per_kind
    lane = lax.broadcasted_iota(jnp.int32, (tm, V7X_LANES), 1)
    low = lane < ATTN_HEAD_DIM
    row = lax.broadcasted_iota(jnp.int32, (tm, V7X_LANES), 0)
    pos = lax.rem(pl.program_id(0), tiles_per_batch) * tm + row
    k_extra = (lane == ATTN_HEAD_DIM + lax.div(pos, MOBA_BLOCK)).astype(F32)
    v_extra = (lane == ATTN_HEAD_DIM).astype(F32)
    extra = jnp.where(kind == 1, k_extra, jnp.where(kind == 2, v_extra, 0.0))
    half = w_ref.shape[1] // 2
    for c in range(2):
        acc = _dot(hm_ref[...], w_ref[:, c * half:(c + 1) * half])
        for p in range(half // V7X_LANES):
            t2 = acc[:, p * V7X_LANES:(p + 1) * V7X_LANES]
            even = jnp.where(low, t2, extra)
            odd = jnp.where(low, pltpu.roll(t2, ATTN_HEAD_DIM, axis=1), extra)
            head = 2 * (c * (half // V7X_LANES) + p)
            o_ref[0, 0, head] = even.astype(BF16)
            o_ref[0, 0, head + 1] = odd.astype(BF16)


def _qkv(x, mod, w, *, batch, seq_len, tm=1024, tn=512):
    t, k = x.shape
    n = w.shape[1]
    heads_per_tile = tn // ATTN_HEAD_DIM
    tiles_per_batch = seq_len // tm
    tiles_per_kind = ATTN_HEADS // heads_per_tile
    assert n == 3 * ATTN_HEADS * ATTN_HEAD_DIM and seq_len % tm == 0
    return pl.pallas_call(
        functools.partial(_qkv_body, sc=1, sh=0, tiles_per_batch=tiles_per_batch, tiles_per_kind=tiles_per_kind),
        out_shape=_sds((3, batch, ATTN_HEADS, seq_len, V7X_LANES), BF16),
        grid=(t // tm, n // tn),
        in_specs=[pl.BlockSpec((tm, k), lambda i, j: (i, 0)),
                  pl.BlockSpec((1, 6, k), lambda i, j: (i // tiles_per_batch, 0, 0)),
                  pl.BlockSpec((k, tn), lambda i, j: (0, j))],
        out_specs=pl.BlockSpec((1, 1, heads_per_tile, tm, V7X_LANES),
                               lambda i, j: (j // tiles_per_kind, i // tiles_per_batch, j % tiles_per_kind,
                                             i % tiles_per_batch, 0)),
        scratch_shapes=[pltpu.VMEM((tm, k), BF16)],
        compiler_params=_cparams(("arbitrary", "arbitrary"), 40),
        name="qkv_proj",
    )(x, mod, w)


ATTN_HEADS_PER_STEP = 2
ATTN_Q_TILE = 1024
GATE_ROWS = 512


def _attn_kernel(q_ref, k_ref, v_ref, pm_ref, o_ref, km_ref, qa_ref):
    tq = ATTN_Q_TILE
    seq_len = q_ref.shape[2]
    n_blocks = seq_len // MOBA_BLOCK
    heads = range(ATTN_HEADS_PER_STEP)
    qi = pl.program_id(2)

    @pl.when(qi == 0)
    def _():
        for hh in heads:
            km_ref[hh] = _dot(pm_ref[...], k_ref[0, hh]).astype(BF16)
        nb_pad = -(-n_blocks // V7X_SUBLANES) * V7X_SUBLANES
        blk_t = lax.broadcasted_iota(jnp.int32, (nb_pad, GATE_ROWS), 0)
        pos_t = lax.broadcasted_iota(jnp.int32, (nb_pad, GATE_ROWS), 1)

        def gate_rows(ci, carry, *, select):
            r0 = pl.multiple_of(ci * GATE_ROWS, GATE_ROWS)
            own_blk = lax.div(r0 + pos_t, MOBA_BLOCK)
            past = blk_t < own_blk
            for hh in heads:
                q = q_ref[0, hh, pl.ds(r0, GATE_ROWS), :]
                chosen = past
                if select:
                    gates = _dot_nt(km_ref[hh], q)[ATTN_HEAD_DIM:ATTN_HEAD_DIM + nb_pad, :]
                    gates = jnp.where(past, gates, -jnp.inf)
                    rank = jnp.zeros((nb_pad, GATE_ROWS), F32)
                    for i in range(n_blocks):
                        g_i = gates[i:i + 1, :]
                        beats = (g_i > gates) | ((g_i == gates) & (blk_t > i))
                        rank = rank + beats.astype(F32)
                    chosen = past & (rank < MOBA_TOPK)
                mask_t = jnp.where(chosen | (blk_t == own_blk), 0.0, NEG_INF)
                mask_t = jnp.concatenate([jnp.zeros((ATTN_HEAD_DIM, GATE_ROWS), F32), mask_t,
                                          jnp.zeros((V7X_LANES - ATTN_HEAD_DIM - nb_pad, GATE_ROWS), F32)], axis=0)
                qa_ref[hh, pl.ds(r0, GATE_ROWS), :] = (q.astype(F32) + mask_t.T).astype(BF16)
            return carry

        keep_all = min(seq_len, (MOBA_TOPK + 1) * MOBA_BLOCK) // GATE_ROWS
        lax.fori_loop(0, keep_all, functools.partial(gate_rows, select=False), 0)
        lax.fori_loop(keep_all, seq_len // GATE_ROWS, functools.partial(gate_rows, select=True), 0)

    q0 = pl.multiple_of(qi * tq, tq)

    def online_update(m_i, acc, s, v):
        m_new = jnp.maximum(m_i, jnp.max(s, axis=-1, keepdims=True))
        p = jnp.exp(s - m_new)
        return m_new, acc * jnp.exp(m_i - m_new) + _dot(p.astype(BF16), v)

    def kv_step(j, carry):
        start = pl.multiple_of(j * tq, tq)
        out = []
        for hh in heads:
            m_i, acc = carry[hh]
            s = _dot_nt(qa_ref[hh, pl.ds(q0, tq), :], k_ref[0, hh, pl.ds(start, tq), :])
            out.append(online_update(m_i, acc, s, v_ref[0, hh, pl.ds(start, tq), :]))
        return tuple(out)

    init = tuple((jnp.full((tq, 1), NEG_INF, F32), jnp.zeros((tq, V7X_LANES), F32)) for _ in heads)
    carry = lax.fori_loop(0, qi, kv_step, init)

    row = lax.broadcasted_iota(jnp.int32, (tq, tq), 0)
    col = lax.broadcasted_iota(jnp.int32, (tq, tq), 1)
    outs = []
    for hh in heads:
        m_i, acc = carry[hh]
        s = _dot_nt(qa_ref[hh, pl.ds(q0, tq), :], k_ref[0, hh, pl.ds(q0, tq), :])
        _, acc = online_update(m_i, acc, jnp.where(col <= row, s, NEG_INF), v_ref[0, hh, pl.ds(q0, tq), :])
        outs.append(acc / acc[:, ATTN_HEAD_DIM:ATTN_HEAD_DIM + 1])
    lane = lax.broadcasted_iota(jnp.int32, (tq, V7X_LANES), 1)
    for pair in range(ATTN_HEADS_PER_STEP // 2):
        both = jnp.where(lane < ATTN_HEAD_DIM, outs[2 * pair], pltpu.roll(outs[2 * pair + 1], ATTN_HEAD_DIM, axis=1))
        o_ref[0, :, pair * V7X_LANES:(pair + 1) * V7X_LANES] = both.astype(BF16)


def _attention(qkv, *, batch, seq_len):
    nb = seq_len // MOBA_BLOCK
    tq = ATTN_Q_TILE
    hps = ATTN_HEADS_PER_STEP
    assert hps % 2 == 0 and 2 * ATTN_HEAD_DIM == V7X_LANES and nb <= V7X_LANES - ATTN_HEAD_DIM
    assert seq_len % GATE_ROWS == 0 and seq_len % tq == 0 and tq % MOBA_BLOCK == 0
    rows = jnp.arange(V7X_LANES, dtype=jnp.int32)[:, None] - ATTN_HEAD_DIM
    cols = jnp.arange(seq_len, dtype=jnp.int32)[None, :] // MOBA_BLOCK
    pool = jnp.where(rows == cols, 1.0 / MOBA_BLOCK, 0.0).astype(BF16)
    def head_spec(kind):
        return pl.BlockSpec((None, 1, hps, seq_len, V7X_LANES), lambda b, h, i: (kind, b, h, 0, 0))

    return pl.pallas_call(
        _attn_kernel,
        out_shape=_sds((batch, seq_len, ATTN_HEADS * ATTN_HEAD_DIM), BF16),
        grid=(batch, ATTN_HEADS // hps, seq_len // tq),
        in_specs=[head_spec(0), head_spec(1), head_spec(2),
                  pl.BlockSpec((V7X_LANES, seq_len), lambda b, h, i: (0, 0))],
        out_specs=pl.BlockSpec((1, tq, hps * ATTN_HEAD_DIM), lambda b, h, i: (b, i, h)),
        scratch_shapes=[pltpu.VMEM((hps, V7X_LANES, V7X_LANES), BF16),
                        pltpu.VMEM((hps, seq_len, V7X_LANES), BF16)],
        compiler_params=_cparams(("arbitrary", "arbitrary", "arbitrary"), 48),
        name="moba_attention",
    )(qkv, qkv, qkv, pool)


def kernel(x, c, w_ada, b_ada, ln_g, ln_b, ssm_w_in, ssm_conv_w, ssm_conv_b, ssm_dt_bias, ssm_a_log, ssm_d,
           ssm_norm_w, ssm_w_out, attn_w_qkv, attn_w_o, w_router, b_router, moe_w_gate, moe_w_up, moe_w_down):
    batch, seq_len, d = x.shape
    assert seq_len % 512 == 0 and seq_len % SSM_CHUNK == 0 and seq_len % MOBA_BLOCK == 0
    t = batch * seq_len
    mod = _ada_mod(c, w_ada, b_ada)
    xf = x.reshape(t, d)

    d_inner = SSM_GROUPS * GROUP_W
    conv_dim = d_inner + 2 * SSM_GROUPS * SSM_STATE
    w_in = ssm_w_in[0]
    heads = ssm_dt_bias.shape[1]
    w_main = w_in[:, :d_inner + conv_dim].astype(BF16)
    w_dt = jnp.pad(w_in[:, d_inner + conv_dim:], ((0, 0), (0, V7X_LANES - heads))).astype(BF16)
    proj = _mm_mod(xf, mod[0], w_main, sc=1, sh=0, seq_len=seq_len, out_dtype=BF16)
    dt_raw = _mm_mod(xf, mod[0], w_dt, sc=1, sh=0, seq_len=seq_len, out_dtype=F32)
    y = _ssd(proj, dt_raw, ssm_conv_w[0], ssm_conv_b[0], ssm_dt_bias[0], ssm_a_log[0], ssm_d[0], ssm_norm_w[0],
             batch=batch, seq_len=seq_len)
    x1, *routing = _proj_ln_route(y, ssm_w_out[0].astype(BF16), xf, mod[0], ln_g[0, 0], ln_b[0, 0],
                                       w_router, b_router, seq_len=seq_len)
    xf = _moe_layer(x1, routing, mod[0], moe_w_gate, moe_w_up, moe_w_down, ln_g[0, 1], ln_b[0, 1],
                    layer=0, seq_len=seq_len)

    assert ATTN_HEAD_DIM ** -0.5 == 0.125
    col_scale = jnp.where(jnp.arange(3 * d) < d, ATTN_HEAD_DIM ** -0.5, 1.0)
    w_qkv = (attn_w_qkv[0] * col_scale).astype(BF16)
    qkv = _qkv(xf, mod[1], w_qkv, batch=batch, seq_len=seq_len)
    o = _attention(qkv, batch=batch, seq_len=seq_len).reshape(t, d)
    x1, *routing = _proj_ln_route(o, attn_w_o[0].astype(BF16), xf, mod[1], ln_g[1, 0], ln_b[1, 0],
                                       w_router, b_router, seq_len=seq_len)
    xf = _moe_layer(x1, routing, mod[1], moe_w_gate, moe_w_up, moe_w_down, ln_g[1, 1], ln_b[1, 1],
                    layer=1, seq_len=seq_len)
    return xf.reshape(batch, seq_len, d)
```

```python
import functools

import jax
import jax.numpy as jnp
from jax import lax
from jax.experimental import pallas as pl
from jax.experimental.pallas import tpu as pltpu

F32 = jnp.float32
BF16 = jnp.bfloat16

V7X_LANES = 128
V7X_SUBLANES = 8
V7X_VMEM_BYTES = 64 * 1024 * 1024

DEPTH = 2
SSM_HEAD_DIM = 64
SSM_STATE = 128
SSM_GROUPS = 8
SSM_HEADS_PER_GROUP = 4
SSM_CONV = 4
SSM_CHUNK = 256
ATTN_HEADS = 16
ATTN_HEAD_DIM = 64
MOBA_BLOCK = 256
MOBA_TOPK = 3
N_EXPERTS = 32
N_EXPERT_GROUPS = 4
EXPERTS_PER_GROUP = N_EXPERTS // N_EXPERT_GROUPS
DEEPNORM_ALPHA = (2.0 * DEPTH) ** 0.25
LN_EPS = 1e-5
RMS_EPS = 1e-5
NEG_INF = -1e30
LOG2_E = 1.4426950408889634

GROUP_W = SSM_HEADS_PER_GROUP * SSM_HEAD_DIM
CONV_HALO = V7X_SUBLANES
MOE_ROW_TILE = 512
TOKEN_TILE = 512
DMA_UNROLL = 8


def _cparams(semantics, vmem_mib):
    assert vmem_mib * 1024 * 1024 <= V7X_VMEM_BYTES
    return pltpu.CompilerParams(dimension_semantics=semantics, vmem_limit_bytes=vmem_mib * 1024 * 1024)


def _sds(shape, dtype):
    return jax.ShapeDtypeStruct(shape, dtype)


def _dot(a, b):
    return jnp.dot(a, b, preferred_element_type=F32)


def _dot_nt(a, b):
    return lax.dot_general(a, b, (((1,), (1,)), ((), ())), preferred_element_type=F32)


def _dot_tn(a, b):
    return lax.dot_general(a, b, (((0,), (0,)), ((), ())), preferred_element_type=F32)


def _split3(a):
    hi = a.astype(BF16)
    r1 = a - hi.astype(F32)
    mid = r1.astype(BF16)
    lo = (r1 - mid.astype(F32)).astype(BF16)
    return hi, mid, lo


def _silu(x):
    h = 0.5 * x
    return h + h * jnp.tanh(h)


def _layer_norm(v, gamma, beta):
    mu = jnp.mean(v, axis=-1, keepdims=True)
    d = v - mu
    var = jnp.mean(d * d, axis=-1, keepdims=True)
    return d * lax.rsqrt(var + LN_EPS) * gamma + beta


def _ada_kernel(c_ref, w_ref, b_ref, o_ref):
    cs = _silu(c_ref[...])
    o_ref[0] = jnp.dot(cs, w_ref[0], preferred_element_type=F32, precision=lax.Precision.HIGHEST) + b_ref[0]


def _ada_mod(c, w_ada, b_ada):
    depth, d, n = w_ada.shape
    b = c.shape[0]
    tn = 1024
    out = pl.pallas_call(
        _ada_kernel,
        out_shape=_sds((depth, b, n), F32),
        grid=(depth, n // tn),
        in_specs=[pl.BlockSpec((b, d), lambda l, j: (0, 0)),
                  pl.BlockSpec((1, d, tn), lambda l, j: (l, 0, j)),
                  pl.BlockSpec((1, 1, tn), lambda l, j: (l, 0, j))],
        out_specs=pl.BlockSpec((1, b, tn), lambda l, j: (l, 0, j)),
        compiler_params=_cparams(("arbitrary", "arbitrary"), 32),
        name="ada_mod",
    )(c, w_ada, b_ada.reshape(depth, 1, n))
    return out.reshape(depth, b, 6, d)


def _modulate_into(hm_ref, x_ref, mod_ref, sc, sh):
    m = mod_ref[0]
    hm_ref[...] = (x_ref[...] * (1.0 + m[sc:sc + 1, :]) + m[sh:sh + 1, :]).astype(BF16)


def _mm_mod_kernel(x_ref, mod_ref, w_ref, o_ref, hm_ref, *, sc, sh):
    @pl.when(pl.program_id(1) == 0)
    def _():
        _modulate_into(hm_ref, x_ref, mod_ref, sc, sh)

    o_ref[...] = _dot(hm_ref[...], w_ref[...]).astype(o_ref.dtype)


def _mm_mod(x, mod, w, *, sc, sh, seq_len, out_dtype, tm=2048, tn=1536):
    t, k = x.shape
    n = w.shape[1]
    tn = min(tn, n)
    assert seq_len % tm == 0 and n % tn == 0
    tiles_per_batch = seq_len // tm
    return pl.pallas_call(
        functools.partial(_mm_mod_kernel, sc=sc, sh=sh),
        out_shape=_sds((t, n), out_dtype),
        grid=(t // tm, n // tn),
        in_specs=[pl.BlockSpec((tm, k), lambda i, j: (i, 0)),
                  pl.BlockSpec((1, 6, k), lambda i, j: (i // tiles_per_batch, 0, 0)),
                  pl.BlockSpec((k, tn), lambda i, j: (0, j))],
        out_specs=pl.BlockSpec((tm, tn), lambda i, j: (i, j)),
        scratch_shapes=[pltpu.VMEM((tm, k), BF16)],
        compiler_params=_cparams(("arbitrary", "arbitrary"), 48),
        name="mm_mod",
    )(x, mod, w)


def _ssd_kernel(z_ref, xp_ref, bcp_ref, dtr_ref, cw_ref, cb_ref, dtb_ref, alog_ref, dsk_ref, nw_ref,
                o_ref, ubuf, act, state, acum_t):
    lc = SSM_CHUNK
    d_inner = xp_ref.shape[1]
    n_slabs = ubuf.shape[0]
    x_slabs = d_inner // V7X_LANES
    chunk = pl.program_id(1)

    @pl.when(chunk == 0)
    def _():
        ubuf[:, 0:CONV_HALO, :] = jnp.zeros((n_slabs, CONV_HALO, V7X_LANES), F32)
        state[...] = jnp.zeros_like(state)

    def conv_slabs(src_ref, first_slab, count):
        def slab_body(j, carry):
            src_off = pl.multiple_of(j * V7X_LANES, V7X_LANES)
            slab = first_slab + j
            off = pl.multiple_of(slab * V7X_LANES, V7X_LANES)
            u = src_ref[:, pl.ds(src_off, V7X_LANES)].astype(F32)
            ubuf[slab, CONV_HALO:CONV_HALO + lc, :] = u
            acc = cb_ref[:, pl.ds(off, V7X_LANES)] + cw_ref[SSM_CONV - 1:SSM_CONV, pl.ds(off, V7X_LANES)] * u
            for k in range(SSM_CONV - 1):
                tap = CONV_HALO - (SSM_CONV - 1) + k
                acc = acc + cw_ref[k:k + 1, pl.ds(off, V7X_LANES)] * ubuf[slab, tap:tap + lc, :]
            act[:, pl.ds(off, V7X_LANES)] = _silu(acc)
            ubuf[slab, 0:CONV_HALO, :] = u[lc - CONV_HALO:lc, :]
            return carry

        lax.fori_loop(0, count, slab_body, 0)

    conv_slabs(xp_ref, 0, x_slabs)
    conv_slabs(bcp_ref, x_slabs, n_slabs - x_slabs)

    dt = jax.nn.softplus(dtr_ref[...] + dtb_ref[...])
    da = dt * (-LOG2_E * jnp.exp(alog_ref[...]))
    row = lax.broadcasted_iota(jnp.int32, (lc, lc), 0)
    col = lax.broadcasted_iota(jnp.int32, (lc, lc), 1)
    causal = col <= row
    tril = causal.astype(BF16)
    acum = sum(_dot(tril, part) for part in _split3(da))
    acum_t[...] = acum.T
    acum_parts = _split3(acum)
    dt_bf16 = dt.astype(BF16)
    head_of_col = lax.broadcasted_iota(jnp.int32, (lc, GROUP_W), 1) // SSM_HEAD_DIM
    sel_row = lax.broadcasted_iota(jnp.int32, (V7X_LANES, GROUP_W), 0)
    sel_col = lax.broadcasted_iota(jnp.int32, (V7X_LANES, GROUP_W), 1) // SSM_HEAD_DIM

    def group_body(g, carry):
        xo = pl.multiple_of(g * GROUP_W, GROUP_W)
        bo = pl.multiple_of(d_inner + g * SSM_STATE, SSM_STATE)
        co = pl.multiple_of(d_inner + SSM_GROUPS * SSM_STATE + g * SSM_STATE, SSM_STATE)
        x_g = act[:, pl.ds(xo, GROUP_W)]
        b_g = act[:, pl.ds(bo, SSM_STATE)].astype(BF16)
        c_g = act[:, pl.ds(co, SSM_STATE)].astype(BF16)
        sel = (sel_row == SSM_HEADS_PER_GROUP * g + sel_col).astype(BF16)
        ab = sum(_dot(part, sel) for part in acum_parts)
        dtb = _dot(dt_bf16, sel)
        alast = ab[lc - 1:lc, :]
        cb = _dot_nt(c_g, b_g)
        xdt = x_g * dtb
        y = jnp.zeros((lc, GROUP_W), F32)
        for r in range(SSM_HEADS_PER_GROUP):
            acol = ab[:, r * SSM_HEAD_DIM:r * SSM_HEAD_DIM + 1]
            arow = acum_t[pl.ds(SSM_HEADS_PER_GROUP * g + r, 1), :]
            decay = jnp.exp2(jnp.where(causal, acol - arow, -jnp.inf))
            w = (cb * decay).astype(BF16)
            x_r = jnp.where(head_of_col == r, xdt, 0.0).astype(BF16)
            y = y + _dot(w, x_r)
        st = state[:, pl.ds(xo, GROUP_W)]
        y = y + _dot(c_g, st.astype(BF16)) * jnp.exp2(ab)
        to_end = jnp.exp2(alast - ab) * dtb
        xw = (x_g * to_end).astype(BF16)
        state[:, pl.ds(xo, GROUP_W)] = st * jnp.exp2(alast) + _dot_tn(b_g, xw)
        y = y + dsk_ref[:, pl.ds(xo, GROUP_W)] * x_g
        gated = y * _silu(z_ref[:, pl.ds(xo, GROUP_W)].astype(F32))
        ms = jnp.mean(gated * gated, axis=-1, keepdims=True)
        o_ref[:, pl.ds(xo, GROUP_W)] = (gated * lax.rsqrt(ms + RMS_EPS) * nw_ref[:, pl.ds(xo, GROUP_W)]).astype(BF16)
        return carry

    lax.fori_loop(0, SSM_GROUPS, group_body, 0, unroll=4)


def _ssd(proj, dt_raw, conv_w, conv_b, dt_bias, a_log, d_skip, norm_w, *, batch, seq_len):
    t = proj.shape[0]
    d_inner = SSM_GROUPS * GROUP_W
    conv_dim = d_inner + 2 * SSM_GROUPS * SSM_STATE
    assert proj.shape[1] == d_inner + conv_dim and conv_dim == 2 * d_inner
    nc = seq_len // SSM_CHUNK
    lc = SSM_CHUNK
    pad = V7X_LANES - dt_bias.shape[0]
    heads = dt_bias.shape[0]
    row_map = lambda b, c: b * nc + c
    const = lambda b, c: (0, 0)
    return pl.pallas_call(
        _ssd_kernel,
        out_shape=_sds((t, d_inner), BF16),
        grid=(batch, nc),
        in_specs=[pl.BlockSpec((lc, d_inner), lambda b, c: (row_map(b, c), 0)),
                  pl.BlockSpec((lc, d_inner), lambda b, c: (row_map(b, c), 1)),
                  pl.BlockSpec((lc, d_inner), lambda b, c: (row_map(b, c), 2)),
                  pl.BlockSpec((lc, V7X_LANES), lambda b, c: (row_map(b, c), 0)),
                  pl.BlockSpec((SSM_CONV, conv_dim), const),
                  pl.BlockSpec((1, conv_dim), const),
                  pl.BlockSpec((1, V7X_LANES), const),
                  pl.BlockSpec((1, V7X_LANES), const),
                  pl.BlockSpec((1, d_inner), const),
                  pl.BlockSpec((1, d_inner), const)],
        out_specs=pl.BlockSpec((lc, d_inner), lambda b, c: (row_map(b, c), 0)),
        scratch_shapes=[pltpu.VMEM((conv_dim // V7X_LANES, CONV_HALO + lc, V7X_LANES), F32),
                        pltpu.VMEM((lc, conv_dim), F32),
                        pltpu.VMEM((SSM_STATE, d_inner), F32),
                        pltpu.VMEM((V7X_LANES, lc), F32)],
        compiler_params=_cparams(("arbitrary", "arbitrary"), 48),
        name="ssd_chunk_scan",
    )(proj, proj, proj, dt_raw, conv_w, conv_b.reshape(1, conv_dim),
      jnp.pad(dt_bias, (0, pad)).reshape(1, V7X_LANES), jnp.pad(a_log, (0, pad)).reshape(1, V7X_LANES),
      jnp.repeat(d_skip, SSM_HEAD_DIM).reshape(1, heads * SSM_HEAD_DIM), norm_w.reshape(1, d_inner))


def _route(logits_t, carry_ref):
    tm = logits_t.shape[1]
    lg = logits_t[0:N_EXPERTS, :]
    eid = lax.broadcasted_iota(jnp.int32, (N_EXPERTS, tm), 0).astype(F32)
    e = jnp.exp(lg - jnp.max(lg, axis=0, keepdims=True))
    far = float(N_EXPERTS)
    best = None
    for g in range(N_EXPERT_GROUPS):
        eg = e[g * EXPERTS_PER_GROUP:(g + 1) * EXPERTS_PER_GROUP, :]
        ig = (lax.broadcasted_iota(jnp.int32, (EXPERTS_PER_GROUP, tm), 0) + g * EXPERTS_PER_GROUP).astype(F32)
        m1 = jnp.max(eg, axis=0, keepdims=True)
        i1 = jnp.min(jnp.where(eg == m1, ig, far), axis=0, keepdims=True)
        eg2 = jnp.where(ig == i1, -1.0, eg)
        m2 = jnp.max(eg2, axis=0, keepdims=True)
        i2 = jnp.min(jnp.where(eg2 == m2, ig, far), axis=0, keepdims=True)
        cand = (m1 + m2, m1, m2, i1, i2)
        if best is None:
            best = cand
        else:
            take = cand[0] > best[0]
            best = tuple(jnp.where(take, c, b) for c, b in zip(cand, best))
    _, m1, m2, i1, i2 = best
    denom = m1 + m2
    hit1 = eid == i1
    hit2 = eid == i2
    onehot = (hit1 | hit2).astype(F32)
    row = lax.broadcasted_iota(jnp.int32, (tm, tm), 0)
    col = lax.broadcasted_iota(jnp.int32, (tm, tm), 1)
    earlier = (row < col).astype(BF16)
    rank = _dot(onehot.astype(BF16), earlier) + carry_ref[:, 0:1]
    r1 = jnp.sum(jnp.where(hit1, rank, 0.0), axis=0, keepdims=True)
    r2 = jnp.sum(jnp.where(hit2, rank, 0.0), axis=0, keepdims=True)
    carry_ref[...] = carry_ref[...] + jnp.sum(onehot, axis=1, keepdims=True)
    sub = lax.broadcasted_iota(jnp.int32, (V7X_SUBLANES, tm), 0)
    rec_t = jnp.zeros((V7X_SUBLANES, tm), F32)
    for k, val in enumerate((i1, i2, m1 / denom, m2 / denom, r1, r2)):
        rec_t = jnp.where(sub == k, val, rec_t)
    padded = jnp.concatenate([rec_t, jnp.zeros((V7X_LANES - V7X_SUBLANES, tm), F32)], axis=0)
    return padded.T, rec_t


def _proj_ln_route_kernel(a_ref, w_ref, xres_ref, mod_ref, lng_ref, lnb_ref, wr_ref, br_ref,
                          x_ref, route_ref, route_t_ref, cnt_ref, carry_ref, *, gate_idx, sc, sh):
    @pl.when(pl.program_id(0) == 0)
    def _():
        carry_ref[...] = jnp.zeros_like(carry_ref)

    m = mod_ref[0]
    y = _dot(a_ref[...], w_ref[...])
    v = DEEPNORM_ALPHA * xres_ref[...] + (1.0 + m[gate_idx:gate_idx + 1, :]) * y
    x1 = _layer_norm(v, lng_ref[...], lnb_ref[...])
    x_ref[...] = x1
    hm = (x1 * (1.0 + m[sc:sc + 1, :]) + m[sh:sh + 1, :]).astype(BF16)
    logits_t = _dot_nt(wr_ref[...], hm) + br_ref[...]
    route_ref[...], route_t_ref[...] = _route(logits_t, carry_ref)
    cnt_ref[...] = carry_ref[...]


def _proj_ln_route(a, w, xres, mod, ln_g, ln_b, w_router, b_router, *, seq_len):
    t, k = a.shape
    d = w.shape[1]
    tm = TOKEN_TILE
    tiles_per_batch = seq_len // tm
    const = lambda i: (0, 0)
    wr = jnp.pad(w_router.T, ((0, V7X_LANES - N_EXPERTS), (0, 0))).astype(BF16)
    br = jnp.pad(b_router, (0, V7X_LANES - N_EXPERTS)).reshape(V7X_LANES, 1)
    return pl.pallas_call(
        functools.partial(_proj_ln_route_kernel, gate_idx=2, sc=4, sh=3),
        out_shape=(_sds((t, d), F32), _sds((t, V7X_LANES), F32), _sds((V7X_SUBLANES, t), F32),
                   _sds((N_EXPERTS, V7X_LANES), F32)),
        grid=(t // tm,),
        in_specs=[pl.BlockSpec((tm, k), lambda i: (i, 0)),
                  pl.BlockSpec((k, d), const),
                  pl.BlockSpec((tm, d), lambda i: (i, 0)),
                  pl.BlockSpec((1, 6, d), lambda i: (i // tiles_per_batch, 0, 0)),
                  pl.BlockSpec((1, d), const),
                  pl.BlockSpec((1, d), const),
                  pl.BlockSpec((V7X_LANES, d), const),
                  pl.BlockSpec((V7X_LANES, 1), const)],
        out_specs=(pl.BlockSpec((tm, d), lambda i: (i, 0)),
                   pl.BlockSpec((tm, V7X_LANES), lambda i: (i, 0)),
                   pl.BlockSpec((V7X_SUBLANES, tm), lambda i: (0, i)),
                   pl.BlockSpec((N_EXPERTS, V7X_LANES), const)),
        scratch_shapes=[pltpu.VMEM((N_EXPERTS, V7X_LANES), F32)],
        compiler_params=_cparams(("arbitrary",), 40),
        name="proj_ln_route",
    )(a, w, xres, mod, ln_g.reshape(1, d), ln_b.reshape(1, d), wr, br)


def _record_copy(src_ref, src_slot, dst_ref, dst_slot, sem, n):
    src = src_ref.at[pl.ds(pl.multiple_of(src_slot * n, n), n)]
    dst = dst_ref.at[pl.ds(pl.multiple_of(dst_slot * n, n), n)]
    return pltpu.make_async_copy(src, dst, sem)


def _to_records(rec_ref, rows):
    t = rows.shape[0]
    n = rows.shape[1] // V7X_LANES
    for s in range(n):
        rec_ref[pl.ds(s, t, stride=n), :] = rows[:, s * V7X_LANES:(s + 1) * V7X_LANES]


def _from_records(rec_ref, t, n):
    return jnp.concatenate([rec_ref[pl.ds(s, t, stride=n), :] for s in range(n)], axis=1)


def _dispatch_kernel(tail_ref, p1_ref, p2_ref, x_ref, mod_ref, xs_ref, hm_ref, zero_ref, sem, zero_sem, *, sc, sh):
    tt, d = x_ref.shape
    n = d // V7X_LANES
    i = pl.program_id(0)
    n_steps = pl.num_programs(0)
    cur = lax.rem(i, 2)

    @pl.when(i == 0)
    def _():
        zero_ref[...] = jnp.zeros_like(zero_ref)
        rows = zero_ref.shape[0]

        def tail_copy(e):
            start = pl.multiple_of(jnp.maximum(tail_ref[0, e], 0) * n, rows)
            return pltpu.make_async_copy(zero_ref, xs_ref.at[pl.ds(start, rows)], zero_sem)

        def fill(e, carry):
            @pl.when(tail_ref[0, e] >= 0)
            def _():
                tail_copy(e).start()
            return carry

        def fill_wait(e, carry):
            @pl.when(tail_ref[0, e] >= 0)
            def _():
                tail_copy(e).wait()
            return carry

        def spare_copy(j):
            return pltpu.make_async_copy(zero_ref, xs_ref.at[pl.ds(pl.multiple_of(j * rows, rows), rows)], zero_sem)

        def spare_fill(j, carry):
            spare_copy(j).start()
            return carry

        def spare_wait(j, carry):
            spare_copy(j).wait()
            return carry

        n_tiles = xs_ref.shape[0] // rows
        lax.fori_loop(0, N_EXPERTS, fill, 0)
        lax.fori_loop(tail_ref[0, N_EXPERTS], n_tiles, spare_fill, 0)
        lax.fori_loop(0, N_EXPERTS, fill_wait, 0)
        lax.fori_loop(tail_ref[0, N_EXPERTS], n_tiles, spare_wait, 0)

    def drain(buf):
        def body(blk, carry):
            for _ in range(2 * DMA_UNROLL):
                _record_copy(hm_ref.at[buf], 0, xs_ref, 0, sem.at[buf], n).wait()
            return carry

        lax.fori_loop(0, tt // DMA_UNROLL, body, 0)

    @pl.when(i >= 2)
    def _():
        drain(cur)

    m = mod_ref[0]
    _to_records(hm_ref.at[cur], x_ref[...] * (1.0 + m[sc:sc + 1, :]) + m[sh:sh + 1, :])

    def issue(blk, carry):
        for u in range(DMA_UNROLL):
            r = blk * DMA_UNROLL + u
            _record_copy(hm_ref.at[cur], r, xs_ref, p1_ref[0, r], sem.at[cur], n).start(priority=0)
            _record_copy(hm_ref.at[cur], r, xs_ref, p2_ref[0, r], sem.at[cur], n).start(priority=1)
        return carry

    lax.fori_loop(0, tt // DMA_UNROLL, issue, 0)

    @pl.when(i == n_steps - 1)
    def _():
        drain(cur)

    @pl.when(jnp.logical_and(i == n_steps - 1, i >= 1))
    def _():
        drain(1 - cur)


def _slot_spec(tt, copy):
    return pl.BlockSpec((None, None, 1, tt), lambda i: (i, copy, 0, 0), memory_space=pltpu.SMEM)


def _dispatch(x, mod, slots, tail_slot, n_slots, *, seq_len):
    t, d = x.shape
    n = d // V7X_LANES
    tt = TOKEN_TILE
    tiles_per_batch = seq_len // tt
    return pl.pallas_call(
        functools.partial(_dispatch_kernel, sc=4, sh=3),
        out_shape=_sds((n_slots * n, V7X_LANES), F32),
        grid=(t // tt,),
        in_specs=[pl.BlockSpec((1, N_EXPERTS + 1), lambda i: (0, 0), memory_space=pltpu.SMEM),
                  _slot_spec(tt, 0), _slot_spec(tt, 1),
                  pl.BlockSpec((tt, d), lambda i: (i, 0)),
                  pl.BlockSpec((1, 6, d), lambda i: (i // tiles_per_batch, 0, 0))],
        out_specs=pl.BlockSpec(memory_space=pl.ANY),
        scratch_shapes=[pltpu.VMEM((2, tt * n, V7X_LANES), F32), pltpu.VMEM((MOE_ROW_TILE * n, V7X_LANES), F32),
                        pltpu.SemaphoreType.DMA((2,)), pltpu.SemaphoreType.DMA(())],
        compiler_params=_cparams(("arbitrary",), 32),
        name="moe_dispatch",
    )(tail_slot.reshape(1, N_EXPERTS + 1), slots, slots, x, mod)


def _ffn_kernel(te_ref, nu_ref, x_ref, wg_ref, wu_ref, wd_ref, o_ref, wg_bf, wu_bf, wd_bf):
    i = pl.program_id(0)
    d, f = wg_bf.shape
    n = d // V7X_LANES
    tm = x_ref.shape[0] // n
    used = i < nu_ref[0]
    new_expert = jnp.logical_or(i == 0, te_ref[i] != te_ref[jnp.maximum(i - 1, 0)])

    @pl.when(jnp.logical_and(used, new_expert))
    def _():
        wg_bf[...] = wg_ref[0, 0].astype(BF16)
        wu_bf[...] = wu_ref[0, 0].astype(BF16)
        wd_bf[...] = wd_ref[0, 0].astype(BF16)

    @pl.when(used)
    def _():
        x = _from_records(x_ref, tm, n).astype(BF16)
        hg = _dot(x, wg_bf[...])
        hu = _dot(x, wu_bf[...])
        _to_records(o_ref, _dot((_silu(hg) * hu).astype(BF16), wd_bf[...]))

    @pl.when(jnp.logical_not(used))
    def _():
        o_ref[...] = jnp.zeros_like(o_ref)


def _ffn(xs, tile_expert, n_used, w_gate, w_up, w_down, *, layer):
    _, _, d, f = w_gate.shape
    n = d // V7X_LANES
    tm = MOE_ROW_TILE
    w_map = lambda i, te, nu: (layer, te[i], 0, 0)
    grid_spec = pltpu.PrefetchScalarGridSpec(
        num_scalar_prefetch=2,
        grid=(xs.shape[0] // (tm * n),),
        in_specs=[pl.BlockSpec((tm * n, V7X_LANES), lambda i, te, nu: (jnp.minimum(i, nu[0] - 1), 0)),
                  pl.BlockSpec((1, 1, d, f), w_map),
                  pl.BlockSpec((1, 1, d, f), w_map),
                  pl.BlockSpec((1, 1, f, d), w_map)],
        out_specs=pl.BlockSpec((tm * n, V7X_LANES), lambda i, te, nu: (i, 0)),
        scratch_shapes=[pltpu.VMEM((d, f), BF16), pltpu.VMEM((d, f), BF16), pltpu.VMEM((f, d), BF16)],
    )
    return pl.pallas_call(
        _ffn_kernel,
        out_shape=_sds(xs.shape, F32),
        grid_spec=grid_spec,
        compiler_params=_cparams(("arbitrary",), 48),
        name="moe_ffn",
    )(tile_expert, n_used, xs, w_gate, w_up, w_down)


def _combine_ln_kernel(p1_ref, p2_ref, p1_next_ref, p2_next_ref, xres_ref, route_ref, mod_ref, lng_ref, lnb_ref,
                       ye_ref, o_ref, buf, sem, *, gate_idx):
    tt, d = xres_ref.shape
    n = d // V7X_LANES
    i = pl.program_id(0)
    cur = lax.rem(i, 2)

    def fetch(pa_ref, pb_ref, slot):
        def body(blk, carry):
            for u in range(DMA_UNROLL):
                r = blk * DMA_UNROLL + u
                _record_copy(ye_ref, pa_ref[0, r], buf.at[slot, 0], r, sem.at[slot], n).start(priority=0)
                _record_copy(ye_ref, pb_ref[0, r], buf.at[slot, 1], r, sem.at[slot], n).start(priority=1)
            return carry

        lax.fori_loop(0, tt // DMA_UNROLL, body, 0)

    @pl.when(i == 0)
    def _():
        fetch(p1_ref, p2_ref, 0)

    @pl.when(i + 1 < pl.num_programs(0))
    def _():
        fetch(p1_next_ref, p2_next_ref, 1 - cur)

    def drain(blk, carry):
        for _ in range(2 * DMA_UNROLL):
            _record_copy(ye_ref, 0, buf.at[cur, 0], 0, sem.at[cur], n).wait()
        return carry

    lax.fori_loop(0, tt // DMA_UNROLL, drain, 0)
    m = mod_ref[0]
    rec = route_ref[...]
    y = (rec[:, 2:3] * _from_records(buf.at[cur, 0], tt, n) + rec[:, 3:4] * _from_records(buf.at[cur, 1], tt, n))
    v = DEEPNORM_ALPHA * xres_ref[...] + (1.0 + m[gate_idx:gate_idx + 1, :]) * y
    o_ref[...] = _layer_norm(v, lng_ref[...], lnb_ref[...])


def _combine_ln(xres, route, slots, ye, mod, ln_g, ln_b, *, seq_len):
    t, d = xres.shape
    n = d // V7X_LANES
    tt = TOKEN_TILE
    tiles_per_batch = seq_len // tt
    n_steps = t // tt
    const = lambda i: (0, 0)

    def next_slot_spec(copy):
        return pl.BlockSpec((None, None, 1, tt), lambda i: (jnp.minimum(i + 1, n_steps - 1), copy, 0, 0),
                            memory_space=pltpu.SMEM)

    return pl.pallas_call(
        functools.partial(_combine_ln_kernel, gate_idx=5),
        out_shape=_sds((t, d), F32),
        grid=(n_steps,),
        in_specs=[_slot_spec(tt, 0), _slot_spec(tt, 1), next_slot_spec(0), next_slot_spec(1),
                  pl.BlockSpec((tt, d), lambda i: (i, 0)),
                  pl.BlockSpec((tt, V7X_LANES), lambda i: (i, 0)),
                  pl.BlockSpec((1, 6, d), lambda i: (i // tiles_per_batch, 0, 0)),
                  pl.BlockSpec((1, d), const),
                  pl.BlockSpec((1, d), const),
                  pl.BlockSpec(memory_space=pl.ANY)],
        out_specs=pl.BlockSpec((tt, d), lambda i: (i, 0)),
        scratch_shapes=[pltpu.VMEM((2, 2, tt * n, V7X_LANES), F32), pltpu.SemaphoreType.DMA((2,))],
        compiler_params=_cparams(("arbitrary",), 32),
        name="moe_combine_ln",
    )(slots, slots, slots, slots, xres, route, mod, ln_g.reshape(1, d), ln_b.reshape(1, d), ye)


def _slot_kernel(route_t_ref, off_ref, o_ref):
    tt = o_ref.shape[-1]
    n = route_t_ref.shape[1]
    eid = lax.broadcasted_iota(jnp.int32, (N_EXPERTS, n), 0).astype(F32)
    for copy in range(2):
        expert = route_t_ref[copy:copy + 1, :]
        rank = route_t_ref[4 + copy:5 + copy, :]
        off = jnp.sum(jnp.where(eid == expert, off_ref[:, 0:1], 0.0), axis=0, keepdims=True)
        slot = (off + rank).astype(jnp.int32)
        for s in range(o_ref.shape[0]):
            o_ref[s, copy] = slot[:, s * tt:(s + 1) * tt]


def _slots(route_t, row_off):
    t = route_t.shape[1]
    tt = TOKEN_TILE
    tiles_per_step = 4
    assert t % (tt * tiles_per_step) == 0
    off = jnp.broadcast_to(row_off.astype(F32)[:, None], (N_EXPERTS, V7X_LANES))
    return pl.pallas_call(
        _slot_kernel,
        out_shape=_sds((t // tt, 2, 1, tt), jnp.int32),
        grid=(t // (tt * tiles_per_step),),
        in_specs=[pl.BlockSpec((V7X_SUBLANES, tt * tiles_per_step), lambda i: (0, i)),
                  pl.BlockSpec((N_EXPERTS, V7X_LANES), lambda i: (0, 0))],
        out_specs=pl.BlockSpec((tiles_per_step, 2, 1, tt), lambda i: (i, 0, 0, 0)),
        compiler_params=_cparams(("arbitrary",), 32),
        name="moe_slots",
    )(route_t, off)


def _moe_tables(counts, n_tokens):
    tm = MOE_ROW_TILE
    max_tiles = (2 * n_tokens) // tm + N_EXPERTS
    cnt = counts[:, 0].astype(jnp.int32)
    tiles_e = (cnt + tm - 1) // tm
    tile_end = jnp.cumsum(tiles_e)
    row_off = (tile_end - tiles_e) * tm
    n_used = tile_end[-1:]
    tail_slot = jnp.concatenate([jnp.where(tiles_e > 0, (tile_end - 1) * tm, -1), n_used]).astype(jnp.int32)
    tile_ids = jnp.minimum(jnp.arange(max_tiles, dtype=jnp.int32), n_used - 1)
    tile_expert = jnp.sum(tile_ids[:, None] >= tile_end[None, :], axis=1).astype(jnp.int32)
    return row_off, tail_slot, tile_expert, n_used.astype(jnp.int32), max_tiles * tm


def _moe_layer(x1, routing, mod, w_gate, w_up, w_down, ln_g, ln_b, *, layer, seq_len):
    route, route_t, counts = routing
    row_off, tail_slot, tile_expert, n_used, n_slots = _moe_tables(counts, route.shape[0])
    slots = _slots(route_t, row_off)
    xs = _dispatch(x1, mod, slots, tail_slot, n_slots, seq_len=seq_len)
    ye = _ffn(xs, tile_expert, n_used, w_gate, w_up, w_down, layer=layer)
    return _combine_ln(x1, route, slots, ye, mod, ln_g, ln_b, seq_len=seq_len)


def _qkv_body(x_ref, mod_ref, w_ref, o_ref, hm_ref, *, sc, sh, tiles_per_batch, tiles_per_kind):
    @pl.when(pl.program_id(1) == 0)
    def _():
        _modulate_into(hm_ref, x_ref, mod_ref, sc, sh)

    tm = x_ref.shape[0]
    kind = pl.program_id(1) // tiles_per_kind
    lane = lax.broadcasted_iota(jnp.int32, (tm, V7X_LANES), 1)
    low = lane < ATTN_HEAD_DIM
    row = lax.broadcasted_iota(jnp.int32, (tm, V7X_LANES), 0)
    pos = lax.rem(pl.program_id(0), tiles_per_batch) * tm + row
    k_extra = (lane == ATTN_HEAD_DIM + lax.div(pos, MOBA_BLOCK)).astype(F32)
    v_extra = (lane == ATTN_HEAD_DIM).astype(F32)
    extra = jnp.where(kind == 1, k_extra, jnp.where(kind == 2, v_extra, 0.0))
    half = w_ref.shape[1] // 2
    for c in range(2):
        acc = _dot(hm_ref[...], w_ref[:, c * half:(c + 1) * half])
        for p in range(half // V7X_LANES):
            t2 = acc[:, p * V7X_LANES:(p + 1) * V7X_LANES]
            even = jnp.where(low, t2, extra)
            odd = jnp.where(low, pltpu.roll(t2, ATTN_HEAD_DIM, axis=1), extra)
            head = 2 * (c * (half // V7X_LANES) + p)
            o_ref[0, 0, head] = even.astype(BF16)
            o_ref[0, 0, head + 1] = odd.astype(BF16)


def _qkv(x, mod, w, *, batch, seq_len, tm=1024, tn=512):
    t, k = x.shape
    n = w.shape[1]
    heads_per_tile = tn // ATTN_HEAD_DIM
    tiles_per_batch = seq_len // tm
    tiles_per_kind = ATTN_HEADS // heads_per_tile
    assert n == 3 * ATTN_HEADS * ATTN_HEAD_DIM and seq_len % tm == 0
    return pl.pallas_call(
        functools.partial(_qkv_body, sc=1, sh=0, tiles_per_batch=tiles_per_batch, tiles_per_kind=tiles_per_kind),
        out_shape=_sds((3, batch, ATTN_HEADS, seq_len, V7X_LANES), BF16),
        grid=(t // tm, n // tn),
        in_specs=[pl.BlockSpec((tm, k), lambda i, j: (i, 0)),
                  pl.BlockSpec((1, 6, k), lambda i, j: (i // tiles_per_batch, 0, 0)),
                  pl.BlockSpec((k, tn), lambda i, j: (0, j))],
        out_specs=pl.BlockSpec((1, 1, heads_per_tile, tm, V7X_LANES),
                               lambda i, j: (j // tiles_per_kind, i // tiles_per_batch, j % tiles_per_kind,
                                             i % tiles_per_batch, 0)),
        scratch_shapes=[pltpu.VMEM((tm, k), BF16)],
        compiler_params=_cparams(("arbitrary", "arbitrary"), 40),
        name="qkv_proj",
    )(x, mod, w)


ATTN_HEADS_PER_STEP = 2
ATTN_Q_TILE = 1024
GATE_ROWS = 512


def _attn_kernel(q_ref, k_ref, v_ref, pm_ref, o_ref, km_ref, qa_ref):
    tq = ATTN_Q_TILE
    seq_len = q_ref.shape[2]
    n_blocks = seq_len // MOBA_BLOCK
    heads = range(ATTN_HEADS_PER_STEP)
    qi = pl.program_id(2)

    @pl.when(qi == 0)
    def _():
        for hh in heads:
            km_ref[hh] = _dot(pm_ref[...], k_ref[0, hh]).astype(BF16)
        nb_pad = -(-n_blocks // V7X_SUBLANES) * V7X_SUBLANES
        blk_t = lax.broadcasted_iota(jnp.int32, (nb_pad, GATE_ROWS), 0)
        pos_t = lax.broadcasted_iota(jnp.int32, (nb_pad, GATE_ROWS), 1)

        def gate_rows(ci, carry, *, select):
            r0 = pl.multiple_of(ci * GATE_ROWS, GATE_ROWS)
            own_blk = lax.div(r0 + pos_t, MOBA_BLOCK)
            past = blk_t < own_blk
            for hh in heads:
                q = q_ref[0, hh, pl.ds(r0, GATE_ROWS), :]
                chosen = past
                if select:
                    gates = _dot_nt(km_ref[hh], q)[ATTN_HEAD_DIM:ATTN_HEAD_DIM + nb_pad, :]
                    gates = jnp.where(past, gates, -jnp.inf)
                    rank = jnp.zeros((nb_pad, GATE_ROWS), F32)
                    for i in range(n_blocks):
                        g_i = gates[i:i + 1, :]
                        beats = (g_i > gates) | ((g_i == gates) & (blk_t > i))
                        rank = rank + beats.astype(F32)
                    chosen = past & (rank < MOBA_TOPK)
                mask_t = jnp.where(chosen | (blk_t == own_blk), 0.0, NEG_INF)
                mask_t = jnp.concatenate([jnp.zeros((ATTN_HEAD_DIM, GATE_ROWS), F32), mask_t,
                                          jnp.zeros((V7X_LANES - ATTN_HEAD_DIM - nb_pad, GATE_ROWS), F32)], axis=0)
                qa_ref[hh, pl.ds(r0, GATE_ROWS), :] = (q.astype(F32) + mask_t.T).astype(BF16)
            return carry

        keep_all = min(seq_len, (MOBA_TOPK + 1) * MOBA_BLOCK) // GATE_ROWS
        lax.fori_loop(0, keep_all, functools.partial(gate_rows, select=False), 0)
        lax.fori_loop(keep_all, seq_len // GATE_ROWS, functools.partial(gate_rows, select=True), 0)

    q0 = pl.multiple_of(qi * tq, tq)

    def online_update(m_i, acc, s, v):
        m_new = jnp.maximum(m_i, jnp.max(s, axis=-1, keepdims=True))
        p = jnp.exp(s - m_new)
        return m_new, acc * jnp.exp(m_i - m_new) + _dot(p.astype(BF16), v)

    def kv_step(j, carry):
        start = pl.multiple_of(j * tq, tq)
        out = []
        for hh in heads:
            m_i, acc = carry[hh]
            s = _dot_nt(qa_ref[hh, pl.ds(q0, tq), :], k_ref[0, hh, pl.ds(start, tq), :])
            out.append(online_update(m_i, acc, s, v_ref[0, hh, pl.ds(start, tq), :]))
        return tuple(out)

    init = tuple((jnp.full((tq, 1), NEG_INF, F32), jnp.zeros((tq, V7X_LANES), F32)) for _ in heads)
    carry = lax.fori_loop(0, qi, kv_step, init)

    row = lax.broadcasted_iota(jnp.int32, (tq, tq), 0)
    col = lax.broadcasted_iota(jnp.int32, (tq, tq), 1)
    outs = []
    for hh in heads:
        m_i, acc = carry[hh]
        s = _dot_nt(qa_ref[hh, pl.ds(q0, tq), :], k_ref[0, hh, pl.ds(q0, tq), :])
        _, acc = online_update(m_i, acc, jnp.where(col <= row, s, NEG_INF), v_ref[0, hh, pl.ds(q0, tq), :])
        outs.append(acc / acc[:, ATTN_HEAD_DIM:ATTN_HEAD_DIM + 1])
    lane = lax.broadcasted_iota(jnp.int32, (tq, V7X_LANES), 1)
    for pair in range(ATTN_HEADS_PER_STEP // 2):
        both = jnp.where(lane < ATTN_HEAD_DIM, outs[2 * pair], pltpu.roll(outs[2 * pair + 1], ATTN_HEAD_DIM, axis=1))
        o_ref[0, :, pair * V7X_LANES:(pair + 1) * V7X_LANES] = both.astype(BF16)


def _attention(qkv, *, batch, seq_len):
    nb = seq_len // MOBA_BLOCK
    tq = ATTN_Q_TILE
    hps = ATTN_HEADS_PER_STEP
    assert hps % 2 == 0 and 2 * ATTN_HEAD_DIM == V7X_LANES and nb <= V7X_LANES - ATTN_HEAD_DIM
    assert seq_len % GATE_ROWS == 0 and seq_len % tq == 0 and tq % MOBA_BLOCK == 0
    rows = jnp.arange(V7X_LANES, dtype=jnp.int32)[:, None] - ATTN_HEAD_DIM
    cols = jnp.arange(seq_len, dtype=jnp.int32)[None, :] // MOBA_BLOCK
    pool = jnp.where(rows == cols, 1.0 / MOBA_BLOCK, 0.0).astype(BF16)
    def head_spec(kind):
        return pl.BlockSpec((None, 1, hps, seq_len, V7X_LANES), lambda b, h, i: (kind, b, h, 0, 0))

    return pl.pallas_call(
        _attn_kernel,
        out_shape=_sds((batch, seq_len, ATTN_HEADS * ATTN_HEAD_DIM), BF16),
        grid=(batch, ATTN_HEADS // hps, seq_len // tq),
        in_specs=[head_spec(0), head_spec(1), head_spec(2),
                  pl.BlockSpec((V7X_LANES, seq_len), lambda b, h, i: (0, 0))],
        out_specs=pl.BlockSpec((1, tq, hps * ATTN_HEAD_DIM), lambda b, h, i: (b, i, h)),
        scratch_shapes=[pltpu.VMEM((hps, V7X_LANES, V7X_LANES), BF16),
                        pltpu.VMEM((hps, seq_len, V7X_LANES), BF16)],
        compiler_params=_cparams(("arbitrary", "arbitrary", "arbitrary"), 48),
        name="moba_attention",
    )(qkv, qkv, qkv, pool)


def kernel(x, c, w_ada, b_ada, ln_g, ln_b, ssm_w_in, ssm_conv_w, ssm_conv_b, ssm_dt_bias, ssm_a_log, ssm_d,
           ssm_norm_w, ssm_w_out, attn_w_qkv, attn_w_o, w_router, b_router, moe_w_gate, moe_w_up, moe_w_down):
    batch, seq_len, d = x.shape
    assert seq_len % 512 == 0 and seq_len % SSM_CHUNK == 0 and seq_len % MOBA_BLOCK == 0
    t = batch * seq_len
    mod = _ada_mod(c, w_ada, b_ada)
    xf = x.reshape(t, d)

    d_inner = SSM_GROUPS * GROUP_W
    conv_dim = d_inner + 2 * SSM_GROUPS * SSM_STATE
    w_in = ssm_w_in[0]
    heads = ssm_dt_bias.shape[1]
    w_main = w_in[:, :d_inner + conv_dim].astype(BF16)
    w_dt = jnp.pad(w_in[:, d_inner + conv_dim:], ((0, 0), (0, V7X_LANES - heads))).astype(BF16)
    proj = _mm_mod(xf, mod[0], w_main, sc=1, sh=0, seq_len=seq_len, out_dtype=BF16)
    dt_raw = _mm_mod(xf, mod[0], w_dt, sc=1, sh=0, seq_len=seq_len, out_dtype=F32)
    y = _ssd(proj, dt_raw, ssm_conv_w[0], ssm_conv_b[0], ssm_dt_bias[0], ssm_a_log[0], ssm_d[0], ssm_norm_w[0],
             batch=batch, seq_len=seq_len)
    x1, *routing = _proj_ln_route(y, ssm_w_out[0].astype(BF16), xf, mod[0], ln_g[0, 0], ln_b[0, 0],
                                       w_router, b_router, seq_len=seq_len)
    xf = _moe_layer(x1, routing, mod[0], moe_w_gate, moe_w_up, moe_w_down, ln_g[0, 1], ln_b[0, 1],
                    layer=0, seq_len=seq_len)

    assert ATTN_HEAD_DIM ** -0.5 == 0.125
    col_scale = jnp.where(jnp.arange(3 * d) < d, ATTN_HEAD_DIM ** -0.5, 1.0)
    w_qkv = (attn_w_qkv[0] * col_scale).astype(BF16)
    qkv = _qkv(xf, mod[1], w_qkv, batch=batch, seq_len=seq_len)
    o = _attention(qkv, batch=batch, seq_len=seq_len).reshape(t, d)
    x1, *routing = _proj_ln_route(o, attn_w_o[0].astype(BF16), xf, mod[1], ln_g[1, 0], ln_b[1, 0],
                                       w_router, b_router, seq_len=seq_len)
    xf = _moe_layer(x1, routing, mod[1], moe_w_gate, moe_w_up, moe_w_down, ln_g[1, 1], ln_b[1, 1],
                    layer=1, seq_len=seq_len)
    return xf.reshape(batch, seq_len, d)
```

```python
import functools

import jax
import jax.numpy as jnp
from jax import lax
from jax.experimental import pallas as pl
from jax.experimental.pallas import tpu as pltpu

F32 = jnp.float32
BF16 = jnp.bfloat16

V7X_LANES = 128
V7X_SUBLANES = 8
V7X_VMEM_BYTES = 64 * 1024 * 1024

DEPTH = 2
SSM_HEAD_DIM = 64
SSM_STATE = 128
SSM_GROUPS = 8
SSM_HEADS_PER_GROUP = 4
SSM_CONV = 4
SSM_CHUNK = 256
ATTN_HEADS = 16
ATTN_HEAD_DIM = 64
MOBA_BLOCK = 256
MOBA_TOPK = 3
N_EXPERTS = 32
N_EXPERT_GROUPS = 4
EXPERTS_PER_GROUP = N_EXPERTS // N_EXPERT_GROUPS
DEEPNORM_ALPHA = (2.0 * DEPTH) ** 0.25
LN_EPS = 1e-5
RMS_EPS = 1e-5
NEG_INF = -1e30
LOG2_E = 1.4426950408889634

GROUP_W = SSM_HEADS_PER_GROUP * SSM_HEAD_DIM
CONV_HALO = V7X_SUBLANES
MOE_ROW_TILE = 512
TOKEN_TILE = 512
DMA_UNROLL = 8


def _cparams(semantics, vmem_mib):
    assert vmem_mib * 1024 * 1024 <= V7X_VMEM_BYTES
    return pltpu.CompilerParams(dimension_semantics=semantics, vmem_limit_bytes=vmem_mib * 1024 * 1024)


def _sds(shape, dtype):
    return jax.ShapeDtypeStruct(shape, dtype)


def _dot(a, b):
    return jnp.dot(a, b, preferred_element_type=F32)


def _dot_nt(a, b):
    return lax.dot_general(a, b, (((1,), (1,)), ((), ())), preferred_element_type=F32)


def _dot_tn(a, b):
    return lax.dot_general(a, b, (((0,), (0,)), ((), ())), preferred_element_type=F32)


def _split3(a):
    hi = a.astype(BF16)
    r1 = a - hi.astype(F32)
    mid = r1.astype(BF16)
    lo = (r1 - mid.astype(F32)).astype(BF16)
    return hi, mid, lo


def _silu(x):
    h = 0.5 * x
    return h + h * jnp.tanh(h)


def _layer_norm(v, gamma, beta):
    mu = jnp.mean(v, axis=-1, keepdims=True)
    d = v - mu
    var = jnp.mean(d * d, axis=-1, keepdims=True)
    return d * lax.rsqrt(var + LN_EPS) * gamma + beta


def _ada_kernel(c_ref, w_ref, b_ref, o_ref):
    cs = _silu(c_ref[...])
    o_ref[0] = jnp.dot(cs, w_ref[0], preferred_element_type=F32, precision=lax.Precision.HIGHEST) + b_ref[0]


def _ada_mod(c, w_ada, b_ada):
    depth, d, n = w_ada.shape
    b = c.shape[0]
    tn = 1024
    out = pl.pallas_call(
        _ada_kernel,
        out_shape=_sds((depth, b, n), F32),
        grid=(depth, n // tn),
        in_specs=[pl.BlockSpec((b, d), lambda l, j: (0, 0)),
                  pl.BlockSpec((1, d, tn), lambda l, j: (l, 0, j)),
                  pl.BlockSpec((1, 1, tn), lambda l, j: (l, 0, j))],
        out_specs=pl.BlockSpec((1, b, tn), lambda l, j: (l, 0, j)),
        compiler_params=_cparams(("arbitrary", "arbitrary"), 32),
        name="ada_mod",
    )(c, w_ada, b_ada.reshape(depth, 1, n))
    return out.reshape(depth, b, 6, d)


def _modulate_into(hm_ref, x_ref, mod_ref, sc, sh):
    m = mod_ref[0]
    hm_ref[...] = (x_ref[...] * (1.0 + m[sc:sc + 1, :]) + m[sh:sh + 1, :]).astype(BF16)


def _mm_mod_kernel(x_ref, mod_ref, w_ref, o_ref, hm_ref, *, sc, sh):
    @pl.when(pl.program_id(1) == 0)
    def _():
        _modulate_into(hm_ref, x_ref, mod_ref, sc, sh)

    o_ref[...] = _dot(hm_ref[...], w_ref[...]).astype(o_ref.dtype)


def _mm_mod(x, mod, w, *, sc, sh, seq_len, out_dtype, tm=2048, tn=1536):
    t, k = x.shape
    n = w.shape[1]
    tn = min(tn, n)
    assert seq_len % tm == 0 and n % tn == 0
    tiles_per_batch = seq_len // tm
    return pl.pallas_call(
        functools.partial(_mm_mod_kernel, sc=sc, sh=sh),
        out_shape=_sds((t, n), out_dtype),
        grid=(t // tm, n // tn),
        in_specs=[pl.BlockSpec((tm, k), lambda i, j: (i, 0)),
                  pl.BlockSpec((1, 6, k), lambda i, j: (i // tiles_per_batch, 0, 0)),
                  pl.BlockSpec((k, tn), lambda i, j: (0, j))],
        out_specs=pl.BlockSpec((tm, tn), lambda i, j: (i, j)),
        scratch_shapes=[pltpu.VMEM((tm, k), BF16)],
        compiler_params=_cparams(("arbitrary", "arbitrary"), 48),
        name="mm_mod",
    )(x, mod, w)


def _ssd_kernel(z_ref, xp_ref, bcp_ref, dtr_ref, cw_ref, cb_ref, dtb_ref, alog_ref, dsk_ref, nw_ref,
                o_ref, ubuf, act, state, acum_t):
    lc = SSM_CHUNK
    d_inner = xp_ref.shape[1]
    n_slabs = ubuf.shape[0]
    x_slabs = d_inner // V7X_LANES
    chunk = pl.program_id(1)

    @pl.when(chunk == 0)
    def _():
        ubuf[:, 0:CONV_HALO, :] = jnp.zeros((n_slabs, CONV_HALO, V7X_LANES), F32)
        state[...] = jnp.zeros_like(state)

    def conv_slabs(src_ref, first_slab, count):
        def slab_body(j, carry):
            src_off = pl.multiple_of(j * V7X_LANES, V7X_LANES)
            slab = first_slab + j
            off = pl.multiple_of(slab * V7X_LANES, V7X_LANES)
            u = src_ref[:, pl.ds(src_off, V7X_LANES)].astype(F32)
            ubuf[slab, CONV_HALO:CONV_HALO + lc, :] = u
            acc = cb_ref[:, pl.ds(off, V7X_LANES)] + cw_ref[SSM_CONV - 1:SSM_CONV, pl.ds(off, V7X_LANES)] * u
            for k in range(SSM_CONV - 1):
                tap = CONV_HALO - (SSM_CONV - 1) + k
                acc = acc + cw_ref[k:k + 1, pl.ds(off, V7X_LANES)] * ubuf[slab, tap:tap + lc, :]
            act[:, pl.ds(off, V7X_LANES)] = _silu(acc)
            ubuf[slab, 0:CONV_HALO, :] = u[lc - CONV_HALO:lc, :]
            return carry

        lax.fori_loop(0, count, slab_body, 0)

    conv_slabs(xp_ref, 0, x_slabs)
    conv_slabs(bcp_ref, x_slabs, n_slabs - x_slabs)

    dt = jax.nn.softplus(dtr_ref[...] + dtb_ref[...])
    da = dt * (-LOG2_E * jnp.exp(alog_ref[...]))
    row = lax.broadcasted_iota(jnp.int32, (lc, lc), 0)
    col = lax.broadcasted_iota(jnp.int32, (lc, lc), 1)
    causal = col <= row
    tril = causal.astype(BF16)
    acum = sum(_dot(tril, part) for part in _split3(da))
    acum_t[...] = acum.T
    acum_parts = _split3(acum)
    dt_bf16 = dt.astype(BF16)
    head_of_col = lax.broadcasted_iota(jnp.int32, (lc, GROUP_W), 1) // SSM_HEAD_DIM
    sel_row = lax.broadcasted_iota(jnp.int32, (V7X_LANES, GROUP_W), 0)
    sel_col = lax.broadcasted_iota(jnp.int32, (V7X_LANES, GROUP_W), 1) // SSM_HEAD_DIM

    def group_body(g, carry):
        xo = pl.multiple_of(g * GROUP_W, GROUP_W)
        bo = pl.multiple_of(d_inner + g * SSM_STATE, SSM_STATE)
        co = pl.multiple_of(d_inner + SSM_GROUPS * SSM_STATE + g * SSM_STATE, SSM_STATE)
        x_g = act[:, pl.ds(xo, GROUP_W)]
        b_g = act[:, pl.ds(bo, SSM_STATE)].astype(BF16)
        c_g = act[:, pl.ds(co, SSM_STATE)].astype(BF16)
        sel = (sel_row == SSM_HEADS_PER_GROUP * g + sel_col).astype(BF16)
        ab = sum(_dot(part, sel) for part in acum_parts)
        dtb = _dot(dt_bf16, sel)
        alast = ab[lc - 1:lc, :]
        cb = _dot_nt(c_g, b_g)
        xdt = x_g * dtb
        y = jnp.zeros((lc, GROUP_W), F32)
        for r in range(SSM_HEADS_PER_GROUP):
            acol = ab[:, r * SSM_HEAD_DIM:r * SSM_HEAD_DIM + 1]
            arow = acum_t[pl.ds(SSM_HEADS_PER_GROUP * g + r, 1), :]
            decay = jnp.exp2(jnp.where(causal, acol - arow, -jnp.inf))
            w = (cb * decay).astype(BF16)
            x_r = jnp.where(head_of_col == r, xdt, 0.0).astype(BF16)
            y = y + _dot(w, x_r)
        st = state[:, pl.ds(xo, GROUP_W)]
        y = y + _dot(c_g, st.astype(BF16)) * jnp.exp2(ab)
        to_end = jnp.exp2(alast - ab) * dtb
        xw = (x_g * to_end).astype(BF16)
        state[:, pl.ds(xo, GROUP_W)] = st * jnp.exp2(alast) + _dot_tn(b_g, xw)
        y = y + dsk_ref[:, pl.ds(xo, GROUP_W)] * x_g
        gated = y * _silu(z_ref[:, pl.ds(xo, GROUP_W)].astype(F32))
        ms = jnp.mean(gated * gated, axis=-1, keepdims=True)
        o_ref[:, pl.ds(xo, GROUP_W)] = (gated * lax.rsqrt(ms + RMS_EPS) * nw_ref[:, pl.ds(xo, GROUP_W)]).astype(BF16)
        return carry

    lax.fori_loop(0, SSM_GROUPS, group_body, 0, unroll=8)


def _ssd(proj, dt_raw, conv_w, conv_b, dt_bias, a_log, d_skip, norm_w, *, batch, seq_len):
    t = proj.shape[0]
    d_inner = SSM_GROUPS * GROUP_W
    conv_dim = d_inner + 2 * SSM_GROUPS * SSM_STATE
    assert proj.shape[1] == d_inner + conv_dim and conv_dim == 2 * d_inner
    nc = seq_len // SSM_CHUNK
    lc = SSM_CHUNK
    pad = V7X_LANES - dt_bias.shape[0]
    heads = dt_bias.shape[0]
    row_map = lambda b, c: b * nc + c
    const = lambda b, c: (0, 0)
    return pl.pallas_call(
        _ssd_kernel,
        out_shape=_sds((t, d_inner), BF16),
        grid=(batch, nc),
        in_specs=[pl.BlockSpec((lc, d_inner), lambda b, c: (row_map(b, c), 0)),
                  pl.BlockSpec((lc, d_inner), lambda b, c: (row_map(b, c), 1)),
                  pl.BlockSpec((lc, d_inner), lambda b, c: (row_map(b, c), 2)),
                  pl.BlockSpec((lc, V7X_LANES), lambda b, c: (row_map(b, c), 0)),
                  pl.BlockSpec((SSM_CONV, conv_dim), const),
                  pl.BlockSpec((1, conv_dim), const),
                  pl.BlockSpec((1, V7X_LANES), const),
                  pl.BlockSpec((1, V7X_LANES), const),
                  pl.BlockSpec((1, d_inner), const),
                  pl.BlockSpec((1, d_inner), const)],
        out_specs=pl.BlockSpec((lc, d_inner), lambda b, c: (row_map(b, c), 0)),
        scratch_shapes=[pltpu.VMEM((conv_dim // V7X_LANES, CONV_HALO + lc, V7X_LANES), F32),
                        pltpu.VMEM((lc, conv_dim), F32),
                        pltpu.VMEM((SSM_STATE, d_inner), F32),
                        pltpu.VMEM((V7X_LANES, lc), F32)],
        compiler_params=_cparams(("arbitrary", "arbitrary"), 48),
        name="ssd_chunk_scan",
    )(proj, proj, proj, dt_raw, conv_w, conv_b.reshape(1, conv_dim),
      jnp.pad(dt_bias, (0, pad)).reshape(1, V7X_LANES), jnp.pad(a_log, (0, pad)).reshape(1, V7X_LANES),
      jnp.repeat(d_skip, SSM_HEAD_DIM).reshape(1, heads * SSM_HEAD_DIM), norm_w.reshape(1, d_inner))


def _route(logits_t, carry_ref):
    tm = logits_t.shape[1]
    lg = logits_t[0:N_EXPERTS, :]
    eid = lax.broadcasted_iota(jnp.int32, (N_EXPERTS, tm), 0).astype(F32)
    e = jnp.exp(lg - jnp.max(lg, axis=0, keepdims=True))
    far = float(N_EXPERTS)
    best = None
    for g in range(N_EXPERT_GROUPS):
        eg = e[g * EXPERTS_PER_GROUP:(g + 1) * EXPERTS_PER_GROUP, :]
        ig = (lax.broadcasted_iota(jnp.int32, (EXPERTS_PER_GROUP, tm), 0) + g * EXPERTS_PER_GROUP).astype(F32)
        m1 = jnp.max(eg, axis=0, keepdims=True)
        i1 = jnp.min(jnp.where(eg == m1, ig, far), axis=0, keepdims=True)
        eg2 = jnp.where(ig == i1, -1.0, eg)
        m2 = jnp.max(eg2, axis=0, keepdims=True)
        i2 = jnp.min(jnp.where(eg2 == m2, ig, far), axis=0, keepdims=True)
        cand = (m1 + m2, m1, m2, i1, i2)
        if best is None:
            best = cand
        else:
            take = cand[0] > best[0]
            best = tuple(jnp.where(take, c, b) for c, b in zip(cand, best))
    _, m1, m2, i1, i2 = best
    denom = m1 + m2
    hit1 = eid == i1
    hit2 = eid == i2
    onehot = (hit1 | hit2).astype(F32)
    row = lax.broadcasted_iota(jnp.int32, (tm, tm), 0)
    col = lax.broadcasted_iota(jnp.int32, (tm, tm), 1)
    earlier = (row < col).astype(BF16)
    rank = _dot(onehot.astype(BF16), earlier) + carry_ref[:, 0:1]
    r1 = jnp.sum(jnp.where(hit1, rank, 0.0), axis=0, keepdims=True)
    r2 = jnp.sum(jnp.where(hit2, rank, 0.0), axis=0, keepdims=True)
    carry_ref[...] = carry_ref[...] + jnp.sum(onehot, axis=1, keepdims=True)
    sub = lax.broadcasted_iota(jnp.int32, (V7X_SUBLANES, tm), 0)
    rec_t = jnp.zeros((V7X_SUBLANES, tm), F32)
    for k, val in enumerate((i1, i2, m1 / denom, m2 / denom, r1, r2)):
        rec_t = jnp.where(sub == k, val, rec_t)
    padded = jnp.concatenate([rec_t, jnp.zeros((V7X_LANES - V7X_SUBLANES, tm), F32)], axis=0)
    return padded.T, rec_t


def _proj_ln_route_kernel(a_ref, w_ref, xres_ref, mod_ref, lng_ref, lnb_ref, wr_ref, br_ref,
                          x_ref, route_ref, route_t_ref, cnt_ref, carry_ref, *, gate_idx, sc, sh):
    @pl.when(pl.program_id(0) == 0)
    def _():
        carry_ref[...] = jnp.zeros_like(carry_ref)

    m = mod_ref[0]
    y = _dot(a_ref[...], w_ref[...])
    v = DEEPNORM_ALPHA * xres_ref[...] + (1.0 + m[gate_idx:gate_idx + 1, :]) * y
    x1 = _layer_norm(v, lng_ref[...], lnb_ref[...])
    x_ref[...] = x1
    hm = (x1 * (1.0 + m[sc:sc + 1, :]) + m[sh:sh + 1, :]).astype(BF16)
    logits_t = _dot_nt(wr_ref[...], hm) + br_ref[...]
    route_ref[...], route_t_ref[...] = _route(logits_t, carry_ref)
    cnt_ref[...] = carry_ref[...]


def _proj_ln_route(a, w, xres, mod, ln_g, ln_b, w_router, b_router, *, seq_len):
    t, k = a.shape
    d = w.shape[1]
    tm = TOKEN_TILE
    tiles_per_batch = seq_len // tm
    const = lambda i: (0, 0)
    wr = jnp.pad(w_router.T, ((0, V7X_LANES - N_EXPERTS), (0, 0))).astype(BF16)
    br = jnp.pad(b_router, (0, V7X_LANES - N_EXPERTS)).reshape(V7X_LANES, 1)
    return pl.pallas_call(
        functools.partial(_proj_ln_route_kernel, gate_idx=2, sc=4, sh=3),
        out_shape=(_sds((t, d), F32), _sds((t, V7X_LANES), F32), _sds((V7X_SUBLANES, t), F32),
                   _sds((N_EXPERTS, V7X_LANES), F32)),
        grid=(t // tm,),
        in_specs=[pl.BlockSpec((tm, k), lambda i: (i, 0)),
                  pl.BlockSpec((k, d), const),
                  pl.BlockSpec((tm, d), lambda i: (i, 0)),
                  pl.BlockSpec((1, 6, d), lambda i: (i // tiles_per_batch, 0, 0)),
                  pl.BlockSpec((1, d), const),
                  pl.BlockSpec((1, d), const),
                  pl.BlockSpec((V7X_LANES, d), const),
                  pl.BlockSpec((V7X_LANES, 1), const)],
        out_specs=(pl.BlockSpec((tm, d), lambda i: (i, 0)),
                   pl.BlockSpec((tm, V7X_LANES), lambda i: (i, 0)),
                   pl.BlockSpec((V7X_SUBLANES, tm), lambda i: (0, i)),
                   pl.BlockSpec((N_EXPERTS, V7X_LANES), const)),
        scratch_shapes=[pltpu.VMEM((N_EXPERTS, V7X_LANES), F32)],
        compiler_params=_cparams(("arbitrary",), 40),
        name="proj_ln_route",
    )(a, w, xres, mod, ln_g.reshape(1, d), ln_b.reshape(1, d), wr, br)


def _record_copy(src_ref, src_slot, dst_ref, dst_slot, sem, n):
    src = src_ref.at[pl.ds(pl.multiple_of(src_slot * n, n), n)]
    dst = dst_ref.at[pl.ds(pl.multiple_of(dst_slot * n, n), n)]
    return pltpu.make_async_copy(src, dst, sem)


def _to_records(rec_ref, rows):
    t = rows.shape[0]
    n = rows.shape[1] // V7X_LANES
    for s in range(n):
        rec_ref[pl.ds(s, t, stride=n), :] = rows[:, s * V7X_LANES:(s + 1) * V7X_LANES]


def _from_records(rec_ref, t, n):
    return jnp.concatenate([rec_ref[pl.ds(s, t, stride=n), :] for s in range(n)], axis=1)


def _dispatch_kernel(tail_ref, p1_ref, p2_ref, x_ref, mod_ref, xs_ref, hm_ref, zero_ref, sem, zero_sem, *, sc, sh):
    tt, d = x_ref.shape
    n = d // V7X_LANES
    i = pl.program_id(0)
    n_steps = pl.num_programs(0)
    cur = lax.rem(i, 2)

    @pl.when(i == 0)
    def _():
        zero_ref[...] = jnp.zeros_like(zero_ref)
        rows = zero_ref.shape[0]

        def tail_copy(e):
            start = pl.multiple_of(jnp.maximum(tail_ref[0, e], 0) * n, rows)
            return pltpu.make_async_copy(zero_ref, xs_ref.at[pl.ds(start, rows)], zero_sem)

        def fill(e, carry):
            @pl.when(tail_ref[0, e] >= 0)
            def _():
                tail_copy(e).start()
            return carry

        def fill_wait(e, carry):
            @pl.when(tail_ref[0, e] >= 0)
            def _():
                tail_copy(e).wait()
            return carry

        def spare_copy(j):
            return pltpu.make_async_copy(zero_ref, xs_ref.at[pl.ds(pl.multiple_of(j * rows, rows), rows)], zero_sem)

        def spare_fill(j, carry):
            spare_copy(j).start()
            return carry

        def spare_wait(j, carry):
            spare_copy(j).wait()
            return carry

        n_tiles = xs_ref.shape[0] // rows
        lax.fori_loop(0, N_EXPERTS, fill, 0)
        lax.fori_loop(tail_ref[0, N_EXPERTS], n_tiles, spare_fill, 0)
        lax.fori_loop(0, N_EXPERTS, fill_wait, 0)
        lax.fori_loop(tail_ref[0, N_EXPERTS], n_tiles, spare_wait, 0)

    def drain(buf):
        def body(blk, carry):
            for _ in range(2 * DMA_UNROLL):
                _record_copy(hm_ref.at[buf], 0, xs_ref, 0, sem.at[buf], n).wait()
            return carry

        lax.fori_loop(0, tt // DMA_UNROLL, body, 0)

    @pl.when(i >= 2)
    def _():
        drain(cur)

    m = mod_ref[0]
    _to_records(hm_ref.at[cur], x_ref[...] * (1.0 + m[sc:sc + 1, :]) + m[sh:sh + 1, :])

    def issue(blk, carry):
        for u in range(DMA_UNROLL):
            r = blk * DMA_UNROLL + u
            _record_copy(hm_ref.at[cur], r, xs_ref, p1_ref[0, r], sem.at[cur], n).start(priority=0)
            _record_copy(hm_ref.at[cur], r, xs_ref, p2_ref[0, r], sem.at[cur], n).start(priority=1)
        return carry

    lax.fori_loop(0, tt // DMA_UNROLL, issue, 0)

    @pl.when(i == n_steps - 1)
    def _():
        drain(cur)

    @pl.when(jnp.logical_and(i == n_steps - 1, i >= 1))
    def _():
        drain(1 - cur)


def _slot_spec(tt, copy):
    return pl.BlockSpec((None, None, 1, tt), lambda i: (i, copy, 0, 0), memory_space=pltpu.SMEM)


def _dispatch(x, mod, slots, tail_slot, n_slots, *, seq_len):
    t, d = x.shape
    n = d // V7X_LANES
    tt = TOKEN_TILE
    tiles_per_batch = seq_len // tt
    return pl.pallas_call(
        functools.partial(_dispatch_kernel, sc=4, sh=3),
        out_shape=_sds((n_slots * n, V7X_LANES), F32),
        grid=(t // tt,),
        in_specs=[pl.BlockSpec((1, N_EXPERTS + 1), lambda i: (0, 0), memory_space=pltpu.SMEM),
                  _slot_spec(tt, 0), _slot_spec(tt, 1),
                  pl.BlockSpec((tt, d), lambda i: (i, 0)),
                  pl.BlockSpec((1, 6, d), lambda i: (i // tiles_per_batch, 0, 0))],
        out_specs=pl.BlockSpec(memory_space=pl.ANY),
        scratch_shapes=[pltpu.VMEM((2, tt * n, V7X_LANES), F32), pltpu.VMEM((MOE_ROW_TILE * n, V7X_LANES), F32),
                        pltpu.SemaphoreType.DMA((2,)), pltpu.SemaphoreType.DMA(())],
        compiler_params=_cparams(("arbitrary",), 32),
        name="moe_dispatch",
    )(tail_slot.reshape(1, N_EXPERTS + 1), slots, slots, x, mod)


def _ffn_kernel(te_ref, nu_ref, x_ref, wg_ref, wu_ref, wd_ref, o_ref, wg_bf, wu_bf, wd_bf):
    i = pl.program_id(0)
    d, f = wg_bf.shape
    n = d // V7X_LANES
    tm = x_ref.shape[0] // n
    used = i < nu_ref[0]
    new_expert = jnp.logical_or(i == 0, te_ref[i] != te_ref[jnp.maximum(i - 1, 0)])

    @pl.when(jnp.logical_and(used, new_expert))
    def _():
        wg_bf[...] = wg_ref[0, 0].astype(BF16)
        wu_bf[...] = wu_ref[0, 0].astype(BF16)
        wd_bf[...] = wd_ref[0, 0].astype(BF16)

    @pl.when(used)
    def _():
        x = _from_records(x_ref, tm, n).astype(BF16)
        hg = _dot(x, wg_bf[...])
        hu = _dot(x, wu_bf[...])
        _to_records(o_ref, _dot((_silu(hg) * hu).astype(BF16), wd_bf[...]))

    @pl.when(jnp.logical_not(used))
    def _():
        o_ref[...] = jnp.zeros_like(o_ref)


def _ffn(xs, tile_expert, n_used, w_gate, w_up, w_down, *, layer):
    _, _, d, f = w_gate.shape
    n = d // V7X_LANES
    tm = MOE_ROW_TILE
    w_map = lambda i, te, nu: (layer, te[i], 0, 0)
    grid_spec = pltpu.PrefetchScalarGridSpec(
        num_scalar_prefetch=2,
        grid=(xs.shape[0] // (tm * n),),
        in_specs=[pl.BlockSpec((tm * n, V7X_LANES), lambda i, te, nu: (jnp.minimum(i, nu[0] - 1), 0)),
                  pl.BlockSpec((1, 1, d, f), w_map),
                  pl.BlockSpec((1, 1, d, f), w_map),
                  pl.BlockSpec((1, 1, f, d), w_map)],
        out_specs=pl.BlockSpec((tm * n, V7X_LANES), lambda i, te, nu: (i, 0)),
        scratch_shapes=[pltpu.VMEM((d, f), BF16), pltpu.VMEM((d, f), BF16), pltpu.VMEM((f, d), BF16)],
    )
    return pl.pallas_call(
        _ffn_kernel,
        out_shape=_sds(xs.shape, F32),
        grid_spec=grid_spec,
        compiler_params=_cparams(("arbitrary",), 48),
        name="moe_ffn",
    )(tile_expert, n_used, xs, w_gate, w_up, w_down)


def _combine_ln_kernel(p1_ref, p2_ref, p1_next_ref, p2_next_ref, xres_ref, route_ref, mod_ref, lng_ref, lnb_ref,
                       ye_ref, o_ref, buf, sem, *, gate_idx):
    tt, d = xres_ref.shape
    n = d // V7X_LANES
    i = pl.program_id(0)
    cur = lax.rem(i, 2)

    def fetch(pa_ref, pb_ref, slot):
        def body(blk, carry):
            for u in range(DMA_UNROLL):
                r = blk * DMA_UNROLL + u
                _record_copy(ye_ref, pa_ref[0, r], buf.at[slot, 0], r, sem.at[slot], n).start(priority=0)
                _record_copy(ye_ref, pb_ref[0, r], buf.at[slot, 1], r, sem.at[slot], n).start(priority=1)
            return carry

        lax.fori_loop(0, tt // DMA_UNROLL, body, 0)

    @pl.when(i == 0)
    def _():
        fetch(p1_ref, p2_ref, 0)

    @pl.when(i + 1 < pl.num_programs(0))
    def _():
        fetch(p1_next_ref, p2_next_ref, 1 - cur)

    def drain(blk, carry):
        for _ in range(2 * DMA_UNROLL):
            _record_copy(ye_ref, 0, buf.at[cur, 0], 0, sem.at[cur], n).wait()
        return carry

    lax.fori_loop(0, tt // DMA_UNROLL, drain, 0)
    m = mod_ref[0]
    rec = route_ref[...]
    y = (rec[:, 2:3] * _from_records(buf.at[cur, 0], tt, n) + rec[:, 3:4] * _from_records(buf.at[cur, 1], tt, n))
    v = DEEPNORM_ALPHA * xres_ref[...] + (1.0 + m[gate_idx:gate_idx + 1, :]) * y
    o_ref[...] = _layer_norm(v, lng_ref[...], lnb_ref[...])


def _combine_ln(xres, route, slots, ye, mod, ln_g, ln_b, *, seq_len):
    t, d = xres.shape
    n = d // V7X_LANES
    tt = TOKEN_TILE
    tiles_per_batch = seq_len // tt
    n_steps = t // tt
    const = lambda i: (0, 0)

    def next_slot_spec(copy):
        return pl.BlockSpec((None, None, 1, tt), lambda i: (jnp.minimum(i + 1, n_steps - 1), copy, 0, 0),
                            memory_space=pltpu.SMEM)

    return pl.pallas_call(
        functools.partial(_combine_ln_kernel, gate_idx=5),
        out_shape=_sds((t, d), F32),
        grid=(n_steps,),
        in_specs=[_slot_spec(tt, 0), _slot_spec(tt, 1), next_slot_spec(0), next_slot_spec(1),
                  pl.BlockSpec((tt, d), lambda i: (i, 0)),
                  pl.BlockSpec((tt, V7X_LANES), lambda i: (i, 0)),
                  pl.BlockSpec((1, 6, d), lambda i: (i // tiles_per_batch, 0, 0)),
                  pl.BlockSpec((1, d), const),
                  pl.BlockSpec((1, d), const),
                  pl.BlockSpec(memory_space=pl.ANY)],
        out_specs=pl.BlockSpec((tt, d), lambda i: (i, 0)),
        scratch_shapes=[pltpu.VMEM((2, 2, tt * n, V7X_LANES), F32), pltpu.SemaphoreType.DMA((2,))],
        compiler_params=_cparams(("arbitrary",), 32),
        name="moe_combine_ln",
    )(slots, slots, slots, slots, xres, route, mod, ln_g.reshape(1, d), ln_b.reshape(1, d), ye)


def _slot_kernel(route_t_ref, off_ref, o_ref):
    tt = o_ref.shape[-1]
    n = route_t_ref.shape[1]
    eid = lax.broadcasted_iota(jnp.int32, (N_EXPERTS, n), 0).astype(F32)
    for copy in range(2):
        expert = route_t_ref[copy:copy + 1, :]
        rank = route_t_ref[4 + copy:5 + copy, :]
        off = jnp.sum(jnp.where(eid == expert, off_ref[:, 0:1], 0.0), axis=0, keepdims=True)
        slot = (off + rank).astype(jnp.int32)
        for s in range(o_ref.shape[0]):
            o_ref[s, copy] = slot[:, s * tt:(s + 1) * tt]


def _slots(route_t, row_off):
    t = route_t.shape[1]
    tt = TOKEN_TILE
    tiles_per_step = 4
    assert t % (tt * tiles_per_step) == 0
    off = jnp.broadcast_to(row_off.astype(F32)[:, None], (N_EXPERTS, V7X_LANES))
    return pl.pallas_call(
        _slot_kernel,
        out_shape=_sds((t // tt, 2, 1, tt), jnp.int32),
        grid=(t // (tt * tiles_per_step),),
        in_specs=[pl.BlockSpec((V7X_SUBLANES, tt * tiles_per_step), lambda i: (0, i)),
                  pl.BlockSpec((N_EXPERTS, V7X_LANES), lambda i: (0, 0))],
        out_specs=pl.BlockSpec((tiles_per_step, 2, 1, tt), lambda i: (i, 0, 0, 0)),
        compiler_params=_cparams(("arbitrary",), 32),
        name="moe_slots",
    )(route_t, off)


def _moe_tables(counts, n_tokens):
    tm = MOE_ROW_TILE
    max_tiles = (2 * n_tokens) // tm + N_EXPERTS
    cnt = counts[:, 0].astype(jnp.int32)
    tiles_e = (cnt + tm - 1) // tm
    tile_end = jnp.cumsum(tiles_e)
    row_off = (tile_end - tiles_e) * tm
    n_used = tile_end[-1:]
    tail_slot = jnp.concatenate([jnp.where(tiles_e > 0, (tile_end - 1) * tm, -1), n_used]).astype(jnp.int32)
    tile_ids = jnp.minimum(jnp.arange(max_tiles, dtype=jnp.int32), n_used - 1)
    tile_expert = jnp.sum(tile_ids[:, None] >= tile_end[None, :], axis=1).astype(jnp.int32)
    return row_off, tail_slot, tile_expert, n_used.astype(jnp.int32), max_tiles * tm


def _moe_layer(x1, routing, mod, w_gate, w_up, w_down, ln_g, ln_b, *, layer, seq_len):
    route, route_t, counts = routing
    row_off, tail_slot, tile_expert, n_used, n_slots = _moe_tables(counts, route.shape[0])
    slots = _slots(route_t, row_off)
    xs = _dispatch(x1, mod, slots, tail_slot, n_slots, seq_len=seq_len)
    ye = _ffn(xs, tile_expert, n_used, w_gate, w_up, w_down, layer=layer)
    return _combine_ln(x1, route, slots, ye, mod, ln_g, ln_b, seq_len=seq_len)


def _qkv_body(x_ref, mod_ref, w_ref, o_ref, hm_ref, *, sc, sh, tiles_per_batch, tiles_per_kind):
    @pl.when(pl.program_id(1) == 0)
    def _():
        _modulate_into(hm_ref, x_ref, mod_ref, sc, sh)

    tm = x_ref.shape[0]
    kind = pl.program_id(1) // tiles_per_kind
    lane = lax.broadcasted_iota(jnp.int32, (tm, V7X_LANES), 1)
    low = lane < ATTN_HEAD_DIM
    row = lax.broadcasted_iota(jnp.int32, (tm, V7X_LANES), 0)
    pos = lax.rem(pl.program_id(0), tiles_per_batch) * tm + row
    k_extra = (lane == ATTN_HEAD_DIM + lax.div(pos, MOBA_BLOCK)).astype(F32)
    v_extra = (lane == ATTN_HEAD_DIM).astype(F32)
    extra = jnp.where(kind == 1, k_extra, jnp.where(kind == 2, v_extra, 0.0))
    half = w_ref.shape[1] // 2
    for c in range(2):
        acc = _dot(hm_ref[...], w_ref[:, c * half:(c + 1) * half])
        for p in range(half // V7X_LANES):
            t2 = acc[:, p * V7X_LANES:(p + 1) * V7X_LANES]
            even = jnp.where(low, t2, extra)
            odd = jnp.where(low, pltpu.roll(t2, ATTN_HEAD_DIM, axis=1), extra)
            head = 2 * (c * (half // V7X_LANES) + p)
            o_ref[0, 0, head] = even.astype(BF16)
            o_ref[0, 0, head + 1] = odd.astype(BF16)


def _qkv(x, mod, w, *, batch, seq_len, tm=2048, tn=512):
    t, k = x.shape
    n = w.shape[1]
    heads_per_tile = tn // ATTN_HEAD_DIM
    tiles_per_batch = seq_len // tm
    tiles_per_kind = ATTN_HEADS // heads_per_tile
    assert n == 3 * ATTN_HEADS * ATTN_HEAD_DIM and seq_len % tm == 0
    return pl.pallas_call(
        functools.partial(_qkv_body, sc=1, sh=0, tiles_per_batch=tiles_per_batch, tiles_per_kind=tiles_per_kind),
        out_shape=_sds((3, batch, ATTN_HEADS, seq_len, V7X_LANES), BF16),
        grid=(t // tm, n // tn),
        in_specs=[pl.BlockSpec((tm, k), lambda i, j: (i, 0)),
                  pl.BlockSpec((1, 6, k), lambda i, j: (i // tiles_per_batch, 0, 0)),
                  pl.BlockSpec((k, tn), lambda i, j: (0, j))],
        out_specs=pl.BlockSpec((1, 1, heads_per_tile, tm, V7X_LANES),
                               lambda i, j: (j // tiles_per_kind, i // tiles_per_batch, j % tiles_per_kind,
                                             i % tiles_per_batch, 0)),
        scratch_shapes=[pltpu.VMEM((tm, k), BF16)],
        compiler_params=_cparams(("arbitrary", "arbitrary"), 40),
        name="qkv_proj",
    )(x, mod, w)


ATTN_HEADS_PER_STEP = 2
ATTN_Q_TILE = 1024
GATE_ROWS = 512


def _attn_kernel(q_ref, k_ref, v_ref, pm_ref, o_ref, km_ref, qa_ref):
    tq = ATTN_Q_TILE
    seq_len = q_ref.shape[2]
    n_blocks = seq_len // MOBA_BLOCK
    heads = range(ATTN_HEADS_PER_STEP)
    qi = pl.program_id(2)

    @pl.when(qi == 0)
    def _():
        for hh in heads:
            km_ref[hh] = _dot(pm_ref[...], k_ref[0, hh]).astype(BF16)
        nb_pad = -(-n_blocks // V7X_SUBLANES) * V7X_SUBLANES
        blk_t = lax.broadcasted_iota(jnp.int32, (nb_pad, GATE_ROWS), 0)
        pos_t = lax.broadcasted_iota(jnp.int32, (nb_pad, GATE_ROWS), 1)

        def gate_rows(ci, carry, *, select):
            r0 = pl.multiple_of(ci * GATE_ROWS, GATE_ROWS)
            own_blk = lax.div(r0 + pos_t, MOBA_BLOCK)
            past = blk_t < own_blk
            for hh in heads:
                q = q_ref[0, hh, pl.ds(r0, GATE_ROWS), :]
                chosen = past
                if select:
                    gates = _dot_nt(km_ref[hh], q)[ATTN_HEAD_DIM:ATTN_HEAD_DIM + nb_pad, :]
                    gates = jnp.where(past, gates, -jnp.inf)
                    rank = jnp.zeros((nb_pad, GATE_ROWS), F32)
                    for i in range(n_blocks):
                        g_i = gates[i:i + 1, :]
                        beats = (g_i > gates) | ((g_i == gates) & (blk_t > i))
                        rank = rank + beats.astype(F32)
                    chosen = past & (rank < MOBA_TOPK)
                mask_t = jnp.where(chosen | (blk_t == own_blk), 0.0, NEG_INF)
                mask_t = jnp.concatenate([jnp.zeros((ATTN_HEAD_DIM, GATE_ROWS), F32), mask_t,
                                          jnp.zeros((V7X_LANES - ATTN_HEAD_DIM - nb_pad, GATE_ROWS), F32)], axis=0)
                qa_ref[hh, pl.ds(r0, GATE_ROWS), :] = (q.astype(F32) + mask_t.T).astype(BF16)
            return carry

        keep_all = min(seq_len, (MOBA_TOPK + 1) * MOBA_BLOCK) // GATE_ROWS
        lax.fori_loop(0, keep_all, functools.partial(gate_rows, select=False), 0)
        lax.fori_loop(keep_all, seq_len // GATE_ROWS, functools.partial(gate_rows, select=True), 0)

    q0 = pl.multiple_of(qi * tq, tq)

    def online_update(m_i, acc, s, v):
        m_new = jnp.maximum(m_i, jnp.max(s, axis=-1, keepdims=True))
        p = jnp.exp(s - m_new)
        return m_new, acc * jnp.exp(m_i - m_new) + _dot(p.astype(BF16), v)

    def kv_step(j, carry):
        start = pl.multiple_of(j * tq, tq)
        out = []
        for hh in heads:
            m_i, acc = carry[hh]
            s = _dot_nt(qa_ref[hh, pl.ds(q0, tq), :], k_ref[0, hh, pl.ds(start, tq), :])
            out.append(online_update(m_i, acc, s, v_ref[0, hh, pl.ds(start, tq), :]))
        return tuple(out)

    init = tuple((jnp.full((tq, 1), NEG_INF, F32), jnp.zeros((tq, V7X_LANES), F32)) for _ in heads)
    carry = lax.fori_loop(0, qi, kv_step, init)

    row = lax.broadcasted_iota(jnp.int32, (tq, tq), 0)
    col = lax.broadcasted_iota(jnp.int32, (tq, tq), 1)
    outs = []
    for hh in heads:
        m_i, acc = carry[hh]
        s = _dot_nt(qa_ref[hh, pl.ds(q0, tq), :], k_ref[0, hh, pl.ds(q0, tq), :])
        _, acc = online_update(m_i, acc, jnp.where(col <= row, s, NEG_INF), v_ref[0, hh, pl.ds(q0, tq), :])
        outs.append(acc / acc[:, ATTN_HEAD_DIM:ATTN_HEAD_DIM + 1])
    lane = lax.broadcasted_iota(jnp.int32, (tq, V7X_LANES), 1)
    for pair in range(ATTN_HEADS_PER_STEP // 2):
        both = jnp.where(lane < ATTN_HEAD_DIM, outs[2 * pair], pltpu.roll(outs[2 * pair + 1], ATTN_HEAD_DIM, axis=1))
        o_ref[0, :, pair * V7X_LANES:(pair + 1) * V7X_LANES] = both.astype(BF16)


def _attention(qkv, *, batch, seq_len):
    nb = seq_len // MOBA_BLOCK
    tq = ATTN_Q_TILE
    hps = ATTN_HEADS_PER_STEP
    assert hps % 2 == 0 and 2 * ATTN_HEAD_DIM == V7X_LANES and nb <= V7X_LANES - ATTN_HEAD_DIM
    assert seq_len % GATE_ROWS == 0 and seq_len % tq == 0 and tq % MOBA_BLOCK == 0
    rows = jnp.arange(V7X_LANES, dtype=jnp.int32)[:, None] - ATTN_HEAD_DIM
    cols = jnp.arange(seq_len, dtype=jnp.int32)[None, :] // MOBA_BLOCK
    pool = jnp.where(rows == cols, 1.0 / MOBA_BLOCK, 0.0).astype(BF16)
    def head_spec(kind):
        return pl.BlockSpec((None, 1, hps, seq_len, V7X_LANES), lambda b, h, i: (kind, b, h, 0, 0))

    return pl.pallas_call(
        _attn_kernel,
        out_shape=_sds((batch, seq_len, ATTN_HEADS * ATTN_HEAD_DIM), BF16),
        grid=(batch, ATTN_HEADS // hps, seq_len // tq),
        in_specs=[head_spec(0), head_spec(1), head_spec(2),
                  pl.BlockSpec((V7X_LANES, seq_len), lambda b, h, i: (0, 0))],
        out_specs=pl.BlockSpec((1, tq, hps * ATTN_HEAD_DIM), lambda b, h, i: (b, i, h)),
        scratch_shapes=[pltpu.VMEM((hps, V7X_LANES, V7X_LANES), BF16),
                        pltpu.VMEM((hps, seq_len, V7X_LANES), BF16)],
        compiler_params=_cparams(("arbitrary", "arbitrary", "arbitrary"), 48),
        name="moba_attention",
    )(qkv, qkv, qkv, pool)


def kernel(x, c, w_ada, b_ada, ln_g, ln_b, ssm_w_in, ssm_conv_w, ssm_conv_b, ssm_dt_bias, ssm_a_log, ssm_d,
           ssm_norm_w, ssm_w_out, attn_w_qkv, attn_w_o, w_router, b_router, moe_w_gate, moe_w_up, moe_w_down):
    batch, seq_len, d = x.shape
    assert seq_len % 512 == 0 and seq_len % SSM_CHUNK == 0 and seq_len % MOBA_BLOCK == 0
    t = batch * seq_len
    mod = _ada_mod(c, w_ada, b_ada)
    xf = x.reshape(t, d)

    d_inner = SSM_GROUPS * GROUP_W
    conv_dim = d_inner + 2 * SSM_GROUPS * SSM_STATE
    w_in = ssm_w_in[0]
    heads = ssm_dt_bias.shape[1]
    w_main = w_in[:, :d_inner + conv_dim].astype(BF16)
    w_dt = jnp.pad(w_in[:, d_inner + conv_dim:], ((0, 0), (0, V7X_LANES - heads))).astype(BF16)
    proj = _mm_mod(xf, mod[0], w_main, sc=1, sh=0, seq_len=seq_len, out_dtype=BF16)
    dt_raw = _mm_mod(xf, mod[0], w_dt, sc=1, sh=0, seq_len=seq_len, out_dtype=F32)
    y = _ssd(proj, dt_raw, ssm_conv_w[0], ssm_conv_b[0], ssm_dt_bias[0], ssm_a_log[0], ssm_d[0], ssm_norm_w[0],
             batch=batch, seq_len=seq_len)
    x1, *routing = _proj_ln_route(y, ssm_w_out[0].astype(BF16), xf, mod[0], ln_g[0, 0], ln_b[0, 0],
                                       w_router, b_router, seq_len=seq_len)
    xf = _moe_layer(x1, routing, mod[0], moe_w_gate, moe_w_up, moe_w_down, ln_g[0, 1], ln_b[0, 1],
                    layer=0, seq_len=seq_len)

    assert ATTN_HEAD_DIM ** -0.5 == 0.125
    col_scale = jnp.where(jnp.arange(3 * d) < d, ATTN_HEAD_DIM ** -0.5, 1.0)
    w_qkv = (attn_w_qkv[0] * col_scale).astype(BF16)
    qkv = _qkv(xf, mod[1], w_qkv, batch=batch, seq_len=seq_len)
    o = _attention(qkv, batch=batch, seq_len=seq_len).reshape(t, d)
    x1, *routing = _proj_ln_route(o, attn_w_o[0].astype(BF16), xf, mod[1], ln_g[1, 0], ln_b[1, 0],
                                       w_router, b_router, seq_len=seq_len)
    xf = _moe_layer(x1, routing, mod[1], moe_w_gate, moe_w_up, moe_w_down, ln_g[1, 1], ln_b[1, 1],
                    layer=1, seq_len=seq_len)
    return xf.reshape(batch, seq_len, d)
```
